```python
import math
import jax, jax.numpy as jnp
from jax import lax
import numpy as np

D_MODEL = 1024
BATCH = 8
SEQ = 2048
DEPTH = 1
DEC_BATCH = 128
DEC_SEQ = 8
PAST_LEN = 16384
PAGE_SIZE = 128

D_MIX = D_MODEL
HG_HEADS = 4
HG_DK = 128
HG_DV = D_MIX // 2 // HG_HEADS
ML_HEADS = 4
ML_DK = 128
ML_DV = D_MIX // 2 // ML_HEADS
HG_KW = HG_HEADS * HG_DK
HG_W = HG_HEADS * HG_DV
ML_KW = ML_HEADS * ML_DK
ML_W = ML_HEADS * ML_DV
D_FF = 2816
CONV_W = 3
CHUNK = 64
EPS = 1e-6
IN_SIZES = (HG_KW, HG_KW, HG_W, HG_W, ML_KW, ML_KW, ML_W, ML_W, ML_HEADS, ML_HEADS)
IN_COLS = 2 * HG_KW + 2 * HG_W + 2 * ML_KW + 2 * ML_W + 2 * ML_HEADS

kernel_name = 'hymba_hgrn2_mlstm_convffn_step'


def _split_points():
    pts, acc = [], 0
    for s in IN_SIZES[:-1]:
        acc += s
        pts.append(acc)
    return pts


def _rmsnorm(x, g):
    xf = x.astype(jnp.float32)
    y = xf * lax.rsqrt(jnp.mean(xf * xf, axis=-1, keepdims=True) + EPS)
    return (y * g.astype(jnp.float32)).astype(x.dtype)


def _chunk_len(T):
    return max(d for d in range(1, min(CHUNK, T) + 1) if T % d == 0)


def _to_chunks(a, L):
    B, T, H, d = a.shape
    return a.reshape(B, T // L, L, H, d).transpose(1, 0, 3, 2, 4)


def _from_chunks(a):
    N, B, H, L, d = a.shape
    return a.transpose(1, 0, 3, 2, 4).reshape(B, N * L, H, d)


def _hgrn2_chunked(q, logf, k, v, S0):
    T = q.shape[1]
    L = _chunk_len(T)
    mask = jnp.tril(jnp.ones((L, L), dtype=bool))

    def body(S, inp):
        qc, gc, kc, vc = inp
        b = jnp.cumsum(gc, axis=2)
        inter = jnp.einsum('bhtk,bhkv->bhtv', qc * jnp.exp(b), S)
        diff = b[:, :, :, None, :] - b[:, :, None, :, :]
        decay = jnp.exp(jnp.where(mask[:, :, None], diff, -jnp.inf))
        A = jnp.einsum('bhtk,bhtsk,bhsk->bhts', qc, decay, kc)
        intra = jnp.einsum('bhts,bhsv->bhtv', A, vc)
        b_last = b[:, :, -1]
        S_new = jnp.exp(b_last)[..., None] * S + jnp.einsum(
            'bhsk,bhsv->bhkv', kc * jnp.exp(b_last[:, :, None] - b), vc)
        return S_new, inter + intra

    S_end, o = lax.scan(body, S0, (_to_chunks(q, L), _to_chunks(logf, L),
                                   _to_chunks(k, L), _to_chunks(v, L)))
    return _from_chunks(o), S_end


def _mlstm_chunked(q, k, v, ig, lf, C0, n0, m0):
    T = q.shape[1]
    L = _chunk_len(T)
    mask = jnp.tril(jnp.ones((L, L), dtype=bool))

    def body(carry, inp):
        C, n, m = carry
        qc, kc, vc, ic, fc = inp
        ic = ic[..., 0]
        fc = fc[..., 0]
        a = jnp.cumsum(fc, axis=-1)
        logD = a[..., :, None] - a[..., None, :] + ic[..., None, :]
        logD = jnp.where(mask, logD, -jnp.inf)
        log_inter = a + m[..., None]
        m_t = jnp.maximum(log_inter, jnp.max(logD, axis=-1))
        Dw = jnp.exp(logD - m_t[..., None])
        wi = jnp.exp(log_inter - m_t)
        Sw = jnp.einsum('bhtk,bhsk->bhts', qc, kc) * Dw
        num = wi[..., None] * jnp.einsum('bhtk,bhkv->bhtv', qc, C) + jnp.einsum('bhts,bhsv->bhtv', Sw, vc)
        den = wi * jnp.einsum('bhtk,bhk->bht', qc, n) + jnp.sum(Sw, axis=-1)
        h = num / jnp.maximum(jnp.abs(den), jnp.exp(-m_t))[..., None]
        a_L = a[..., -1]
        log_end = a_L[..., None] - a + ic
        m_new = jnp.maximum(a_L + m, jnp.max(log_end, axis=-1))
        w_end = jnp.exp(log_end - m_new[..., None])
        f_end = jnp.exp(a_L + m - m_new)
        C_new = f_end[..., None, None] * C + jnp.einsum('bhs,bhsk,bhsv->bhkv', w_end, kc, vc)
        n_new = f_end[..., None] * n + jnp.einsum('bhs,bhsk->bhk', w_end, kc)
        return (C_new, n_new, m_new), h

    (C_e, n_e, m_e), h = lax.scan(body, (C0, n0, m0), (
        _to_chunks(q, L), _to_chunks(k, L), _to_chunks(v, L),
        _to_chunks(ig[..., None], L), _to_chunks(lf[..., None], L)))
    return _from_chunks(h), C_e, n_e, m_e


def _layer(x, S_hg, C_ml, n_ml, m_ml, conv_buf, lb, norm1_g, w_in, hg_norm_g, ml_b_ig,
           ml_b_fg, ml_norm_g, w_out, norm2_g, w_gate, w_val, conv_w, conv_b, w_down):
    B, T, _ = x.shape
    f32 = jnp.float32
    h = _rmsnorm(x, norm1_g)
    proj = h @ w_in
    hq, hf, hi, hgate, mq, mk, mv, mo, mig, mfg = jnp.split(proj, _split_points(), axis=-1)

    f = lb + (1.0 - lb) * jax.nn.sigmoid(hf.astype(f32))
    logf = jnp.log(f).reshape(B, T, HG_HEADS, HG_DK)
    k_hg = (1.0 - f).reshape(B, T, HG_HEADS, HG_DK)
    o_hg, S_new = _hgrn2_chunked(hq.astype(f32).reshape(B, T, HG_HEADS, HG_DK), logf, k_hg,
                                 hi.astype(f32).reshape(B, T, HG_HEADS, HG_DV), S_hg.astype(f32))
    o_hg = _rmsnorm(o_hg, hg_norm_g.reshape(HG_HEADS, HG_DV)) * jax.nn.silu(
        hgate.astype(f32).reshape(B, T, HG_HEADS, HG_DV))
    o_hg = o_hg.reshape(B, T, HG_W).astype(x.dtype)

    q_ml = mq.astype(f32).reshape(B, T, ML_HEADS, ML_DK)
    k_ml = mk.astype(f32).reshape(B, T, ML_HEADS, ML_DK) * (ML_DK ** -0.5)
    v_ml = mv.astype(f32).reshape(B, T, ML_HEADS, ML_DV)
    ig = (mig + ml_b_ig).astype(f32)
    lf = jax.nn.log_sigmoid((mfg + ml_b_fg).astype(f32))
    o_ml, C_new, n_new, m_new = _mlstm_chunked(q_ml, k_ml, v_ml, ig, lf, C_ml.astype(f32),
                                               n_ml.astype(f32), m_ml.astype(f32))
    o_ml = _rmsnorm(o_ml, ml_norm_g.reshape(ML_HEADS, ML_DV)) * jax.nn.sigmoid(
        mo.astype(f32).reshape(B, T, ML_HEADS, ML_DV))
    o_ml = o_ml.reshape(B, T, ML_W).astype(x.dtype)

    x = x + jnp.concatenate([o_hg, o_ml], axis=-1) @ w_out

    h2 = _rmsnorm(x, norm2_g)
    u = h2 @ w_gate
    val = h2 @ w_val
    upad = jnp.concatenate([conv_buf.astype(u.dtype), u], axis=1)
    conv = conv_b
    for j in range(CONV_W):
        conv = conv + conv_w[j] * upad[:, j:j + T]
    x = x + (jax.nn.gelu(conv) * val) @ w_down
    new_buf = upad[:, T:]
    dt = x.dtype
    return x, S_new.astype(dt), C_new.astype(dt), n_new.astype(dt), m_new.astype(dt), new_buf.astype(dt)


def _trunk(x, S_hg, C_ml, n_ml, m_ml, conv_buf, norm1_g, w_in, hg_lb_logits, hg_norm_g, ml_b_ig,
           ml_b_fg, ml_norm_g, w_out, norm2_g, w_gate, w_val, conv_w, conv_b, w_down, final_norm_g):
    lb_all = jnp.cumsum(jax.nn.softmax(hg_lb_logits.astype(jnp.float32), axis=0), axis=0)
    Ss, Cs, ns, ms, bufs = [], [], [], [], []
    for l in range(DEPTH):
        x, s, c, n, m, b = _layer(x, S_hg[l], C_ml[l], n_ml[l], m_ml[l], conv_buf[l], lb_all[l],
                                  norm1_g[l], w_in[l], hg_norm_g[l], ml_b_ig[l], ml_b_fg[l],
                                  ml_norm_g[l], w_out[l], norm2_g[l], w_gate[l], w_val[l],
                                  conv_w[l], conv_b[l], w_down[l])
        Ss.append(s); Cs.append(c); ns.append(n); ms.append(m); bufs.append(b)
    y = _rmsnorm(x, final_norm_g)
    return y, jnp.stack(Ss), jnp.stack(Cs), jnp.stack(ns), jnp.stack(ms), jnp.stack(bufs)


def setup_inputs(seed: int = 0) -> dict:
    key = jax.random.key(seed)
    ks = jax.random.split(key, 24)
    nrm = jax.random.normal
    f32 = jnp.float32
    return {
        'x_prompt': nrm(ks[0], (BATCH, SEQ, D_MODEL), f32),
        'x_sample': nrm(ks[1], (DEC_BATCH, DEC_SEQ, D_MODEL), f32),
        'state_hgrn_S': 0.5 * nrm(ks[2], (DEPTH, DEC_BATCH, HG_HEADS, HG_DK, HG_DV), f32),
        'state_mlstm_C': 0.5 * nrm(ks[3], (DEPTH, DEC_BATCH, ML_HEADS, ML_DK, ML_DV), f32),
        'state_mlstm_n': 0.5 * nrm(ks[4], (DEPTH, DEC_BATCH, ML_HEADS, ML_DK), f32),
        'state_mlstm_m': nrm(ks[5], (DEPTH, DEC_BATCH, ML_HEADS), f32),
        'state_conv': nrm(ks[6], (DEPTH, DEC_BATCH, CONV_W - 1, D_FF), f32),
        'norm1_g': 1.0 + 0.02 * nrm(ks[7], (DEPTH, D_MODEL), f32),
        'w_in': nrm(ks[8], (DEPTH, D_MODEL, IN_COLS), f32) * D_MODEL ** -0.5,
        'hg_lb_logits': 0.1 * nrm(ks[9], (DEPTH + 1, HG_KW), f32),
        'hg_norm_g': 1.0 + 0.02 * nrm(ks[10], (DEPTH, HG_W), f32),
        'ml_b_ig': 0.1 * nrm(ks[11], (DEPTH, ML_HEADS), f32),
        'ml_b_fg': 3.0 + 0.1 * nrm(ks[12], (DEPTH, ML_HEADS), f32),
        'ml_norm_g': 1.0 + 0.02 * nrm(ks[13], (DEPTH, ML_W), f32),
        'w_out': nrm(ks[14], (DEPTH, D_MIX, D_MODEL), f32) * D_MIX ** -0.5,
        'norm2_g': 1.0 + 0.02 * nrm(ks[15], (DEPTH, D_MODEL), f32),
        'w_gate': nrm(ks[16], (DEPTH, D_MODEL, D_FF), f32) * D_MODEL ** -0.5,
        'w_val': nrm(ks[17], (DEPTH, D_MODEL, D_FF), f32) * D_MODEL ** -0.5,
        'conv_w': nrm(ks[18], (DEPTH, CONV_W, D_FF), f32) * CONV_W ** -0.5,
        'conv_b': 0.02 * nrm(ks[19], (DEPTH, D_FF), f32),
        'w_down': nrm(ks[20], (DEPTH, D_FF, D_MODEL), f32) * D_FF ** -0.5,
        'final_norm_g': 1.0 + 0.02 * nrm(ks[21], (D_MODEL,), f32),
    }


def reference(x_prompt, x_sample, state_hgrn_S, state_mlstm_C, state_mlstm_n, state_mlstm_m,
              state_conv, norm1_g, w_in, hg_lb_logits, hg_norm_g, ml_b_ig, ml_b_fg, ml_norm_g,
              w_out, norm2_g, w_gate, w_val, conv_w, conv_b, w_down, final_norm_g):
    B = x_prompt.shape[0]
    dt = x_prompt.dtype
    S0 = jnp.zeros((DEPTH, B, HG_HEADS, HG_DK, HG_DV), dt)
    C0 = jnp.zeros((DEPTH, B, ML_HEADS, ML_DK, ML_DV), dt)
    n0 = jnp.zeros((DEPTH, B, ML_HEADS, ML_DK), dt)
    m0 = jnp.zeros((DEPTH, B, ML_HEADS), dt)
    buf0 = jnp.zeros((DEPTH, B, CONV_W - 1, D_FF), dt)
    y_prompt, S_p, C_p, n_p, m_p, buf_p = _trunk(
        x_prompt, S0, C0, n0, m0, buf0, norm1_g, w_in, hg_lb_logits, hg_norm_g, ml_b_ig, ml_b_fg,
        ml_norm_g, w_out, norm2_g, w_gate, w_val, conv_w, conv_b, w_down, final_norm_g)
    y_sample, S_s, C_s, n_s, m_s, buf_s = _trunk(
        x_sample, state_hgrn_S, state_mlstm_C, state_mlstm_n, state_mlstm_m, state_conv,
        norm1_g, w_in, hg_lb_logits, hg_norm_g, ml_b_ig, ml_b_fg, ml_norm_g, w_out, norm2_g,
        w_gate, w_val, conv_w, conv_b, w_down, final_norm_g)
    return (y_prompt, y_sample, S_p, S_s, C_p, C_s, n_p, n_s, m_p, m_s, buf_p, buf_s)
```

```python
import functools

import jax
import jax.numpy as jnp
from jax import lax
from jax.experimental import pallas as pl
from jax.experimental.pallas import tpu as pltpu

F32 = jnp.float32
BF16 = jnp.bfloat16
HI = lax.Precision.HIGHEST
NT = (((1,), (1,)), ((), ()))
TN = (((0,), (0,)), ((), ()))

D_MODEL = 1024
HEADS = 4
DH = 128
GW = HEADS * DH
D_FF = 2816
CONV_W = 3
EPS = 1e-6
TILE = 128
SUB = 8
VMEM_LIMIT = 56 * 1024 * 1024


def _rms(x, g):
    return x * lax.rsqrt(jnp.mean(x * x, axis=-1, keepdims=True) + EPS) * g


def _bdot(a, b):
    return jnp.dot(a.astype(BF16), b.astype(BF16), preferred_element_type=F32)


def _resident(shape):
    zeros = (0,) * len(shape)
    return pl.BlockSpec(shape, lambda *_: zeros, pipeline_mode=pl.Buffered(1))


def _inproj_kernel(x_ref, g_ref, w_ref, wg_ref, proj_ref, gate_ref):
    hb = _rms(x_ref[...], g_ref[...]).astype(BF16)
    for c in range(0, 8 * GW, GW):
        proj_ref[:, c:c + GW] = jnp.dot(hb, w_ref[:, c:c + GW], preferred_element_type=F32)
    gate_ref[...] = jnp.dot(hb, wg_ref[...], preferred_element_type=F32)


def _inproj(x2, g1, w_main, w_gate2, tm):
    n = x2.shape[0]
    return pl.pallas_call(
        _inproj_kernel,
        grid=(n // tm,),
        in_specs=[
            pl.BlockSpec((tm, D_MODEL), lambda i: (i, 0)),
            _resident((1, D_MODEL)),
            _resident((D_MODEL, 8 * GW)),
            _resident((D_MODEL, 2 * DH)),
        ],
        out_specs=[
            pl.BlockSpec((tm, 8 * GW), lambda i: (i, 0)),
            pl.BlockSpec((tm, 2 * DH), lambda i: (i, 0)),
        ],
        out_shape=[
            jax.ShapeDtypeStruct((n, 8 * GW), F32),
            jax.ShapeDtypeStruct((n, 2 * DH), F32),
        ],
        compiler_params=pltpu.CompilerParams(
            dimension_semantics=("arbitrary",), vmem_limit_bytes=VMEM_LIMIT),
        name="inproj",
    )(x2, g1, w_main, w_gate2)


def _seg_last(x, seg):
    nseg = TILE // seg
    w = x.shape[-1]
    if nseg == 1:
        last = jnp.broadcast_to(x[TILE - SUB:TILE][SUB - 1:SUB], (SUB, w))
        return jnp.concatenate([last] * (TILE // SUB), axis=0)
    y = x.reshape(nseg, seg, w)[:, seg - 1:seg, :]
    return jnp.broadcast_to(y, (nseg, seg, w)).reshape(TILE, w)


def _hgrn_intra(q, k, v, b, seg):
    sub = lax.broadcasted_iota(jnp.int32, (SUB, 1), 0)
    blocks = []
    for r0 in range(0, TILE, SUB):
        bb = b[r0:r0 + SUB]
        qb = q[r0:r0 + SUB]
        kb = k[r0:r0 + SUB]
        vb = v[r0:r0 + SUB]
        acc = jnp.zeros((SUB, DH), F32)
        for s in range(SUB):
            p = jnp.exp(bb - bb[s:s + 1]) * (qb * kb[s:s + 1])
            col = jnp.sum(p, axis=-1, keepdims=True)
            col = jnp.where(sub >= s, col, 0.0)
            acc = acc + col * vb[s:s + 1]
        blocks.append(acc)
    intra = jnp.concatenate(blocks, axis=0)

    if seg > SUB:
        ri = lax.broadcasted_iota(jnp.int32, (TILE, TILE), 0)
        ci = lax.broadcasted_iota(jnp.int32, (TILE, TILE), 1)
        a_off = jnp.zeros((TILE, TILE), F32)
        w = SUB
        while w < seg:
            zeros = jnp.zeros((w, DH), F32)
            qs, ks = [], []
            for r0 in range(0, TILE, 2 * w):
                ref = b[r0 + w - 1:r0 + w]
                ks += [k[r0:r0 + w] * jnp.exp(ref - b[r0:r0 + w]), zeros]
                qs += [zeros, q[r0 + w:r0 + 2 * w] * jnp.exp(b[r0 + w:r0 + 2 * w] - ref)]
            qt = jnp.concatenate(qs, axis=0).astype(BF16)
            kt = jnp.concatenate(ks, axis=0).astype(BF16)
            a_lvl = lax.dot_general(qt, kt, NT, preferred_element_type=F32)
            same_block = (ri // (2 * w)) == (ci // (2 * w))
            a_off = a_off + jnp.where(same_block, a_lvl, 0.0)
            w *= 2
        intra = intra + _bdot(a_off, v)
    return intra


def _mixer_kernel(proj_ref, gate_ref, lbl_ref, hgn_ref, mln_ref, gb_ref,
                  s_ref, c_ref, n_ref, m_ref,
                  o_ref, so_ref, co_ref, no_ref, mo_ref, *, seg):
    nseg = TILE // seg
    carry = seg == TILE

    if carry:
        @pl.when(pl.program_id(1) == 0)
        def _():
            so_ref[...] = s_ref[...]
            co_ref[...] = c_ref[...]
            no_ref[...] = n_ref[...]
            mo_ref[...] = m_ref[...]
        s_in, c_in = so_ref, co_ref
        n_all = no_ref[0]
        m_rows = jnp.broadcast_to(mo_ref[0], (TILE, DH))
    else:
        s_in, c_in = s_ref, c_ref
        n_all = n_ref[...]
        m_rows = m_ref[...]

    ri = lax.broadcasted_iota(jnp.int32, (TILE, TILE), 0)
    ci = lax.broadcasted_iota(jnp.int32, (TILE, TILE), 1)
    tri = ((ri // seg) == (ci // seg)) & (ci <= ri)
    tri_f = tri.astype(F32)
    lane = lax.broadcasted_iota(jnp.int32, (TILE, DH), 1)

    lg = lbl_ref[...]
    ex = jnp.exp(lg - jnp.max(lg, axis=0, keepdims=True))
    lb_all = ex[0:1] / jnp.sum(ex, axis=0, keepdims=True)

    for h in range(HEADS):
        hs = slice(h * DH, (h + 1) * DH)
        q = proj_ref[:, 0 * GW + h * DH:0 * GW + (h + 1) * DH]
        hf = proj_ref[:, 1 * GW + h * DH:1 * GW + (h + 1) * DH]
        v = proj_ref[:, 2 * GW + h * DH:2 * GW + (h + 1) * DH]
        og = proj_ref[:, 3 * GW + h * DH:3 * GW + (h + 1) * DH]
        lb = lb_all[:, hs]
        f = lb + (1.0 - lb) * jax.nn.sigmoid(hf)
        g = jnp.log(f)
        k = 1.0 - f
        b = jnp.dot(tri_f, g, precision=HI, preferred_element_type=F32)
        b_last = _seg_last(b, seg)
        qe = (q * jnp.exp(b)).astype(BF16)
        kd = (k * jnp.exp(b_last - b)).astype(BF16)
        vb16 = v.astype(BF16)
        e_last = jnp.exp(b_last)
        out = _hgrn_intra(q, k, v, b, seg)
        inter = []
        for sg in range(nseg):
            rows = slice(sg * seg, (sg + 1) * seg)
            st = s_in[sg, h]
            inter.append(jnp.dot(qe[rows], st.astype(BF16), preferred_element_type=F32))
            decay = jnp.broadcast_to(e_last[sg * seg:sg * seg + 1], (DH, DH)).T
            so_ref[sg, h] = decay * st + lax.dot_general(kd[rows], vb16[rows], TN, preferred_element_type=F32)
        out = out + jnp.concatenate(inter, axis=0)
        out = _rms(out, hgn_ref[:, hs]) * (og * jax.nn.sigmoid(og))
        o_ref[:, hs] = out

    ig_all = gate_ref[:, 0:DH] + gb_ref[:, 0:DH]
    lf_all = jax.nn.log_sigmoid(gate_ref[:, DH:2 * DH] + gb_ref[:, DH:2 * DH])
    a_all = jnp.dot(tri_f, lf_all, precision=HI, preferred_element_type=F32)
    a_all_t = a_all.T
    ig_all_t = ig_all.T
    m_out = jnp.zeros((TILE, DH), F32)
    for h in range(HEADS):
        hs = slice(h * DH, (h + 1) * DH)
        q = proj_ref[:, 4 * GW + h * DH:4 * GW + (h + 1) * DH]
        k = proj_ref[:, 5 * GW + h * DH:5 * GW + (h + 1) * DH] * (DH ** -0.5)
        v = proj_ref[:, 6 * GW + h * DH:6 * GW + (h + 1) * DH]
        og = proj_ref[:, 7 * GW + h * DH:7 * GW + (h + 1) * DH]
        a_col = jnp.broadcast_to(a_all[:, h:h + 1], (TILE, TILE))
        i_col = jnp.broadcast_to(ig_all[:, h:h + 1], (TILE, TILE))
        m_col = jnp.broadcast_to(m_rows[:, h:h + 1], (TILE, TILE))
        a_row = a_all_t[h:h + 1]
        i_row = ig_all_t[h:h + 1]
        log_d = jnp.where(tri, a_col - a_row + i_row, -jnp.inf)
        log_inter = a_col + m_col
        m_t = jnp.maximum(log_inter, jnp.max(log_d, axis=-1, keepdims=True))
        d_w = jnp.exp(log_d - m_t)
        w_i = jnp.exp(log_inter - m_t)
        qb16 = q.astype(BF16)
        kb16 = k.astype(BF16)
        vb16 = v.astype(BF16)
        s_w = lax.dot_general(qb16, kb16, NT, preferred_element_type=F32) * d_w
        num = jnp.dot(s_w.astype(BF16), vb16, preferred_element_type=F32)
        den = jnp.sum(s_w, axis=-1, keepdims=True)

        m_new = _seg_last(m_t, seg)
        a_end = _seg_last(a_col, seg)
        w_end = jnp.exp(a_end - a_col + i_col - m_new)
        f_end = jnp.exp(a_end + m_col - m_new)
        kw = k * w_end
        kw16 = kw.astype(BF16)
        qc, qn = [], []
        for sg in range(nseg):
            rows = slice(sg * seg, (sg + 1) * seg)
            last = sg * seg + seg - 1
            ct = c_in[sg, h]
            qc.append(jnp.dot(qb16[rows], ct.astype(BF16), preferred_element_type=F32))
            qn.append(jnp.sum(q[rows] * n_all[sg:sg + 1, hs], axis=-1, keepdims=True))
            fe = f_end[last:last + 1, 0:1]
            co_ref[sg, h] = fe * ct + lax.dot_general(kw16[rows], vb16[rows], TN, preferred_element_type=F32)
            n_new = fe * n_all[sg:sg + 1, hs] + jnp.sum(kw[rows], axis=0, keepdims=True)
            if carry:
                no_ref[0, :, hs] = n_new
            else:
                no_ref[sg:sg + 1, hs] = n_new
        num = w_i * jnp.concatenate(qc, axis=0) + num
        den = w_i[:, 0:1] * jnp.concatenate(qn, axis=0) + den
        hout = num / jnp.maximum(jnp.abs(den), jnp.exp(-m_t[:, 0:1]))
        hout = _rms(hout, mln_ref[:, hs]) * jax.nn.sigmoid(og)
        o_ref[:, GW + h * DH:GW + (h + 1) * DH] = hout
        m_out = jnp.where(lane == h, m_t, m_out)

    if carry:
        mo_ref[0] = m_out[TILE - 1:TILE]
    else:
        for sg in range(nseg):
            last = sg * seg + seg - 1
            mo_ref[sg:sg + 1, :] = m_out[last:last + 1]


def _mixer(proj, gates, lb_logits, hg_norm, ml_norm, gate_bias, s0, c0, n0, m0, batch, seq):
    n = batch * seq
    carry = seq >= TILE
    seg = TILE if carry else seq
    assert TILE % seg == 0 and seg % SUB == 0 and n % TILE == 0 and seq % seg == 0
    kern = functools.partial(_mixer_kernel, seg=seg)
    row = lambda shape: pl.BlockSpec(shape, (lambda b, t: (b * nt + t, 0)))
    params = [_resident((2, GW)), _resident((1, GW)), _resident((1, GW)), _resident((1, 2 * DH))]
    st_shape = jax.ShapeDtypeStruct((batch, HEADS, DH, DH), F32)
    if carry:
        nt = seq // TILE
        grid = (batch, nt)
        st_spec = pl.BlockSpec((1, HEADS, DH, DH), lambda b, t: (b, 0, 0, 0))
        n_spec = pl.BlockSpec((1, 1, GW), lambda b, t: (b, 0, 0))
        m_spec = pl.BlockSpec((1, 1, DH), lambda b, t: (b, 0, 0))
        n_in = n0.reshape(batch, 1, GW)
        m_in = jnp.pad(m0, ((0, 0), (0, DH - HEADS))).reshape(batch, 1, DH)
        n_shape = jax.ShapeDtypeStruct((batch, 1, GW), F32)
        m_shape = jax.ShapeDtypeStruct((batch, 1, DH), F32)
        m_in_spec = m_spec
    else:
        nt = 1
        nseg = TILE // seg
        grid = (n // TILE, 1)
        st_spec = pl.BlockSpec((nseg, HEADS, DH, DH), lambda b, t: (b, 0, 0, 0))
        n_spec = pl.BlockSpec((nseg, GW), lambda b, t: (b, 0))
        m_spec = pl.BlockSpec((nseg, DH), lambda b, t: (b, 0))
        m_in_spec = pl.BlockSpec((TILE, DH), lambda b, t: (b, 0))
        n_in = n0.reshape(batch, GW)
        m_in = jnp.repeat(jnp.pad(m0, ((0, 0), (0, DH - HEADS))), seq, axis=0)
        n_shape = jax.ShapeDtypeStruct((batch, GW), F32)
        m_shape = jax.ShapeDtypeStruct((batch, DH), F32)
    o, s_new, c_new, n_new, m_new = pl.pallas_call(
        kern,
        grid=grid,
        in_specs=[row((TILE, 8 * GW)), row((TILE, 2 * DH))] + params + [st_spec, st_spec, n_spec, m_in_spec],
        out_specs=[row((TILE, 2 * GW)), st_spec, st_spec, n_spec, m_spec],
        out_shape=[jax.ShapeDtypeStruct((n, 2 * GW), F32), st_shape, st_shape, n_shape, m_shape],
        compiler_params=pltpu.CompilerParams(
            dimension_semantics=("arbitrary", "arbitrary"), vmem_limit_bytes=VMEM_LIMIT),
        name="mixer_carry" if carry else "mixer_seg",
    )(proj, gates, lb_logits, hg_norm, ml_norm, gate_bias, s0, c0, n_in, m_in)
    return (o, s_new, c_new, n_new.reshape(batch, HEADS, DH), m_new.reshape(batch, DH)[:, :HEADS])


FF_CHUNK = 256


def _ffn_kernel(x_ref, o_ref, buf_ref, wo_ref, g2_ref, wg_ref, wv_ref, cw_ref, cb_ref, wd_ref, gf_ref,
                y_ref, nb_ref, u_sc, g_sc, *, bb, tt):
    m = bb * tt

    @pl.when(pl.program_id(1) == 0)
    def _():
        u_sc[:, SUB - 2:SUB, :] = buf_ref[...]

    x1 = x_ref[...].reshape(m, D_MODEL) + _bdot(o_ref[...].reshape(m, D_MODEL), wo_ref[...])
    h2 = _rms(x1, g2_ref[...]).astype(BF16)
    for c in range(0, D_FF, FF_CHUNK):
        cs = slice(c, c + FF_CHUNK)
        u = jnp.dot(h2, wg_ref[:, cs], preferred_element_type=F32).reshape(bb, tt, FF_CHUNK)
        val = jnp.dot(h2, wv_ref[:, cs], preferred_element_type=F32).reshape(bb, tt, FF_CHUNK)
        u_sc[:, SUB:SUB + tt, cs] = u
        u1 = u_sc[:, SUB - 1:SUB - 1 + tt, cs]
        u2 = u_sc[:, SUB - 2:SUB - 2 + tt, cs]
        conv = cb_ref[:, cs] + cw_ref[0:1, cs] * u2 + cw_ref[1:2, cs] * u1 + cw_ref[2:3, cs] * u
        g_sc[:, cs] = (jax.nn.gelu(conv) * val).reshape(m, FF_CHUNK).astype(BF16)
    last2 = u_sc[:, SUB + tt - 2:SUB + tt, :]
    u_sc[:, SUB - 2:SUB, :] = last2
    nb_ref[...] = last2
    y = x1 + jnp.dot(g_sc[...], wd_ref[...], preferred_element_type=F32)
    y_ref[...] = _rms(y, gf_ref[...]).reshape(bb, tt, D_MODEL)


def _ffn(x, o, buf, wo, g2, wg, wv, cw, cb, wd, gf, bb, tt):
    batch, seq, _ = x.shape
    assert batch % bb == 0 and seq % tt == 0 and tt % SUB == 0 and tt >= CONV_W - 1
    kern = functools.partial(_ffn_kernel, bb=bb, tt=tt)
    tok = pl.BlockSpec((bb, tt, D_MODEL), lambda b, t: (b, t, 0))
    hist = pl.BlockSpec((bb, CONV_W - 1, D_FF), lambda b, t: (b, 0, 0))
    return pl.pallas_call(
        kern,
        grid=(batch // bb, seq // tt),
        in_specs=[tok, tok, hist,
                  _resident((D_MODEL, D_MODEL)), _resident((1, D_MODEL)),
                  _resident((D_MODEL, D_FF)), _resident((D_MODEL, D_FF)),
                  _resident((CONV_W, D_FF)), _resident((1, D_FF)),
                  _resident((D_FF, D_MODEL)), _resident((1, D_MODEL))],
        out_specs=[tok, hist],
        out_shape=[jax.ShapeDtypeStruct((batch, seq, D_MODEL), F32),
                   jax.ShapeDtypeStruct((batch, CONV_W - 1, D_FF), F32)],
        scratch_shapes=[pltpu.VMEM((bb, SUB + tt, D_FF), F32), pltpu.VMEM((bb * tt, D_FF), BF16)],
        compiler_params=pltpu.CompilerParams(
            dimension_semantics=("arbitrary", "arbitrary"), vmem_limit_bytes=VMEM_LIMIT),
        name="ffn",
    )(x, o, buf, wo, g2, wg, wv, cw, cb, wd, gf)


TOKENS_PER_STEP = 512


def _layer(x, s0, c0, n0, m0, buf0, p):
    batch, seq, _ = x.shape
    n = batch * seq
    proj, gates = _inproj(x.reshape(n, D_MODEL), p["g1"], p["w_main"], p["w_gate2"], TOKENS_PER_STEP)
    o, s_new, c_new, n_new, m_new = _mixer(proj, gates, p["lb_logits"], p["hg_norm"], p["ml_norm"],
                                           p["gate_bias"], s0, c0, n0, m0, batch, seq)
    tt = min(seq, TOKENS_PER_STEP)
    bb = 1 if tt == TOKENS_PER_STEP else TOKENS_PER_STEP // (2 * tt)
    y, buf_new = _ffn(x, o.reshape(batch, seq, D_MODEL), buf0, p["wo"], p["g2"], p["wg"], p["wv"],
                      p["cw"], p["cb"], p["wd"], p["gf"], bb, tt)
    return y, s_new[None], c_new[None], n_new[None], m_new[None], buf_new[None]


def kernel(x_prompt, x_sample, state_hgrn_S, state_mlstm_C, state_mlstm_n, state_mlstm_m, state_conv, norm1_g, w_in, hg_lb_logits, hg_norm_g, ml_b_ig, ml_b_fg, ml_norm_g, w_out, norm2_g, w_gate, w_val, conv_w, conv_b, w_down, final_norm_g):
    assert norm1_g.shape[0] == 1, "single-layer trunk"
    w = w_in[0]
    gate_cols = w[:, 8 * GW:]
    zpad = jnp.zeros((D_MODEL, DH - HEADS), w.dtype)
    w_gate2 = jnp.concatenate([gate_cols[:, :HEADS], zpad, gate_cols[:, HEADS:], zpad], axis=1)
    bpad = jnp.zeros((DH - HEADS,), F32)
    p = {
        "g1": norm1_g, "w_main": w[:, :8 * GW].astype(BF16), "w_gate2": w_gate2.astype(BF16),
        "lb_logits": hg_lb_logits, "hg_norm": hg_norm_g, "ml_norm": ml_norm_g,
        "gate_bias": jnp.concatenate([ml_b_ig[0], bpad, ml_b_fg[0], bpad])[None],
        "wo": w_out[0].astype(BF16), "g2": norm2_g, "wg": w_gate[0].astype(BF16),
        "wv": w_val[0].astype(BF16), "cw": conv_w[0], "cb": conv_b, "wd": w_down[0].astype(BF16),
        "gf": final_norm_g[None],
    }
    b = x_prompt.shape[0]
    zs = jnp.zeros((b, HEADS, DH, DH), F32)
    prompt = _layer(x_prompt, zs, zs, jnp.zeros((b, HEADS, DH), F32), jnp.zeros((b, HEADS), F32),
                    jnp.zeros((b, CONV_W - 1, D_FF), F32), p)
    sample = _layer(x_sample, state_hgrn_S[0], state_mlstm_C[0], state_mlstm_n[0], state_mlstm_m[0],
                    state_conv[0], p)
    out = []
    for a, c in zip(prompt, sample):
        out += [a, c]
    return tuple(out)
```

```python
import functools

import jax
import jax.numpy as jnp
from jax import lax
from jax.experimental import pallas as pl
from jax.experimental.pallas import tpu as pltpu

F32 = jnp.float32
BF16 = jnp.bfloat16
HI = lax.Precision.HIGHEST
NT = (((1,), (1,)), ((), ()))
TN = (((0,), (0,)), ((), ()))

D_MODEL = 1024
HEADS = 4
DH = 128
GW = HEADS * DH
D_FF = 2816
CONV_W = 3
EPS = 1e-6
TILE = 128
SUB = 8
FF_CHUNK = 256
VMEM_LIMIT = 56 * 1024 * 1024


def _rms(x, g):
    return x * lax.rsqrt(jnp.mean(x * x, axis=-1, keepdims=True) + EPS) * g


def _bdot(a, b):
    return jnp.dot(a.astype(BF16), b.astype(BF16), preferred_element_type=F32)


def _resident(shape):
    zeros = (0,) * len(shape)
    return pl.BlockSpec(shape, lambda *_: zeros, pipeline_mode=pl.Buffered(1))


def _run(gen):
    try:
        while True:
            next(gen)
    except StopIteration as stop:
        return stop.value


def _interleave(primary, *others):
    gens = [primary, *others]
    done = [False] * len(gens)
    vals = [None] * len(gens)

    def step(j):
        try:
            next(gens[j])
        except StopIteration as stop:
            done[j], vals[j] = True, stop.value

    turn = 0
    while not all(done):
        if not done[0]:
            step(0)
        pending = [j for j in range(1, len(gens)) if not done[j]]
        if pending:
            step(pending[turn % len(pending)])
            turn += 1
    return vals


def _inproj_core(x, g_ref, w_ref, wg_ref, proj_ref, gate_ref):
    hb = _rms(x, g_ref[...]).astype(BF16)
    gate_ref[...] = jnp.dot(hb, wg_ref[...], preferred_element_type=F32)
    for c in range(0, 8 * GW, GW):
        yield
        proj_ref[:, c:c + GW] = jnp.dot(hb, w_ref[:, c:c + GW], preferred_element_type=F32)


def _inproj_kernel(x_ref, g_ref, w_ref, wg_ref, proj_ref, gate_ref):
    _run(_inproj_core(x_ref[...], g_ref, w_ref, wg_ref, proj_ref, gate_ref))


def _inproj(x2, g1, w_main, w_gate2, tm):
    n = x2.shape[0]
    return pl.pallas_call(
        _inproj_kernel,
        grid=(n // tm,),
        in_specs=[
            pl.BlockSpec((tm, D_MODEL), lambda i: (i, 0)),
            _resident((1, D_MODEL)),
            _resident((D_MODEL, 8 * GW)),
            _resident((D_MODEL, 2 * DH)),
        ],
        out_specs=[
            pl.BlockSpec((tm, 8 * GW), lambda i: (i, 0)),
            pl.BlockSpec((tm, 2 * DH), lambda i: (i, 0)),
        ],
        out_shape=[
            jax.ShapeDtypeStruct((n, 8 * GW), F32),
            jax.ShapeDtypeStruct((n, 2 * DH), F32),
        ],
        compiler_params=pltpu.CompilerParams(
            dimension_semantics=("arbitrary",), vmem_limit_bytes=VMEM_LIMIT),
        name="inproj",
    )(x2, g1, w_main, w_gate2)


def _seg_last(x, seg):
    nseg = TILE // seg
    w = x.shape[-1]
    if nseg == 1:
        last = jnp.broadcast_to(x[TILE - SUB:TILE][SUB - 1:SUB], (SUB, w))
        return jnp.concatenate([last] * (TILE // SUB), axis=0)
    y = x.reshape(nseg, seg, w)[:, seg - 1:seg, :]
    return jnp.broadcast_to(y, (nseg, seg, w)).reshape(TILE, w)


def _hgrn_intra(q, k, v, b, seg, b_ref, k_ref, v_ref):
    sub = lax.broadcasted_iota(jnp.int32, (SUB, 1), 0)
    blocks = []
    for r0 in range(0, TILE, SUB):
        bb = b[r0:r0 + SUB]
        qb = q[r0:r0 + SUB]
        acc = jnp.zeros((SUB, DH), F32)
        for s in range(SUB):
            bs = jnp.broadcast_to(b_ref[r0 + s:r0 + s + 1], (SUB, DH))
            ks = jnp.broadcast_to(k_ref[r0 + s:r0 + s + 1], (SUB, DH))
            vs = jnp.broadcast_to(v_ref[r0 + s:r0 + s + 1], (SUB, DH))
            p = jnp.exp2(bb - bs) * (qb * ks)
            col = jnp.sum(p, axis=-1, keepdims=True)
            col = jnp.where(sub >= s, col, 0.0)
            acc = acc + col * vs
        blocks.append(acc)
        if r0 % (4 * SUB) == 3 * SUB:
            yield
    intra = jnp.concatenate(blocks, axis=0)

    if seg > SUB:
        ri = lax.broadcasted_iota(jnp.int32, (TILE, TILE), 0)
        ci = lax.broadcasted_iota(jnp.int32, (TILE, TILE), 1)
        a_off = jnp.zeros((TILE, TILE), F32)
        w = SUB
        while w < seg:
            zeros = jnp.zeros((w, DH), F32)
            qs, ks = [], []
            for r0 in range(0, TILE, 2 * w):
                ref = b[r0 + w - 1:r0 + w]
                ks += [k[r0:r0 + w] * jnp.exp2(ref - b[r0:r0 + w]), zeros]
                qs += [zeros, q[r0 + w:r0 + 2 * w] * jnp.exp2(b[r0 + w:r0 + 2 * w] - ref)]
            qt = jnp.concatenate(qs, axis=0).astype(BF16)
            kt = jnp.concatenate(ks, axis=0).astype(BF16)
            a_lvl = lax.dot_general(qt, kt, NT, preferred_element_type=F32)
            same_block = (ri // (2 * w)) == (ci // (2 * w))
            a_off = a_off + jnp.where(same_block, a_lvl, 0.0)
            w *= 2
        yield
        intra = intra + _bdot(a_off, v)
    return intra


def _mixer_core(proj_ref, gate_ref, lbl_ref, hgn_ref, mln_ref, gb_ref,
                s_in, c_in, s_out, c_out, n_all, m_rows, put_n, o_ref, b_sc, k_sc, seg):
    nseg = TILE // seg
    ri = lax.broadcasted_iota(jnp.int32, (TILE, TILE), 0)
    ci = lax.broadcasted_iota(jnp.int32, (TILE, TILE), 1)
    tri = ((ri // seg) == (ci // seg)) & (ci <= ri)
    tri_f = tri.astype(F32)
    lane = lax.broadcasted_iota(jnp.int32, (TILE, DH), 1)

    lg = lbl_ref[...]
    ex = jnp.exp(lg - jnp.max(lg, axis=0, keepdims=True))
    lb_all = ex[0:1] / jnp.sum(ex, axis=0, keepdims=True)

    ig_all = gate_ref[:, 0:DH] + gb_ref[:, 0:DH]
    lf_all = jax.nn.log_sigmoid(gate_ref[:, DH:2 * DH] + gb_ref[:, DH:2 * DH])
    a_all = jnp.dot(tri_f, lf_all, precision=HI, preferred_element_type=F32)
    a_all_t = a_all.T
    ig_all_t = ig_all.T
    m_out = jnp.zeros((TILE, DH), F32)
    yield

    for h in range(HEADS):
        hs = slice(h * DH, (h + 1) * DH)

        q = proj_ref[:, 0 * GW + h * DH:0 * GW + (h + 1) * DH]
        hf = proj_ref[:, 1 * GW + h * DH:1 * GW + (h + 1) * DH]
        v = proj_ref[:, 2 * GW + h * DH:2 * GW + (h + 1) * DH]
        og = proj_ref[:, 3 * GW + h * DH:3 * GW + (h + 1) * DH]
        lb = lb_all[:, hs]
        f = lb + (1.0 - lb) * jax.nn.sigmoid(hf)
        k = 1.0 - f
        b = jnp.dot(tri_f, jnp.log2(f), precision=HI, preferred_element_type=F32)
        b_last = _seg_last(b, seg)
        qe = (q * jnp.exp2(b)).astype(BF16)
        kd = (k * jnp.exp2(b_last - b)).astype(BF16)
        vb16 = v.astype(BF16)
        e_last = jnp.exp2(b_last)
        b_sc[h] = b
        k_sc[h] = k
        yield
        out = yield from _hgrn_intra(q, k, v, b, seg, b_sc.at[h], k_sc.at[h],
                                     proj_ref.at[:, 2 * GW + h * DH:2 * GW + (h + 1) * DH])
        inter = []
        for sg in range(nseg):
            rows = slice(sg * seg, (sg + 1) * seg)
            st = s_in[sg, h]
            inter.append(jnp.dot(qe[rows], st.astype(BF16), preferred_element_type=F32))
            decay = jnp.broadcast_to(e_last[sg * seg:sg * seg + 1], (DH, DH)).T
            s_out[sg, h] = decay * st + lax.dot_general(kd[rows], vb16[rows], TN, preferred_element_type=F32)
        out = out + jnp.concatenate(inter, axis=0)
        out = _rms(out, hgn_ref[:, hs]) * (og * jax.nn.sigmoid(og))
        o_ref[:, hs] = out
        yield

        q = proj_ref[:, 4 * GW + h * DH:4 * GW + (h + 1) * DH]
        k = proj_ref[:, 5 * GW + h * DH:5 * GW + (h + 1) * DH] * (DH ** -0.5)
        v = proj_ref[:, 6 * GW + h * DH:6 * GW + (h + 1) * DH]
        og = proj_ref[:, 7 * GW + h * DH:7 * GW + (h + 1) * DH]
        a_col = jnp.broadcast_to(a_all[:, h:h + 1], (TILE, TILE))
        i_col = jnp.broadcast_to(ig_all[:, h:h + 1], (TILE, TILE))
        m_col = jnp.broadcast_to(m_rows[:, h:h + 1], (TILE, TILE))
        a_row = a_all_t[h:h + 1]
        i_row = ig_all_t[h:h + 1]
        log_d = jnp.where(tri, a_col - a_row + i_row, -jnp.inf)
        log_inter = a_col + m_col
        m_t = jnp.maximum(log_inter, jnp.max(log_d, axis=-1, keepdims=True))
        d_w = jnp.exp(log_d - m_t)
        w_i = jnp.exp(log_inter - m_t)
        qb16 = q.astype(BF16)
        kb16 = k.astype(BF16)
        vb16 = v.astype(BF16)
        s_w = lax.dot_general(qb16, kb16, NT, preferred_element_type=F32) * d_w
        num = jnp.dot(s_w.astype(BF16), vb16, preferred_element_type=F32)
        den = jnp.sum(s_w, axis=-1, keepdims=True)
        yield

        m_new = _seg_last(m_t, seg)
        a_end = _seg_last(a_col, seg)
        w_end = jnp.exp(a_end - a_col + i_col - m_new)
        f_end = jnp.exp(a_end + m_col - m_new)
        kw = k * w_end
        kw16 = kw.astype(BF16)
        qc, qn = [], []
        for sg in range(nseg):
            rows = slice(sg * seg, (sg + 1) * seg)
            last = sg * seg + seg - 1
            ct = c_in[sg, h]
            qc.append(jnp.dot(qb16[rows], ct.astype(BF16), preferred_element_type=F32))
            qn.append(jnp.sum(q[rows] * n_all[sg:sg + 1, hs], axis=-1, keepdims=True))
            fe = f_end[last:last + 1, 0:1]
            c_out[sg, h] = fe * ct + lax.dot_general(kw16[rows], vb16[rows], TN, preferred_element_type=F32)
            put_n(sg, hs, fe * n_all[sg:sg + 1, hs] + jnp.sum(kw[rows], axis=0, keepdims=True))
        num = w_i * jnp.concatenate(qc, axis=0) + num
        den = w_i[:, 0:1] * jnp.concatenate(qn, axis=0) + den
        hout = num / jnp.maximum(jnp.abs(den), jnp.exp(-m_t[:, 0:1]))
        hout = _rms(hout, mln_ref[:, hs]) * jax.nn.sigmoid(og)
        o_ref[:, GW + h * DH:GW + (h + 1) * DH] = hout
        m_out = jnp.where(lane == h, m_t, m_out)
        yield
    return m_out


def _mixer_seg_kernel(proj_ref, gate_ref, lbl_ref, hgn_ref, mln_ref, gb_ref, s_ref, c_ref, n_ref, m_ref,
                      o_ref, so_ref, co_ref, no_ref, mo_ref, b_sc, k_sc, *, seg):
    def put_n(sg, hs, val):
        no_ref[sg:sg + 1, hs] = val

    m_out = _run(_mixer_core(proj_ref, gate_ref, lbl_ref, hgn_ref, mln_ref, gb_ref, s_ref, c_ref, so_ref, co_ref,
                             n_ref[...], m_ref[...], put_n, o_ref, b_sc, k_sc, seg))
    for sg in range(TILE // seg):
        last = sg * seg + seg - 1
        mo_ref[sg:sg + 1, :] = m_out[last:last + 1]


def _mixer_seg(proj, gates, lb_logits, hg_norm, ml_norm, gate_bias, s0, c0, n0, m0, batch, seq):
    n = batch * seq
    assert TILE % seq == 0 and seq % SUB == 0 and n % TILE == 0
    nseg = TILE // seq
    row = lambda shape: pl.BlockSpec(shape, lambda i: (i, 0))
    st_spec = pl.BlockSpec((nseg, HEADS, DH, DH), lambda i: (i, 0, 0, 0))
    st_shape = jax.ShapeDtypeStruct((batch, HEADS, DH, DH), F32)
    m_rows = jnp.repeat(jnp.pad(m0, ((0, 0), (0, DH - HEADS))), seq, axis=0)
    o, s_new, c_new, n_new, m_new = pl.pallas_call(
        functools.partial(_mixer_seg_kernel, seg=seq),
        grid=(n // TILE,),
        in_specs=[row((TILE, 8 * GW)), row((TILE, 2 * DH)),
                  _resident((2, GW)), _resident((1, GW)), _resident((1, GW)), _resident((1, 2 * DH)),
                  st_spec, st_spec, row((nseg, GW)), row((TILE, DH))],
        out_specs=[row((TILE, 2 * GW)), st_spec, st_spec, row((nseg, GW)), row((nseg, DH))],
        out_shape=[jax.ShapeDtypeStruct((n, 2 * GW), F32), st_shape, st_shape,
                   jax.ShapeDtypeStruct((batch, GW), F32), jax.ShapeDtypeStruct((batch, DH), F32)],
        scratch_shapes=[pltpu.VMEM((HEADS, TILE, DH), F32), pltpu.VMEM((HEADS, TILE, DH), F32)],
        compiler_params=pltpu.CompilerParams(
            dimension_semantics=("arbitrary",), vmem_limit_bytes=VMEM_LIMIT),
        name="mixer_seg",
    )(proj, gates, lb_logits, hg_norm, ml_norm, gate_bias, s0, c0, n0.reshape(batch, GW), m_rows)
    return o, s_new, c_new, n_new.reshape(batch, HEADS, DH), m_new[:, :HEADS]


SEQS_PER_STEP = 2


def _mixer_carry_kernel(proj_ref, gate_ref, lbl_ref, hgn_ref, mln_ref, gb_ref, s_ref, c_ref, n_ref, m_ref,
                        o_ref, so_ref, co_ref, no_ref, mo_ref, b_sc, k_sc):
    @pl.when(pl.program_id(1) == 0)
    def _():
        so_ref[...] = s_ref[...]
        co_ref[...] = c_ref[...]
        no_ref[...] = n_ref[...]
        mo_ref[...] = m_ref[...]

    tiles = []
    for j in range(SEQS_PER_STEP):
        def put_n(sg, hs, val, j=j):
            no_ref[j, :, hs] = val

        state, cell = so_ref.at[j:j + 1], co_ref.at[j:j + 1]
        tiles.append(_mixer_core(proj_ref.at[j], gate_ref.at[j], lbl_ref, hgn_ref, mln_ref, gb_ref,
                                 state, cell, state, cell, no_ref[j], jnp.broadcast_to(mo_ref[j], (TILE, DH)),
                                 put_n, o_ref.at[j], b_sc.at[j], k_sc.at[j], TILE))
    for j, m_out in enumerate(_interleave(*tiles)):
        mo_ref[j] = m_out[TILE - 1:TILE]


def _mixer_carry(proj, gates, lb_logits, hg_norm, ml_norm, gate_bias, s0, c0, n0, m0, batch, seq):
    assert seq % TILE == 0 and batch % SEQS_PER_STEP == 0
    nb = SEQS_PER_STEP
    tok = lambda w: pl.BlockSpec((nb, TILE, w), lambda b, t: (b, t, 0))
    st_spec = pl.BlockSpec((nb, HEADS, DH, DH), lambda b, t: (b, 0, 0, 0))
    n_spec = pl.BlockSpec((nb, 1, GW), lambda b, t: (b, 0, 0))
    m_spec = pl.BlockSpec((nb, 1, DH), lambda b, t: (b, 0, 0))
    st_shape = jax.ShapeDtypeStruct((batch, HEADS, DH, DH), F32)
    o, s_new, c_new, n_new, m_new = pl.pallas_call(
        _mixer_carry_kernel,
        grid=(batch // nb, seq // TILE),
        in_specs=[tok(8 * GW), tok(2 * DH),
                  _resident((2, GW)), _resident((1, GW)), _resident((1, GW)), _resident((1, 2 * DH)),
                  st_spec, st_spec, n_spec, m_spec],
        out_specs=[tok(2 * GW), st_spec, st_spec, n_spec, m_spec],
        out_shape=[jax.ShapeDtypeStruct((batch, seq, 2 * GW), F32), st_shape, st_shape,
                   jax.ShapeDtypeStruct((batch, 1, GW), F32), jax.ShapeDtypeStruct((batch, 1, DH), F32)],
        scratch_shapes=[pltpu.VMEM((nb, HEADS, TILE, DH), F32), pltpu.VMEM((nb, HEADS, TILE, DH), F32)],
        compiler_params=pltpu.CompilerParams(
            dimension_semantics=("arbitrary", "arbitrary"), vmem_limit_bytes=VMEM_LIMIT),
        name="mixer_carry",
    )(proj.reshape(batch, seq, 8 * GW), gates.reshape(batch, seq, 2 * DH), lb_logits, hg_norm, ml_norm, gate_bias,
      s0, c0, n0.reshape(batch, 1, GW), jnp.pad(m0, ((0, 0), (0, DH - HEADS))).reshape(batch, 1, DH))
    return o, s_new, c_new, n_new.reshape(batch, HEADS, DH), m_new.reshape(batch, DH)[:, :HEADS]


def _ffn_core(x, o, wo_ref, g2_ref, wg_ref, wv_ref, cw_ref, cb_ref, wd_ref, gf_ref, u_sc, g_sc, bb, tt):
    m = bb * tt
    x1 = x + _bdot(o, wo_ref[...])
    h2 = _rms(x1, g2_ref[...]).astype(BF16)
    for c in range(0, D_FF, FF_CHUNK):
        yield
        cs = slice(c, c + FF_CHUNK)
        u = jnp.dot(h2, wg_ref[:, cs], preferred_element_type=F32).reshape(bb, tt, FF_CHUNK)
        val = jnp.dot(h2, wv_ref[:, cs], preferred_element_type=F32).reshape(bb, tt, FF_CHUNK)
        u_sc[:, SUB:SUB + tt, cs] = u
        u1 = u_sc[:, SUB - 1:SUB - 1 + tt, cs]
        u2 = u_sc[:, SUB - 2:SUB - 2 + tt, cs]
        conv = cb_ref[:, cs] + cw_ref[0:1, cs] * u2 + cw_ref[1:2, cs] * u1 + cw_ref[2:3, cs] * u
        g_sc[:, cs] = (jax.nn.gelu(conv) * val).reshape(m, FF_CHUNK).astype(BF16)
    u_sc[:, SUB - 2:SUB, :] = u_sc[:, SUB + tt - 2:SUB + tt, :]
    yield
    y = x1 + jnp.dot(g_sc[...], wd_ref[...], preferred_element_type=F32)
    return _rms(y, gf_ref[...])


def _ffn_kernel(x_ref, o_ref, buf_ref, wo_ref, g2_ref, wg_ref, wv_ref, cw_ref, cb_ref, wd_ref, gf_ref,
                y_ref, nb_ref, u_sc, g_sc, *, bb, tt):
    m = bb * tt

    @pl.when(pl.program_id(1) == 0)
    def _():
        u_sc[:, SUB - 2:SUB, :] = buf_ref[...]

    y = _run(_ffn_core(x_ref[...].reshape(m, D_MODEL), o_ref[...].reshape(m, D_MODEL), wo_ref, g2_ref, wg_ref,
                       wv_ref, cw_ref, cb_ref, wd_ref, gf_ref, u_sc, g_sc, bb, tt))
    nb_ref[...] = u_sc[:, SUB - 2:SUB, :]
    y_ref[...] = y.reshape(bb, tt, D_MODEL)


def _ffn(x, o, buf, wo, g2, wg, wv, cw, cb, wd, gf, bb, tt):
    batch, seq, _ = x.shape
    assert batch % bb == 0 and seq % tt == 0 and tt % SUB == 0 and tt >= CONV_W - 1
    kern = functools.partial(_ffn_kernel, bb=bb, tt=tt)
    tok = pl.BlockSpec((bb, tt, D_MODEL), lambda b, t: (b, t, 0))
    hist = pl.BlockSpec((bb, CONV_W - 1, D_FF), lambda b, t: (b, 0, 0))
    return pl.pallas_call(
        kern,
        grid=(batch // bb, seq // tt),
        in_specs=[tok, tok, hist,
                  _resident((D_MODEL, D_MODEL)), _resident((1, D_MODEL)),
                  _resident((D_MODEL, D_FF)), _resident((D_MODEL, D_FF)),
                  _resident((CONV_W, D_FF)), _resident((1, D_FF)),
                  _resident((D_FF, D_MODEL)), _resident((1, D_MODEL))],
        out_specs=[tok, hist],
        out_shape=[jax.ShapeDtypeStruct((batch, seq, D_MODEL), F32),
                   jax.ShapeDtypeStruct((batch, CONV_W - 1, D_FF), F32)],
        scratch_shapes=[pltpu.VMEM((bb, SUB + tt, D_FF), F32), pltpu.VMEM((bb * tt, D_FF), BF16)],
        compiler_params=pltpu.CompilerParams(
            dimension_semantics=("arbitrary", "arbitrary"), vmem_limit_bytes=VMEM_LIMIT),
        name="ffn",
    )(x, o, buf, wo, g2, wg, wv, cw, cb, wd, gf)


ROWS_PER_STEP = 512


def _layer(x, s0, c0, n0, m0, buf0, p):
    batch, seq, _ = x.shape
    n = batch * seq
    proj, gates = _inproj(x.reshape(n, D_MODEL), p["g1"], p["w_main"], p["w_gate2"], ROWS_PER_STEP)
    mixer = _mixer_carry if seq >= TILE else _mixer_seg
    o, s_new, c_new, n_new, m_new = mixer(proj, gates, p["lb_logits"], p["hg_norm"], p["ml_norm"],
                                          p["gate_bias"], s0, c0, n0, m0, batch, seq)
    tt = min(seq, ROWS_PER_STEP)
    bb = 1 if tt == ROWS_PER_STEP else ROWS_PER_STEP // (2 * tt)
    y, buf_new = _ffn(x, o.reshape(batch, seq, D_MODEL), buf0, p["wo"], p["g2"], p["wg"], p["wv"],
                      p["cw"], p["cb"], p["wd"], p["gf"], bb, tt)
    return y, s_new[None], c_new[None], n_new[None], m_new[None], buf_new[None]


def kernel(x_prompt, x_sample, state_hgrn_S, state_mlstm_C, state_mlstm_n, state_mlstm_m, state_conv, norm1_g, w_in, hg_lb_logits, hg_norm_g, ml_b_ig, ml_b_fg, ml_norm_g, w_out, norm2_g, w_gate, w_val, conv_w, conv_b, w_down, final_norm_g):
    assert norm1_g.shape[0] == 1, "single-layer trunk"
    w = w_in[0]
    gate_cols = w[:, 8 * GW:]
    zpad = jnp.zeros((D_MODEL, DH - HEADS), w.dtype)
    w_gate2 = jnp.concatenate([gate_cols[:, :HEADS], zpad, gate_cols[:, HEADS:], zpad], axis=1)
    bpad = jnp.zeros((DH - HEADS,), F32)
    p = {
        "g1": norm1_g, "w_main": w[:, :8 * GW].astype(BF16), "w_gate2": w_gate2.astype(BF16),
        "lb_logits": hg_lb_logits, "hg_norm": hg_norm_g, "ml_norm": ml_norm_g,
        "gate_bias": jnp.concatenate([ml_b_ig[0], bpad, ml_b_fg[0], bpad])[None],
        "wo": w_out[0].astype(BF16), "g2": norm2_g, "wg": w_gate[0].astype(BF16),
        "wv": w_val[0].astype(BF16), "cw": conv_w[0], "cb": conv_b, "wd": w_down[0].astype(BF16),
        "gf": final_norm_g[None],
    }
    b = x_prompt.shape[0]
    zs = jnp.zeros((b, HEADS, DH, DH), F32)
    prompt = _layer(x_prompt, zs, zs, jnp.zeros((b, HEADS, DH), F32), jnp.zeros((b, HEADS), F32),
                    jnp.zeros((b, CONV_W - 1, D_FF), F32), p)
    sample = _layer(x_sample, state_hgrn_S[0], state_mlstm_C[0], state_mlstm_n[0], state_mlstm_m[0],
                    state_conv[0], p)
    out = []
    for a, c in zip(prompt, sample):
        out += [a, c]
    return tuple(out)
```

```python
import functools

import jax
import jax.numpy as jnp
from jax import lax
from jax.experimental import pallas as pl
from jax.experimental.pallas import tpu as pltpu

F32 = jnp.float32
BF16 = jnp.bfloat16
HI = lax.Precision.HIGHEST
NT = (((1,), (1,)), ((), ()))
TN = (((0,), (0,)), ((), ()))

D_MODEL = 1024
HEADS = 4
DH = 128
GW = HEADS * DH
D_FF = 2816
CONV_W = 3
EPS = 1e-6
TILE = 128
SUB = 8
FF_CHUNK = 256
VMEM_LIMIT = 56 * 1024 * 1024


def _rms(x, g):
    return x * lax.rsqrt(jnp.mean(x * x, axis=-1, keepdims=True) + EPS) * g


def _rms_mxu(x, g):
    ms = jnp.dot((x * x).astype(BF16), jnp.full((DH, DH), 1.0 / DH, BF16), preferred_element_type=F32)
    return x * lax.rsqrt(ms + EPS) * g


def _bdot(a, b):
    return jnp.dot(a.astype(BF16), b.astype(BF16), preferred_element_type=F32)


def _resident(shape):
    zeros = (0,) * len(shape)
    return pl.BlockSpec(shape, lambda *_: zeros, pipeline_mode=pl.Buffered(1))


def _run(gen):
    try:
        while True:
            next(gen)
    except StopIteration as stop:
        return stop.value


def _interleave(primary, *others):
    gens = [primary, *others]
    done = [False] * len(gens)
    vals = [None] * len(gens)

    def step(j):
        try:
            next(gens[j])
        except StopIteration as stop:
            done[j], vals[j] = True, stop.value

    turn = 0
    while not all(done):
        if not done[0]:
            step(0)
        pending = [j for j in range(1, len(gens)) if not done[j]]
        if pending:
            step(pending[turn % len(pending)])
            turn += 1
    return vals


def _inproj_core(x, g_ref, w_ref, wg_ref, proj_ref, gate_ref):
    hb = _rms(x, g_ref[...]).astype(BF16)
    gate_ref[...] = jnp.dot(hb, wg_ref[...], preferred_element_type=F32)
    for c in range(0, 8 * GW, GW):
        yield
        proj_ref[:, c:c + GW] = jnp.dot(hb, w_ref[:, c:c + GW], preferred_element_type=F32)


def _inproj_kernel(x_ref, g_ref, w_ref, wg_ref, proj_ref, gate_ref):
    _run(_inproj_core(x_ref[...], g_ref, w_ref, wg_ref, proj_ref, gate_ref))


IN_COLS = 8 * GW + 2 * HEADS


def _inproj(x2, g1, w_all, w_gate2, tm):
    n = x2.shape[0]
    return pl.pallas_call(
        _inproj_kernel,
        grid=(n // tm,),
        in_specs=[
            pl.BlockSpec((tm, D_MODEL), lambda i: (i, 0)),
            _resident((1, D_MODEL)),
            _resident((D_MODEL, IN_COLS)),
            _resident((D_MODEL, 2 * DH)),
        ],
        out_specs=[
            pl.BlockSpec((tm, 8 * GW), lambda i: (i, 0)),
            pl.BlockSpec((tm, 2 * DH), lambda i: (i, 0)),
        ],
        out_shape=[
            jax.ShapeDtypeStruct((n, 8 * GW), F32),
            jax.ShapeDtypeStruct((n, 2 * DH), F32),
        ],
        compiler_params=pltpu.CompilerParams(
            dimension_semantics=("arbitrary",), vmem_limit_bytes=VMEM_LIMIT),
        name="inproj",
    )(x2, g1, w_all, w_gate2)


def _seg_last(x, seg):
    nseg = TILE // seg
    w = x.shape[-1]
    if nseg == 1:
        last = jnp.broadcast_to(x[TILE - SUB:TILE][SUB - 1:SUB], (SUB, w))
        return jnp.concatenate([last] * (TILE // SUB), axis=0)
    y = x.reshape(nseg, seg, w)[:, seg - 1:seg, :]
    return jnp.broadcast_to(y, (nseg, seg, w)).reshape(TILE, w)


def _hgrn_intra(q, k, v, b, seg, b_ref, k_ref, v_ref):
    sub = lax.broadcasted_iota(jnp.int32, (SUB, 1), 0)
    blocks = []
    for r0 in range(0, TILE, SUB):
        bb = b[r0:r0 + SUB]
        qb = q[r0:r0 + SUB]
        acc = jnp.zeros((SUB, DH), F32)
        for s in range(SUB):
            bs = jnp.broadcast_to(b_ref[r0 + s:r0 + s + 1], (SUB, DH))
            ks = jnp.broadcast_to(k_ref[r0 + s:r0 + s + 1], (SUB, DH))
            vs = jnp.broadcast_to(v_ref[r0 + s:r0 + s + 1], (SUB, DH))
            p = jnp.exp2(bb - bs) * (qb * ks)
            col = jnp.sum(p, axis=-1, keepdims=True)
            col = jnp.where(sub >= s, col, 0.0)
            acc = acc + col * vs
        blocks.append(acc)
        if r0 % (4 * SUB) == 3 * SUB:
            yield
    intra = jnp.concatenate(blocks, axis=0)

    if seg > SUB:
        ri = lax.broadcasted_iota(jnp.int32, (TILE, TILE), 0)
        ci = lax.broadcasted_iota(jnp.int32, (TILE, TILE), 1)
        a_off = jnp.zeros((TILE, TILE), F32)
        w = SUB
        while w < seg:
            zeros = jnp.zeros((w, DH), F32)
            qs, ks = [], []
            for r0 in range(0, TILE, 2 * w):
                ref = b[r0 + w - 1:r0 + w]
                ks += [k[r0:r0 + w] * jnp.exp2(ref - b[r0:r0 + w]), zeros]
                qs += [zeros, q[r0 + w:r0 + 2 * w] * jnp.exp2(b[r0 + w:r0 + 2 * w] - ref)]
            qt = jnp.concatenate(qs, axis=0).astype(BF16)
            kt = jnp.concatenate(ks, axis=0).astype(BF16)
            a_lvl = lax.dot_general(qt, kt, NT, preferred_element_type=F32)
            same_block = (ri // (2 * w)) == (ci // (2 * w))
            a_off = a_off + jnp.where(same_block, a_lvl, 0.0)
            w *= 2
        yield
        intra = intra + _bdot(a_off, v)
    return intra


def _mixer_core(proj_ref, gate_ref, lbl_ref, hgn_ref, mln_ref, gb_ref,
                s_in, c_in, s_out, c_out, n_all, m_rows, put_n, o_ref, b_sc, k_sc, seg):
    nseg = TILE // seg
    ri = lax.broadcasted_iota(jnp.int32, (TILE, TILE), 0)
    ci = lax.broadcasted_iota(jnp.int32, (TILE, TILE), 1)
    tri = ((ri // seg) == (ci // seg)) & (ci <= ri)
    tri_f = tri.astype(F32)
    lane = lax.broadcasted_iota(jnp.int32, (TILE, DH), 1)

    lg = lbl_ref[...]
    ex = jnp.exp(lg - jnp.max(lg, axis=0, keepdims=True))
    lb_all = ex[0:1] / jnp.sum(ex, axis=0, keepdims=True)

    ig_all = gate_ref[:, 0:DH] + gb_ref[:, 0:DH]
    lf_all = jax.nn.log_sigmoid(gate_ref[:, DH:2 * DH] + gb_ref[:, DH:2 * DH])
    a_all = jnp.dot(tri_f, lf_all, precision=HI, preferred_element_type=F32)
    a_all_t = a_all.T
    ig_all_t = ig_all.T
    m_out = jnp.zeros((TILE, DH), F32)
    yield

    for h in range(HEADS):
        hs = slice(h * DH, (h + 1) * DH)

        q = proj_ref[:, 0 * GW + h * DH:0 * GW + (h + 1) * DH]
        hf = proj_ref[:, 1 * GW + h * DH:1 * GW + (h + 1) * DH]
        v = proj_ref[:, 2 * GW + h * DH:2 * GW + (h + 1) * DH]
        og = proj_ref[:, 3 * GW + h * DH:3 * GW + (h + 1) * DH]
        lb = lb_all[:, hs]
        f = lb + (1.0 - lb) * jax.nn.sigmoid(hf)
        k = 1.0 - f
        b = jnp.dot(tri_f, jnp.log2(f), precision=HI, preferred_element_type=F32)
        b_last = _seg_last(b, seg)
        qe = (q * jnp.exp2(b)).astype(BF16)
        kd = (k * jnp.exp2(b_last - b)).astype(BF16)
        vb16 = v.astype(BF16)
        e_last = jnp.exp2(b_last)
        b_sc[h] = b
        k_sc[h] = k
        yield
        out = yield from _hgrn_intra(q, k, v, b, seg, b_sc.at[h], k_sc.at[h],
                                     proj_ref.at[:, 2 * GW + h * DH:2 * GW + (h + 1) * DH])
        inter = []
        for sg in range(nseg):
            rows = slice(sg * seg, (sg + 1) * seg)
            st = s_in[sg, h]
            inter.append(jnp.dot(qe[rows], st.astype(BF16), preferred_element_type=F32))
            decay = jnp.broadcast_to(e_last[sg * seg:sg * seg + 1], (DH, DH)).T
            s_out[sg, h] = decay * st + lax.dot_general(kd[rows], vb16[rows], TN, preferred_element_type=F32)
        out = out + jnp.concatenate(inter, axis=0)
        out = _rms_mxu(out, hgn_ref[:, hs]) * (og * jax.nn.sigmoid(og))
        o_ref[:, hs] = out
        yield

        q = proj_ref[:, 4 * GW + h * DH:4 * GW + (h + 1) * DH]
        k = proj_ref[:, 5 * GW + h * DH:5 * GW + (h + 1) * DH] * (DH ** -0.5)
        v = proj_ref[:, 6 * GW + h * DH:6 * GW + (h + 1) * DH]
        og = proj_ref[:, 7 * GW + h * DH:7 * GW + (h + 1) * DH]
        a_col = jnp.broadcast_to(a_all[:, h:h + 1], (TILE, TILE))
        i_col = jnp.broadcast_to(ig_all[:, h:h + 1], (TILE, TILE))
        m_col = jnp.broadcast_to(m_rows[:, h:h + 1], (TILE, TILE))
        a_row = a_all_t[h:h + 1]
        i_row = ig_all_t[h:h + 1]
        log_d = jnp.where(tri, a_col - a_row + i_row, -jnp.inf)
        log_inter = a_col + m_col
        m_t = jnp.maximum(log_inter, jnp.max(log_d, axis=-1, keepdims=True))
        d_w = jnp.exp(log_d - m_t)
        w_i = jnp.exp(log_inter - m_t)
        qb16 = q.astype(BF16)
        kb16 = k.astype(BF16)
        vb16 = v.astype(BF16)
        s_w = lax.dot_general(qb16, kb16, NT, preferred_element_type=F32) * d_w
        nd = jnp.dot(s_w.astype(BF16), jnp.concatenate([vb16, jnp.ones((TILE, DH), BF16)], axis=1),
                     preferred_element_type=F32)
        num, den = nd[:, 0:DH], nd[:, DH:2 * DH]
        yield

        m_new = _seg_last(m_t, seg)
        a_end = _seg_last(a_col, seg)
        w_end = jnp.exp(a_end - a_col + i_col - m_new)
        f_end = jnp.exp(a_end + m_col - m_new)
        kw = k * w_end
        kw16 = kw.astype(BF16)
        qc, qn = [], []
        for sg in range(nseg):
            rows = slice(sg * seg, (sg + 1) * seg)
            last = sg * seg + seg - 1
            ct = c_in[sg, h]
            qc.append(jnp.dot(qb16[rows], ct.astype(BF16), preferred_element_type=F32))
            qn.append(jnp.sum(q[rows] * n_all[sg:sg + 1, hs], axis=-1, keepdims=True))
            fe = f_end[last:last + 1, 0:1]
            c_out[sg, h] = fe * ct + lax.dot_general(kw16[rows], vb16[rows], TN, preferred_element_type=F32)
            put_n(sg, hs, fe * n_all[sg:sg + 1, hs] + jnp.sum(kw[rows], axis=0, keepdims=True))
        num = w_i * jnp.concatenate(qc, axis=0) + num
        den = w_i * jnp.concatenate(qn, axis=0) + den
        hout = num / jnp.maximum(jnp.abs(den), jnp.exp(-m_t))
        hout = _rms_mxu(hout, mln_ref[:, hs]) * jax.nn.sigmoid(og)
        o_ref[:, GW + h * DH:GW + (h + 1) * DH] = hout
        m_out = jnp.where(lane == h, m_t, m_out)
        yield
    return m_out


def _mixer_seg_kernel(proj_ref, gate_ref, lbl_ref, hgn_ref, mln_ref, gb_ref, s_ref, c_ref, n_ref, m_ref,
                      o_ref, so_ref, co_ref, no_ref, mo_ref, b_sc, k_sc, *, seg):
    def put_n(sg, hs, val):
        no_ref[sg:sg + 1, hs] = val

    m_out = _run(_mixer_core(proj_ref, gate_ref, lbl_ref, hgn_ref, mln_ref, gb_ref, s_ref, c_ref, so_ref, co_ref,
                             n_ref[...], m_ref[...], put_n, o_ref, b_sc, k_sc, seg))
    for sg in range(TILE // seg):
        last = sg * seg + seg - 1
        mo_ref[sg:sg + 1, :] = m_out[last:last + 1]


def _mixer_seg(proj, gates, lb_logits, hg_norm, ml_norm, gate_bias, s0, c0, n0, m0, batch, seq):
    n = batch * seq
    assert TILE % seq == 0 and seq % SUB == 0 and n % TILE == 0
    nseg = TILE // seq
    row = lambda shape: pl.BlockSpec(shape, lambda i: (i, 0))
    st_spec = pl.BlockSpec((nseg, HEADS, DH, DH), lambda i: (i, 0, 0, 0))
    st_shape = jax.ShapeDtypeStruct((batch, HEADS, DH, DH), F32)
    m_rows = jnp.repeat(jnp.pad(m0, ((0, 0), (0, DH - HEADS))), seq, axis=0)
    o, s_new, c_new, n_new, m_new = pl.pallas_call(
        functools.partial(_mixer_seg_kernel, seg=seq),
        grid=(n // TILE,),
        in_specs=[row((TILE, 8 * GW)), row((TILE, 2 * DH)),
                  _resident((2, GW)), _resident((1, GW)), _resident((1, GW)), _resident((1, 2 * DH)),
                  st_spec, st_spec, row((nseg, GW)), row((TILE, DH))],
        out_specs=[row((TILE, 2 * GW)), st_spec, st_spec, row((nseg, GW)), row((nseg, DH))],
        out_shape=[jax.ShapeDtypeStruct((n, 2 * GW), F32), st_shape, st_shape,
                   jax.ShapeDtypeStruct((batch, GW), F32), jax.ShapeDtypeStruct((batch, DH), F32)],
        scratch_shapes=[pltpu.VMEM((HEADS, TILE, DH), F32), pltpu.VMEM((HEADS, TILE, DH), F32)],
        compiler_params=pltpu.CompilerParams(
            dimension_semantics=("arbitrary",), vmem_limit_bytes=VMEM_LIMIT),
        name="mixer_seg",
    )(proj, gates, lb_logits, hg_norm, ml_norm, gate_bias, s0, c0, n0.reshape(batch, GW), m_rows)
    return o, s_new, c_new, n_new.reshape(batch, HEADS, DH), m_new[:, :HEADS]


SEQS_PER_STEP = 2


def _mixer_carry_kernel(proj_ref, gate_ref, lbl_ref, hgn_ref, mln_ref, gb_ref, s_ref, c_ref, n_ref, m_ref,
                        o_ref, so_ref, co_ref, no_ref, mo_ref, b_sc, k_sc):
    @pl.when(pl.program_id(1) == 0)
    def _():
        so_ref[...] = s_ref[...]
        co_ref[...] = c_ref[...]
        no_ref[...] = n_ref[...]
        mo_ref[...] = m_ref[...]

    tiles = []
    for j in range(SEQS_PER_STEP):
        def put_n(sg, hs, val, j=j):
            no_ref[j, :, hs] = val

        state, cell = so_ref.at[j:j + 1], co_ref.at[j:j + 1]
        tiles.append(_mixer_core(proj_ref.at[j], gate_ref.at[j], lbl_ref, hgn_ref, mln_ref, gb_ref,
                                 state, cell, state, cell, no_ref[j], jnp.broadcast_to(mo_ref[j], (TILE, DH)),
                                 put_n, o_ref.at[j], b_sc.at[j], k_sc.at[j], TILE))
    for j, m_out in enumerate(_interleave(*tiles)):
        mo_ref[j] = m_out[TILE - 1:TILE]


def _mixer_carry(proj, gates, lb_logits, hg_norm, ml_norm, gate_bias, s0, c0, n0, m0, batch, seq):
    assert seq % TILE == 0 and batch % SEQS_PER_STEP == 0
    nb = SEQS_PER_STEP
    tok = lambda w: pl.BlockSpec((nb, TILE, w), lambda b, t: (b, t, 0))
    st_spec = pl.BlockSpec((nb, HEADS, DH, DH), lambda b, t: (b, 0, 0, 0))
    n_spec = pl.BlockSpec((nb, 1, GW), lambda b, t: (b, 0, 0))
    m_spec = pl.BlockSpec((nb, 1, DH), lambda b, t: (b, 0, 0))
    st_shape = jax.ShapeDtypeStruct((batch, HEADS, DH, DH), F32)
    o, s_new, c_new, n_new, m_new = pl.pallas_call(
        _mixer_carry_kernel,
        grid=(batch // nb, seq // TILE),
        in_specs=[tok(8 * GW), tok(2 * DH),
                  _resident((2, GW)), _resident((1, GW)), _resident((1, GW)), _resident((1, 2 * DH)),
                  st_spec, st_spec, n_spec, m_spec],
        out_specs=[tok(2 * GW), st_spec, st_spec, n_spec, m_spec],
        out_shape=[jax.ShapeDtypeStruct((batch, seq, 2 * GW), F32), st_shape, st_shape,
                   jax.ShapeDtypeStruct((batch, 1, GW), F32), jax.ShapeDtypeStruct((batch, 1, DH), F32)],
        scratch_shapes=[pltpu.VMEM((nb, HEADS, TILE, DH), F32), pltpu.VMEM((nb, HEADS, TILE, DH), F32)],
        compiler_params=pltpu.CompilerParams(
            dimension_semantics=("arbitrary", "arbitrary"), vmem_limit_bytes=VMEM_LIMIT),
        name="mixer_carry",
    )(proj.reshape(batch, seq, 8 * GW), gates.reshape(batch, seq, 2 * DH), lb_logits, hg_norm, ml_norm, gate_bias,
      s0, c0, n0.reshape(batch, 1, GW), jnp.pad(m0, ((0, 0), (0, DH - HEADS))).reshape(batch, 1, DH))
    return o, s_new, c_new, n_new.reshape(batch, HEADS, DH), m_new.reshape(batch, DH)[:, :HEADS]


def _ffn_core(x, o, wo_ref, g2_ref, wg_ref, wv_ref, cw_ref, cb_ref, wd_ref, gf_ref, u_sc, g_sc, bb, tt):
    m = bb * tt
    x1 = x + _bdot(o, wo_ref[...])
    h2 = _rms(x1, g2_ref[...]).astype(BF16)
    for c in range(0, D_FF, FF_CHUNK):
        yield
        cs = slice(c, c + FF_CHUNK)
        u = jnp.dot(h2, wg_ref[:, cs], preferred_element_type=F32).reshape(bb, tt, FF_CHUNK)
        val = jnp.dot(h2, wv_ref[:, cs], preferred_element_type=F32).reshape(bb, tt, FF_CHUNK)
        u_sc[:, SUB:SUB + tt, cs] = u
        u1 = u_sc[:, SUB - 1:SUB - 1 + tt, cs]
        u2 = u_sc[:, SUB - 2:SUB - 2 + tt, cs]
        conv = cb_ref[:, cs] + cw_ref[0:1, cs] * u2 + cw_ref[1:2, cs] * u1 + cw_ref[2:3, cs] * u
        g_sc[:, cs] = (jax.nn.gelu(conv) * val).reshape(m, FF_CHUNK).astype(BF16)
    u_sc[:, SUB - 2:SUB, :] = u_sc[:, SUB + tt - 2:SUB + tt, :]
    yield
    y = x1 + jnp.dot(g_sc[...], wd_ref[...], preferred_element_type=F32)
    return _rms(y, gf_ref[...])


def _ffn_kernel(x_ref, o_ref, buf_ref, wo_ref, g2_ref, wg_ref, wv_ref, cw_ref, cb_ref, wd_ref, gf_ref,
                y_ref, nb_ref, u_sc, g_sc, *, bb, tt):
    m = bb * tt

    @pl.when(pl.program_id(1) == 0)
    def _():
        u_sc[:, SUB - 2:SUB, :] = buf_ref[...]

    y = _run(_ffn_core(x_ref[...].reshape(m, D_MODEL), o_ref[...].reshape(m, D_MODEL), wo_ref, g2_ref, wg_ref,
                       wv_ref, cw_ref, cb_ref, wd_ref, gf_ref, u_sc, g_sc, bb, tt))
    nb_ref[...] = u_sc[:, SUB - 2:SUB, :]
    y_ref[...] = y.reshape(bb, tt, D_MODEL)


def _ffn(x, o, buf, wo, g2, wg, wv, cw, cb, wd, gf, bb, tt):
    batch, seq, _ = x.shape
    assert batch % bb == 0 and seq % tt == 0 and tt % SUB == 0 and tt >= CONV_W - 1
    kern = functools.partial(_ffn_kernel, bb=bb, tt=tt)
    tok = pl.BlockSpec((bb, tt, D_MODEL), lambda b, t: (b, t, 0))
    hist = pl.BlockSpec((bb, CONV_W - 1, D_FF), lambda b, t: (b, 0, 0))
    return pl.pallas_call(
        kern,
        grid=(batch // bb, seq // tt),
        in_specs=[tok, tok, hist,
                  _resident((D_MODEL, D_MODEL)), _resident((1, D_MODEL)),
                  _resident((D_MODEL, D_FF)), _resident((D_MODEL, D_FF)),
                  _resident((CONV_W, D_FF)), _resident((1, D_FF)),
                  _resident((D_FF, D_MODEL)), _resident((1, D_MODEL))],
        out_specs=[tok, hist],
        out_shape=[jax.ShapeDtypeStruct((batch, seq, D_MODEL), F32),
                   jax.ShapeDtypeStruct((batch, CONV_W - 1, D_FF), F32)],
        scratch_shapes=[pltpu.VMEM((bb, SUB + tt, D_FF), F32), pltpu.VMEM((bb * tt, D_FF), BF16)],
        compiler_params=pltpu.CompilerParams(
            dimension_semantics=("arbitrary", "arbitrary"), vmem_limit_bytes=VMEM_LIMIT),
        name="ffn",
    )(x, o, buf, wo, g2, wg, wv, cw, cb, wd, gf)


ROWS_PER_STEP = 512


def _layer(x, s0, c0, n0, m0, buf0, p):
    batch, seq, _ = x.shape
    n = batch * seq
    proj, gates = _inproj(x.reshape(n, D_MODEL), p["g1"], p["w_all"], p["w_gate2"], ROWS_PER_STEP)
    mixer = _mixer_carry if seq >= TILE else _mixer_seg
    o, s_new, c_new, n_new, m_new = mixer(proj, gates, p["lb_logits"], p["hg_norm"], p["ml_norm"],
                                          p["gate_bias"], s0, c0, n0, m0, batch, seq)
    tt = min(seq, ROWS_PER_STEP)
    bb = 1 if tt == ROWS_PER_STEP else ROWS_PER_STEP // (2 * tt)
    y, buf_new = _ffn(x, o.reshape(batch, seq, D_MODEL), buf0, p["wo"], p["g2"], p["wg"], p["wv"],
                      p["cw"], p["cb"], p["wd"], p["gf"], bb, tt)
    return y, s_new[None], c_new[None], n_new[None], m_new[None], buf_new[None]


def kernel(x_prompt, x_sample, state_hgrn_S, state_mlstm_C, state_mlstm_n, state_mlstm_m, state_conv, norm1_g, w_in, hg_lb_logits, hg_norm_g, ml_b_ig, ml_b_fg, ml_norm_g, w_out, norm2_g, w_gate, w_val, conv_w, conv_b, w_down, final_norm_g):
    assert norm1_g.shape[0] == 1, "single-layer trunk"
    w = w_in[0]
    gate_cols = w[:, 8 * GW:]
    zpad = jnp.zeros((D_MODEL, DH - HEADS), w.dtype)
    w_gate2 = jnp.concatenate([gate_cols[:, :HEADS], zpad, gate_cols[:, HEADS:], zpad], axis=1)
    bpad = jnp.zeros((DH - HEADS,), F32)
    p = {
        "g1": norm1_g, "w_all": w.astype(BF16), "w_gate2": w_gate2.astype(BF16),
        "lb_logits": hg_lb_logits, "hg_norm": hg_norm_g, "ml_norm": ml_norm_g,
        "gate_bias": jnp.concatenate([ml_b_ig[0], bpad, ml_b_fg[0], bpad])[None],
        "wo": w_out[0].astype(BF16), "g2": norm2_g, "wg": w_gate[0].astype(BF16),
        "wv": w_val[0].astype(BF16), "cw": conv_w[0], "cb": conv_b, "wd": w_down[0].astype(BF16),
        "gf": final_norm_g[None],
    }
    b = x_prompt.shape[0]
    zs = jnp.zeros((b, HEADS, DH, DH), F32)
    prompt = _layer(x_prompt, zs, zs, jnp.zeros((b, HEADS, DH), F32), jnp.zeros((b, HEADS), F32),
                    jnp.zeros((b, CONV_W - 1, D_FF), F32), p)
    sample = _layer(x_sample, state_hgrn_S[0], state_mlstm_C[0], state_mlstm_n[0], state_mlstm_m[0],
                    state_conv[0], p)
    out = []
    for a, c in zip(prompt, sample):
        out += [a, c]
    return tuple(out)
```

```python
import functools

import jax
import jax.numpy as jnp
from jax import lax
from jax.experimental import pallas as pl
from jax.experimental.pallas import tpu as pltpu

F32 = jnp.float32
BF16 = jnp.bfloat16
HI = lax.Precision.HIGHEST
NT = (((1,), (1,)), ((), ()))
TN = (((0,), (0,)), ((), ()))

D_MODEL = 1024
HEADS = 4
DH = 128
GW = HEADS * DH
D_FF = 2816
CONV_W = 3
EPS = 1e-6
TILE = 128
SUB = 8
FF_CHUNK = 256
VMEM_LIMIT = 56 * 1024 * 1024


def _rms(x, g):
    return x * lax.rsqrt(jnp.mean(x * x, axis=-1, keepdims=True) + EPS) * g


def _rms_mxu(x, g):
    ms = jnp.dot((x * x).astype(BF16), jnp.full((DH, DH), 1.0 / DH, BF16), preferred_element_type=F32)
    return x * lax.rsqrt(ms + EPS) * g


def _bdot(a, b):
    return jnp.dot(a.astype(BF16), b.astype(BF16), preferred_element_type=F32)


def _resident(shape):
    zeros = (0,) * len(shape)
    return pl.BlockSpec(shape, lambda *_: zeros, pipeline_mode=pl.Buffered(1))


def _run(gen):
    try:
        while True:
            next(gen)
    except StopIteration as stop:
        return stop.value


def _interleave(primary, *others):
    gens = [primary, *others]
    done = [False] * len(gens)
    vals = [None] * len(gens)

    def step(j):
        try:
            next(gens[j])
        except StopIteration as stop:
            done[j], vals[j] = True, stop.value

    turn = 0
    while not all(done):
        if not done[0]:
            step(0)
        pending = [j for j in range(1, len(gens)) if not done[j]]
        if pending:
            step(pending[turn % len(pending)])
            turn += 1
    return vals


def _inproj_core(x, g_ref, w_ref, wg_ref, proj_ref, gate_ref):
    hb = _rms(x, g_ref[...]).astype(BF16)
    gate_ref[...] = jnp.dot(hb, wg_ref[...], preferred_element_type=F32)
    for c in range(0, 8 * GW, GW):
        yield
        proj_ref[:, c:c + GW] = jnp.dot(hb, w_ref[:, c:c + GW], preferred_element_type=F32)


def _inproj_kernel(x_ref, g_ref, w_ref, wg_ref, proj_ref, gate_ref):
    _run(_inproj_core(x_ref[...], g_ref, w_ref, wg_ref, proj_ref, gate_ref))


IN_COLS = 8 * GW + 2 * HEADS


def _inproj(x2, g1, w_all, w_gate2, tm):
    n = x2.shape[0]
    return pl.pallas_call(
        _inproj_kernel,
        grid=(n // tm,),
        in_specs=[
            pl.BlockSpec((tm, D_MODEL), lambda i: (i, 0)),
            _resident((1, D_MODEL)),
            _resident((D_MODEL, IN_COLS)),
            _resident((D_MODEL, 2 * DH)),
        ],
        out_specs=[
            pl.BlockSpec((tm, 8 * GW), lambda i: (i, 0)),
            pl.BlockSpec((tm, 2 * DH), lambda i: (i, 0)),
        ],
        out_shape=[
            jax.ShapeDtypeStruct((n, 8 * GW), F32),
            jax.ShapeDtypeStruct((n, 2 * DH), F32),
        ],
        compiler_params=pltpu.CompilerParams(
            dimension_semantics=("arbitrary",), vmem_limit_bytes=VMEM_LIMIT),
        name="inproj",
    )(x2, g1, w_all, w_gate2)


def _seg_last(x, seg):
    nseg = TILE // seg
    w = x.shape[-1]
    if nseg == 1:
        last = jnp.broadcast_to(x[TILE - SUB:TILE][SUB - 1:SUB], (SUB, w))
        return jnp.concatenate([last] * (TILE // SUB), axis=0)
    y = x.reshape(nseg, seg, w)[:, seg - 1:seg, :]
    return jnp.broadcast_to(y, (nseg, seg, w)).reshape(TILE, w)


def _hgrn_intra(q, k, v, b, seg, b_ref, k_ref, v_ref):
    sub = lax.broadcasted_iota(jnp.int32, (SUB, 1), 0)
    blocks = []
    for r0 in range(0, TILE, SUB):
        bb = b[r0:r0 + SUB]
        qb = q[r0:r0 + SUB]
        acc = jnp.zeros((SUB, DH), F32)
        for s in range(SUB):
            bs = jnp.broadcast_to(b_ref[r0 + s:r0 + s + 1], (SUB, DH))
            ks = jnp.broadcast_to(k_ref[r0 + s:r0 + s + 1], (SUB, DH))
            vs = jnp.broadcast_to(v_ref[r0 + s:r0 + s + 1], (SUB, DH))
            p = jnp.exp2(bb - bs) * (qb * ks)
            col = jnp.sum(p, axis=-1, keepdims=True)
            col = jnp.where(sub >= s, col, 0.0)
            acc = acc + col * vs
        blocks.append(acc)
        if r0 % (4 * SUB) == 3 * SUB:
            yield
    intra = jnp.concatenate(blocks, axis=0)

    if seg > SUB:
        ri = lax.broadcasted_iota(jnp.int32, (TILE, TILE), 0)
        ci = lax.broadcasted_iota(jnp.int32, (TILE, TILE), 1)
        a_off = jnp.zeros((TILE, TILE), F32)
        w = SUB
        while w < seg:
            zeros = jnp.zeros((w, DH), F32)
            qs, ks = [], []
            for r0 in range(0, TILE, 2 * w):
                ref = b[r0 + w - 1:r0 + w]
                ks += [k[r0:r0 + w] * jnp.exp2(ref - b[r0:r0 + w]), zeros]
                qs += [zeros, q[r0 + w:r0 + 2 * w] * jnp.exp2(b[r0 + w:r0 + 2 * w] - ref)]
            qt = jnp.concatenate(qs, axis=0).astype(BF16)
            kt = jnp.concatenate(ks, axis=0).astype(BF16)
            a_lvl = lax.dot_general(qt, kt, NT, preferred_element_type=F32)
            same_block = (ri // (2 * w)) == (ci // (2 * w))
            a_off = a_off + jnp.where(same_block, a_lvl, 0.0)
            w *= 2
        yield
        intra = intra + _bdot(a_off, v)
    return intra


def _mixer_core(proj_ref, gate_ref, lbl_ref, hgn_ref, mln_ref, gb_ref,
                s_in, c_in, s_out, c_out, n_all, m_rows, put_n, o_ref, b_sc, k_sc, seg):
    nseg = TILE // seg
    ri = lax.broadcasted_iota(jnp.int32, (TILE, TILE), 0)
    ci = lax.broadcasted_iota(jnp.int32, (TILE, TILE), 1)
    tri = ((ri // seg) == (ci // seg)) & (ci <= ri)
    tri_f = tri.astype(F32)
    lane = lax.broadcasted_iota(jnp.int32, (TILE, DH), 1)

    lg = lbl_ref[...]
    ex = jnp.exp(lg - jnp.max(lg, axis=0, keepdims=True))
    lb_all = ex[0:1] / jnp.sum(ex, axis=0, keepdims=True)

    ig_all = gate_ref[:, 0:DH] + gb_ref[:, 0:DH]
    lf_all = jax.nn.log_sigmoid(gate_ref[:, DH:2 * DH] + gb_ref[:, DH:2 * DH])
    a_all = jnp.dot(tri_f, lf_all, precision=HI, preferred_element_type=F32)
    a_all_t = a_all.T
    ig_all_t = ig_all.T
    m_out = jnp.zeros((TILE, DH), F32)
    yield

    for h in range(HEADS):
        hs = slice(h * DH, (h + 1) * DH)

        q = proj_ref[:, 0 * GW + h * DH:0 * GW + (h + 1) * DH]
        hf = proj_ref[:, 1 * GW + h * DH:1 * GW + (h + 1) * DH]
        v = proj_ref[:, 2 * GW + h * DH:2 * GW + (h + 1) * DH]
        og = proj_ref[:, 3 * GW + h * DH:3 * GW + (h + 1) * DH]
        lb = lb_all[:, hs]
        f = lb + (1.0 - lb) * jax.nn.sigmoid(hf)
        k = 1.0 - f
        b = jnp.dot(tri_f, jnp.log2(f), precision=HI, preferred_element_type=F32)
        b_last = _seg_last(b, seg)
        qe = (q * jnp.exp2(b)).astype(BF16)
        kd = (k * jnp.exp2(b_last - b)).astype(BF16)
        vb16 = v.astype(BF16)
        e_last = jnp.exp2(b_last)
        b_sc[h] = b
        k_sc[h] = k
        yield
        out = yield from _hgrn_intra(q, k, v, b, seg, b_sc.at[h], k_sc.at[h],
                                     proj_ref.at[:, 2 * GW + h * DH:2 * GW + (h + 1) * DH])
        inter = []
        for sg in range(nseg):
            rows = slice(sg * seg, (sg + 1) * seg)
            st = s_in[sg, h]
            inter.append(jnp.dot(qe[rows], st.astype(BF16), preferred_element_type=F32))
            decay = jnp.broadcast_to(e_last[sg * seg:sg * seg + 1], (DH, DH)).T
            s_out[sg, h] = decay * st + lax.dot_general(kd[rows], vb16[rows], TN, preferred_element_type=F32)
        out = out + jnp.concatenate(inter, axis=0)
        out = _rms_mxu(out, hgn_ref[:, hs]) * (og * jax.nn.sigmoid(og))
        o_ref[:, hs] = out
        yield

        q = proj_ref[:, 4 * GW + h * DH:4 * GW + (h + 1) * DH]
        k = proj_ref[:, 5 * GW + h * DH:5 * GW + (h + 1) * DH] * (DH ** -0.5)
        v = proj_ref[:, 6 * GW + h * DH:6 * GW + (h + 1) * DH]
        og = proj_ref[:, 7 * GW + h * DH:7 * GW + (h + 1) * DH]
        a_col = jnp.broadcast_to(a_all[:, h:h + 1], (TILE, TILE))
        i_col = jnp.broadcast_to(ig_all[:, h:h + 1], (TILE, TILE))
        m_col = jnp.broadcast_to(m_rows[:, h:h + 1], (TILE, TILE))
        a_row = a_all_t[h:h + 1]
        i_row = ig_all_t[h:h + 1]
        log_d = jnp.where(tri, a_col - a_row + i_row, -jnp.inf)
        log_inter = a_col + m_col
        m_t = jnp.maximum(log_inter, jnp.max(log_d, axis=-1, keepdims=True))
        d_w = jnp.exp(log_d - m_t)
        w_i = jnp.exp(log_inter - m_t)
        qb16 = q.astype(BF16)
        kb16 = k.astype(BF16)
        vb16 = v.astype(BF16)
        s_w = lax.dot_general(qb16, kb16, NT, preferred_element_type=F32) * d_w
        nd = jnp.dot(s_w.astype(BF16), jnp.concatenate([vb16, jnp.ones((TILE, DH), BF16)], axis=1),
                     preferred_element_type=F32)
        num, den = nd[:, 0:DH], nd[:, DH:2 * DH]
        yield

        m_new = _seg_last(m_t, seg)
        a_end = _seg_last(a_col, seg)
        w_end = jnp.exp(a_end - a_col + i_col - m_new)
        f_end = jnp.exp(a_end + m_col - m_new)
        kw = k * w_end
        kw16 = kw.astype(BF16)
        qc, qn = [], []
        for sg in range(nseg):
            rows = slice(sg * seg, (sg + 1) * seg)
            last = sg * seg + seg - 1
            ct = c_in[sg, h]
            qc.append(jnp.dot(qb16[rows], ct.astype(BF16), preferred_element_type=F32))
            qn.append(q[rows] * n_all[sg:sg + 1, hs])
            fe = f_end[last:last + 1, 0:1]
            c_out[sg, h] = fe * ct + lax.dot_general(kw16[rows], vb16[rows], TN, preferred_element_type=F32)
            put_n(sg, hs, fe * n_all[sg:sg + 1, hs] + jnp.sum(kw[rows], axis=0, keepdims=True))
        num = w_i * jnp.concatenate(qc, axis=0) + num
        qn = jnp.dot(jnp.concatenate(qn, axis=0).astype(BF16), jnp.ones((DH, DH), BF16), preferred_element_type=F32)
        den = w_i * qn + den
        hout = num / jnp.maximum(jnp.abs(den), jnp.exp(-m_t))
        hout = _rms_mxu(hout, mln_ref[:, hs]) * jax.nn.sigmoid(og)
        o_ref[:, GW + h * DH:GW + (h + 1) * DH] = hout
        m_out = jnp.where(lane == h, m_t, m_out)
        yield
    return m_out


def _mixer_seg_kernel(proj_ref, gate_ref, lbl_ref, hgn_ref, mln_ref, gb_ref, s_ref, c_ref, n_ref, m_ref,
                      o_ref, so_ref, co_ref, no_ref, mo_ref, b_sc, k_sc, *, seg):
    def put_n(sg, hs, val):
        no_ref[sg:sg + 1, hs] = val

    m_out = _run(_mixer_core(proj_ref, gate_ref, lbl_ref, hgn_ref, mln_ref, gb_ref, s_ref, c_ref, so_ref, co_ref,
                             n_ref[...], m_ref[...], put_n, o_ref, b_sc, k_sc, seg))
    for sg in range(TILE // seg):
        last = sg * seg + seg - 1
        mo_ref[sg:sg + 1, :] = m_out[last:last + 1]


def _mixer_seg(proj, gates, lb_logits, hg_norm, ml_norm, gate_bias, s0, c0, n0, m0, batch, seq):
    n = batch * seq
    assert TILE % seq == 0 and seq % SUB == 0 and n % TILE == 0
    nseg = TILE // seq
    row = lambda shape: pl.BlockSpec(shape, lambda i: (i, 0))
    st_spec = pl.BlockSpec((nseg, HEADS, DH, DH), lambda i: (i, 0, 0, 0))
    st_shape = jax.ShapeDtypeStruct((batch, HEADS, DH, DH), F32)
    m_rows = jnp.repeat(jnp.pad(m0, ((0, 0), (0, DH - HEADS))), seq, axis=0)
    o, s_new, c_new, n_new, m_new = pl.pallas_call(
        functools.partial(_mixer_seg_kernel, seg=seq),
        grid=(n // TILE,),
        in_specs=[row((TILE, 8 * GW)), row((TILE, 2 * DH)),
                  _resident((2, GW)), _resident((1, GW)), _resident((1, GW)), _resident((1, 2 * DH)),
                  st_spec, st_spec, row((nseg, GW)), row((TILE, DH))],
        out_specs=[row((TILE, 2 * GW)), st_spec, st_spec, row((nseg, GW)), row((nseg, DH))],
        out_shape=[jax.ShapeDtypeStruct((n, 2 * GW), F32), st_shape, st_shape,
                   jax.ShapeDtypeStruct((batch, GW), F32), jax.ShapeDtypeStruct((batch, DH), F32)],
        scratch_shapes=[pltpu.VMEM((HEADS, TILE, DH), F32), pltpu.VMEM((HEADS, TILE, DH), F32)],
        compiler_params=pltpu.CompilerParams(
            dimension_semantics=("arbitrary",), vmem_limit_bytes=VMEM_LIMIT),
        name="mixer_seg",
    )(proj, gates, lb_logits, hg_norm, ml_norm, gate_bias, s0, c0, n0.reshape(batch, GW), m_rows)
    return o, s_new, c_new, n_new.reshape(batch, HEADS, DH), m_new[:, :HEADS]


SEQS_PER_STEP = 2


def _mixer_carry_kernel(proj_ref, gate_ref, lbl_ref, hgn_ref, mln_ref, gb_ref, s_ref, c_ref, n_ref, m_ref,
                        o_ref, so_ref, co_ref, no_ref, mo_ref, b_sc, k_sc):
    @pl.when(pl.program_id(1) == 0)
    def _():
        so_ref[...] = s_ref[...]
        co_ref[...] = c_ref[...]
        no_ref[...] = n_ref[...]
        mo_ref[...] = m_ref[...]

    tiles = []
    for j in range(SEQS_PER_STEP):
        def put_n(sg, hs, val, j=j):
            no_ref[j, :, hs] = val

        state, cell = so_ref.at[j:j + 1], co_ref.at[j:j + 1]
        tiles.append(_mixer_core(proj_ref.at[j], gate_ref.at[j], lbl_ref, hgn_ref, mln_ref, gb_ref,
                                 state, cell, state, cell, no_ref[j], jnp.broadcast_to(mo_ref[j], (TILE, DH)),
                                 put_n, o_ref.at[j], b_sc.at[j], k_sc.at[j], TILE))
    for j, m_out in enumerate(_interleave(*tiles)):
        mo_ref[j] = m_out[TILE - 1:TILE]


def _mixer_carry(proj, gates, lb_logits, hg_norm, ml_norm, gate_bias, s0, c0, n0, m0, batch, seq):
    assert seq % TILE == 0 and batch % SEQS_PER_STEP == 0
    nb = SEQS_PER_STEP
    tok = lambda w: pl.BlockSpec((nb, TILE, w), lambda b, t: (b, t, 0))
    st_spec = pl.BlockSpec((nb, HEADS, DH, DH), lambda b, t: (b, 0, 0, 0))
    n_spec = pl.BlockSpec((nb, 1, GW), lambda b, t: (b, 0, 0))
    m_spec = pl.BlockSpec((nb, 1, DH), lambda b, t: (b, 0, 0))
    st_shape = jax.ShapeDtypeStruct((batch, HEADS, DH, DH), F32)
    o, s_new, c_new, n_new, m_new = pl.pallas_call(
        _mixer_carry_kernel,
        grid=(batch // nb, seq // TILE),
        in_specs=[tok(8 * GW), tok(2 * DH),
                  _resident((2, GW)), _resident((1, GW)), _resident((1, GW)), _resident((1, 2 * DH)),
                  st_spec, st_spec, n_spec, m_spec],
        out_specs=[tok(2 * GW), st_spec, st_spec, n_spec, m_spec],
        out_shape=[jax.ShapeDtypeStruct((batch, seq, 2 * GW), F32), st_shape, st_shape,
                   jax.ShapeDtypeStruct((batch, 1, GW), F32), jax.ShapeDtypeStruct((batch, 1, DH), F32)],
        scratch_shapes=[pltpu.VMEM((nb, HEADS, TILE, DH), F32), pltpu.VMEM((nb, HEADS, TILE, DH), F32)],
        compiler_params=pltpu.CompilerParams(
            dimension_semantics=("arbitrary", "arbitrary"), vmem_limit_bytes=VMEM_LIMIT),
        name="mixer_carry",
    )(proj.reshape(batch, seq, 8 * GW), gates.reshape(batch, seq, 2 * DH), lb_logits, hg_norm, ml_norm, gate_bias,
      s0, c0, n0.reshape(batch, 1, GW), jnp.pad(m0, ((0, 0), (0, DH - HEADS))).reshape(batch, 1, DH))
    return o, s_new, c_new, n_new.reshape(batch, HEADS, DH), m_new.reshape(batch, DH)[:, :HEADS]


def _ffn_core(x, o, wo_ref, g2_ref, wg_ref, wv_ref, cw_ref, cb_ref, wd_ref, gf_ref, u_sc, g_sc, bb, tt):
    m = bb * tt
    x1 = x + _bdot(o, wo_ref[...])
    h2 = _rms(x1, g2_ref[...]).astype(BF16)
    for c in range(0, D_FF, FF_CHUNK):
        yield
        cs = slice(c, c + FF_CHUNK)
        u = jnp.dot(h2, wg_ref[:, cs], preferred_element_type=F32).reshape(bb, tt, FF_CHUNK)
        val = jnp.dot(h2, wv_ref[:, cs], preferred_element_type=F32).reshape(bb, tt, FF_CHUNK)
        u_sc[:, SUB:SUB + tt, cs] = u
        u1 = u_sc[:, SUB - 1:SUB - 1 + tt, cs]
        u2 = u_sc[:, SUB - 2:SUB - 2 + tt, cs]
        conv = cb_ref[:, cs] + cw_ref[0:1, cs] * u2 + cw_ref[1:2, cs] * u1 + cw_ref[2:3, cs] * u
        g_sc[:, cs] = (jax.nn.gelu(conv) * val).reshape(m, FF_CHUNK).astype(BF16)
    u_sc[:, SUB - 2:SUB, :] = u_sc[:, SUB + tt - 2:SUB + tt, :]
    yield
    y = x1 + jnp.dot(g_sc[...], wd_ref[...], preferred_element_type=F32)
    return _rms(y, gf_ref[...])


def _ffn_kernel(x_ref, o_ref, buf_ref, wo_ref, g2_ref, wg_ref, wv_ref, cw_ref, cb_ref, wd_ref, gf_ref,
                y_ref, nb_ref, u_sc, g_sc, *, bb, tt):
    m = bb * tt

    @pl.when(pl.program_id(1) == 0)
    def _():
        u_sc[:, SUB - 2:SUB, :] = buf_ref[...]

    y = _run(_ffn_core(x_ref[...].reshape(m, D_MODEL), o_ref[...].reshape(m, D_MODEL), wo_ref, g2_ref, wg_ref,
                       wv_ref, cw_ref, cb_ref, wd_ref, gf_ref, u_sc, g_sc, bb, tt))
    nb_ref[...] = u_sc[:, SUB - 2:SUB, :]
    y_ref[...] = y.reshape(bb, tt, D_MODEL)


def _ffn(x, o, buf, wo, g2, wg, wv, cw, cb, wd, gf, bb, tt):
    batch, seq, _ = x.shape
    assert batch % bb == 0 and seq % tt == 0 and tt % SUB == 0 and tt >= CONV_W - 1
    kern = functools.partial(_ffn_kernel, bb=bb, tt=tt)
    tok = pl.BlockSpec((bb, tt, D_MODEL), lambda b, t: (b, t, 0))
    hist = pl.BlockSpec((bb, CONV_W - 1, D_FF), lambda b, t: (b, 0, 0))
    return pl.pallas_call(
        kern,
        grid=(batch // bb, seq // tt),
        in_specs=[tok, tok, hist,
                  _resident((D_MODEL, D_MODEL)), _resident((1, D_MODEL)),
                  _resident((D_MODEL, D_FF)), _resident((D_MODEL, D_FF)),
                  _resident((CONV_W, D_FF)), _resident((1, D_FF)),
                  _resident((D_FF, D_MODEL)), _resident((1, D_MODEL))],
        out_specs=[tok, hist],
        out_shape=[jax.ShapeDtypeStruct((batch, seq, D_MODEL), F32),
                   jax.ShapeDtypeStruct((batch, CONV_W - 1, D_FF), F32)],
        scratch_shapes=[pltpu.VMEM((bb, SUB + tt, D_FF), F32), pltpu.VMEM((bb * tt, D_FF), BF16)],
        compiler_params=pltpu.CompilerParams(
            dimension_semantics=("arbitrary", "arbitrary"), vmem_limit_bytes=VMEM_LIMIT),
        name="ffn",
    )(x, o, buf, wo, g2, wg, wv, cw, cb, wd, gf)


ROWS_PER_STEP = 512


def _layer(x, s0, c0, n0, m0, buf0, p):
    batch, seq, _ = x.shape
    n = batch * seq
    proj, gates = _inproj(x.reshape(n, D_MODEL), p["g1"], p["w_all"], p["w_gate2"], ROWS_PER_STEP)
    mixer = _mixer_carry if seq >= TILE else _mixer_seg
    o, s_new, c_new, n_new, m_new = mixer(proj, gates, p["lb_logits"], p["hg_norm"], p["ml_norm"],
                                          p["gate_bias"], s0, c0, n0, m0, batch, seq)
    tt = min(seq, ROWS_PER_STEP)
    bb = 1 if tt == ROWS_PER_STEP else ROWS_PER_STEP // (2 * tt)
    y, buf_new = _ffn(x, o.reshape(batch, seq, D_MODEL), buf0, p["wo"], p["g2"], p["wg"], p["wv"],
                      p["cw"], p["cb"], p["wd"], p["gf"], bb, tt)
    return y, s_new[None], c_new[None], n_new[None], m_new[None], buf_new[None]


def kernel(x_prompt, x_sample, state_hgrn_S, state_mlstm_C, state_mlstm_n, state_mlstm_m, state_conv, norm1_g, w_in, hg_lb_logits, hg_norm_g, ml_b_ig, ml_b_fg, ml_norm_g, w_out, norm2_g, w_gate, w_val, conv_w, conv_b, w_down, final_norm_g):
    assert norm1_g.shape[0] == 1, "single-layer trunk"
    w = w_in[0]
    gate_cols = w[:, 8 * GW:]
    zpad = jnp.zeros((D_MODEL, DH - HEADS), w.dtype)
    w_gate2 = jnp.concatenate([gate_cols[:, :HEADS], zpad, gate_cols[:, HEADS:], zpad], axis=1)
    bpad = jnp.zeros((DH - HEADS,), F32)
    p = {
        "g1": norm1_g, "w_all": w.astype(BF16), "w_gate2": w_gate2.astype(BF16),
        "lb_logits": hg_lb_logits, "hg_norm": hg_norm_g, "ml_norm": ml_norm_g,
        "gate_bias": jnp.concatenate([ml_b_ig[0], bpad, ml_b_fg[0], bpad])[None],
        "wo": w_out[0].astype(BF16), "g2": norm2_g, "wg": w_gate[0].astype(BF16),
        "wv": w_val[0].astype(BF16), "cw": conv_w[0], "cb": conv_b, "wd": w_down[0].astype(BF16),
        "gf": final_norm_g[None],
    }
    b = x_prompt.shape[0]
    zs = jnp.zeros((b, HEADS, DH, DH), F32)
    prompt = _layer(x_prompt, zs, zs, jnp.zeros((b, HEADS, DH), F32), jnp.zeros((b, HEADS), F32),
                    jnp.zeros((b, CONV_W - 1, D_FF), F32), p)
    sample = _layer(x_sample, state_hgrn_S[0], state_mlstm_C[0], state_mlstm_n[0], state_mlstm_m[0],
                    state_conv[0], p)
    out = []
    for a, c in zip(prompt, sample):
        out += [a, c]
    return tuple(out)
```

```python
import functools

import jax
import jax.numpy as jnp
from jax import lax
from jax.experimental import pallas as pl
from jax.experimental.pallas import tpu as pltpu

F32 = jnp.float32
BF16 = jnp.bfloat16
HI = lax.Precision.HIGHEST
NT = (((1,), (1,)), ((), ()))
TN = (((0,), (0,)), ((), ()))

D_MODEL = 1024
HEADS = 4
DH = 128
GW = HEADS * DH
D_FF = 2816
CONV_W = 3
EPS = 1e-6
TILE = 128
SUB = 8
FF_CHUNK = 256
VMEM_LIMIT = 56 * 1024 * 1024


def _rms(x, g):
    return x * lax.rsqrt(jnp.mean(x * x, axis=-1, keepdims=True) + EPS) * g


def _rms_mxu(x, g):
    ms = jnp.dot((x * x).astype(BF16), jnp.full((DH, DH), 1.0 / DH, BF16), preferred_element_type=F32)
    return x * lax.rsqrt(ms + EPS) * g


def _bdot(a, b):
    return jnp.dot(a.astype(BF16), b.astype(BF16), preferred_element_type=F32)


def _resident(shape):
    zeros = (0,) * len(shape)
    return pl.BlockSpec(shape, lambda *_: zeros, pipeline_mode=pl.Buffered(1))


def _run(gen):
    try:
        while True:
            next(gen)
    except StopIteration as stop:
        return stop.value


def _interleave(primary, *others):
    gens = [primary, *others]
    done = [False] * len(gens)
    vals = [None] * len(gens)

    def step(j):
        try:
            next(gens[j])
        except StopIteration as stop:
            done[j], vals[j] = True, stop.value

    turn = 0
    while not all(done):
        if not done[0]:
            step(0)
        pending = [j for j in range(1, len(gens)) if not done[j]]
        if pending:
            step(pending[turn % len(pending)])
            turn += 1
    return vals


def _inproj_core(x, g_ref, w_ref, wg_ref, proj_ref, gate_ref):
    hb = _rms(x, g_ref[...]).astype(BF16)
    gate_ref[...] = jnp.dot(hb, wg_ref[...], preferred_element_type=F32)
    for c in range(0, 8 * GW, GW):
        yield
        proj_ref[:, c:c + GW] = jnp.dot(hb, w_ref[:, c:c + GW], preferred_element_type=F32)


def _inproj_kernel(x_ref, g_ref, w_ref, wg_ref, proj_ref, gate_ref):
    _run(_inproj_core(x_ref[...], g_ref, w_ref, wg_ref, proj_ref, gate_ref))


IN_COLS = 8 * GW + 2 * HEADS


def _inproj(x2, g1, w_all, w_gate2, tm):
    n = x2.shape[0]
    return pl.pallas_call(
        _inproj_kernel,
        grid=(n // tm,),
        in_specs=[
            pl.BlockSpec((tm, D_MODEL), lambda i: (i, 0)),
            _resident((1, D_MODEL)),
            _resident((D_MODEL, IN_COLS)),
            _resident((D_MODEL, 2 * DH)),
        ],
        out_specs=[
            pl.BlockSpec((tm, 8 * GW), lambda i: (i, 0)),
            pl.BlockSpec((tm, 2 * DH), lambda i: (i, 0)),
        ],
        out_shape=[
            jax.ShapeDtypeStruct((n, 8 * GW), F32),
            jax.ShapeDtypeStruct((n, 2 * DH), F32),
        ],
        compiler_params=pltpu.CompilerParams(
            dimension_semantics=("arbitrary",), vmem_limit_bytes=VMEM_LIMIT),
        name="inproj",
    )(x2, g1, w_all, w_gate2)


def _seg_last(x, seg):
    nseg = TILE // seg
    w = x.shape[-1]
    if nseg == 1:
        last = jnp.broadcast_to(x[TILE - SUB:TILE][SUB - 1:SUB], (SUB, w))
        return jnp.concatenate([last] * (TILE // SUB), axis=0)
    y = x.reshape(nseg, seg, w)[:, seg - 1:seg, :]
    return jnp.broadcast_to(y, (nseg, seg, w)).reshape(TILE, w)


def _hgrn_intra(q, k, v, b, seg, b_ref, k_ref, v_ref):
    sub = lax.broadcasted_iota(jnp.int32, (SUB, 1), 0)
    blocks = []
    for r0 in range(0, TILE, SUB):
        bb = b[r0:r0 + SUB]
        qb = q[r0:r0 + SUB]
        acc = jnp.zeros((SUB, DH), F32)
        for s in range(SUB):
            bs = jnp.broadcast_to(b_ref[r0 + s:r0 + s + 1], (SUB, DH))
            ks = jnp.broadcast_to(k_ref[r0 + s:r0 + s + 1], (SUB, DH))
            vs = jnp.broadcast_to(v_ref[r0 + s:r0 + s + 1], (SUB, DH))
            p = jnp.exp2(bb - bs) * (qb * ks)
            col = jnp.sum(p, axis=-1, keepdims=True)
            col = jnp.where(sub >= s, col, 0.0)
            acc = acc + col * vs
        blocks.append(acc)
        if r0 % (4 * SUB) == 3 * SUB:
            yield
    intra = jnp.concatenate(blocks, axis=0)

    if seg > SUB:
        ri = lax.broadcasted_iota(jnp.int32, (TILE, TILE), 0)
        ci = lax.broadcasted_iota(jnp.int32, (TILE, TILE), 1)
        a_off = jnp.zeros((TILE, TILE), F32)
        w = SUB
        while w < seg:
            zeros = jnp.zeros((w, DH), F32)
            qs, ks = [], []
            for r0 in range(0, TILE, 2 * w):
                ref = b[r0 + w - 1:r0 + w]
                ks += [k[r0:r0 + w] * jnp.exp2(ref - b[r0:r0 + w]), zeros]
                qs += [zeros, q[r0 + w:r0 + 2 * w] * jnp.exp2(b[r0 + w:r0 + 2 * w] - ref)]
            qt = jnp.concatenate(qs, axis=0).astype(BF16)
            kt = jnp.concatenate(ks, axis=0).astype(BF16)
            a_lvl = lax.dot_general(qt, kt, NT, preferred_element_type=F32)
            same_block = (ri // (2 * w)) == (ci // (2 * w))
            a_off = a_off + jnp.where(same_block, a_lvl, 0.0)
            w *= 2
        yield
        intra = intra + _bdot(a_off, v)
    return intra


def _mixer_core(proj_ref, gate_ref, lbl_ref, hgn_ref, mln_ref, gb_ref,
                s_in, c_in, s_out, c_out, n_all, m_rows, put_n, o_ref, b_sc, k_sc, seg):
    nseg = TILE // seg
    ri = lax.broadcasted_iota(jnp.int32, (TILE, TILE), 0)
    ci = lax.broadcasted_iota(jnp.int32, (TILE, TILE), 1)
    tri = ((ri // seg) == (ci // seg)) & (ci <= ri)
    tri_f = tri.astype(F32)
    lane = lax.broadcasted_iota(jnp.int32, (TILE, DH), 1)

    lg = lbl_ref[...]
    ex = jnp.exp(lg - jnp.max(lg, axis=0, keepdims=True))
    lb_all = ex[0:1] / jnp.sum(ex, axis=0, keepdims=True)

    ig_all = gate_ref[:, 0:DH] + gb_ref[:, 0:DH]
    lf_all = jax.nn.log_sigmoid(gate_ref[:, DH:2 * DH] + gb_ref[:, DH:2 * DH])
    a_all = jnp.dot(tri_f, lf_all, precision=HI, preferred_element_type=F32)
    a_all_t = a_all.T
    ig_all_t = ig_all.T
    m_out = jnp.zeros((TILE, DH), F32)
    yield

    for h in range(HEADS):
        hs = slice(h * DH, (h + 1) * DH)

        q = proj_ref[:, 0 * GW + h * DH:0 * GW + (h + 1) * DH]
        hf = proj_ref[:, 1 * GW + h * DH:1 * GW + (h + 1) * DH]
        v = proj_ref[:, 2 * GW + h * DH:2 * GW + (h + 1) * DH]
        og = proj_ref[:, 3 * GW + h * DH:3 * GW + (h + 1) * DH]
        lb = lb_all[:, hs]
        f = lb + (1.0 - lb) * jax.nn.sigmoid(hf)
        k = 1.0 - f
        b = jnp.dot(tri_f, jnp.log2(f), precision=HI, preferred_element_type=F32)
        b_last = _seg_last(b, seg)
        qe = (q * jnp.exp2(b)).astype(BF16)
        kd = (k * jnp.exp2(b_last - b)).astype(BF16)
        vb16 = v.astype(BF16)
        e_last = jnp.exp2(b_last)
        b_sc[h] = b
        k_sc[h] = k
        yield
        out = yield from _hgrn_intra(q, k, v, b, seg, b_sc.at[h], k_sc.at[h],
                                     proj_ref.at[:, 2 * GW + h * DH:2 * GW + (h + 1) * DH])
        inter = []
        for sg in range(nseg):
            rows = slice(sg * seg, (sg + 1) * seg)
            st = s_in[sg, h]
            inter.append(jnp.dot(qe[rows], st.astype(BF16), preferred_element_type=F32))
            decay = jnp.broadcast_to(e_last[sg * seg:sg * seg + 1], (DH, DH)).T
            s_out[sg, h] = decay * st + lax.dot_general(kd[rows], vb16[rows], TN, preferred_element_type=F32)
        out = out + jnp.concatenate(inter, axis=0)
        out = _rms_mxu(out, hgn_ref[:, hs]) * (og * jax.nn.sigmoid(og))
        o_ref[:, hs] = out
        yield

        q = proj_ref[:, 4 * GW + h * DH:4 * GW + (h + 1) * DH]
        k = proj_ref[:, 5 * GW + h * DH:5 * GW + (h + 1) * DH] * (DH ** -0.5)
        v = proj_ref[:, 6 * GW + h * DH:6 * GW + (h + 1) * DH]
        og = proj_ref[:, 7 * GW + h * DH:7 * GW + (h + 1) * DH]
        a_col = jnp.broadcast_to(a_all[:, h:h + 1], (TILE, TILE))
        i_col = jnp.broadcast_to(ig_all[:, h:h + 1], (TILE, TILE))
        m_col = jnp.broadcast_to(m_rows[:, h:h + 1], (TILE, TILE))
        a_row = a_all_t[h:h + 1]
        i_row = ig_all_t[h:h + 1]
        log_d = jnp.where(tri, a_col - a_row + i_row, -jnp.inf)
        log_inter = a_col + m_col
        m_t = jnp.maximum(log_inter, jnp.max(log_d, axis=-1, keepdims=True))
        d_w = jnp.exp(log_d - m_t)
        w_i = jnp.exp(log_inter - m_t)
        qb16 = q.astype(BF16)
        kb16 = k.astype(BF16)
        vb16 = v.astype(BF16)
        s_w = lax.dot_general(qb16, kb16, NT, preferred_element_type=F32) * d_w
        nd = jnp.dot(s_w.astype(BF16), jnp.concatenate([vb16, jnp.ones((TILE, DH), BF16)], axis=1),
                     preferred_element_type=F32)
        num, den = nd[:, 0:DH], nd[:, DH:2 * DH]
        yield

        m_new = _seg_last(m_t, seg)
        a_end = _seg_last(a_col, seg)
        w_end = jnp.exp(a_end - a_col + i_col - m_new)
        f_end = jnp.exp(a_end + m_col - m_new)
        kw = k * w_end
        kw16 = kw.astype(BF16)
        qc, qn = [], []
        for sg in range(nseg):
            rows = slice(sg * seg, (sg + 1) * seg)
            last = sg * seg + seg - 1
            ct = c_in[sg, h]
            qc.append(jnp.dot(qb16[rows], ct.astype(BF16), preferred_element_type=F32))
            qn.append(q[rows] * n_all[sg:sg + 1, hs])
            fe = f_end[last:last + 1, 0:1]
            c_out[sg, h] = fe * ct + lax.dot_general(kw16[rows], vb16[rows], TN, preferred_element_type=F32)
            put_n(sg, hs, fe * n_all[sg:sg + 1, hs] + jnp.sum(kw[rows], axis=0, keepdims=True))
        num = w_i * jnp.concatenate(qc, axis=0) + num
        qn = jnp.dot(jnp.concatenate(qn, axis=0).astype(BF16), jnp.ones((DH, DH), BF16), preferred_element_type=F32)
        den = w_i * qn + den
        hout = num / jnp.maximum(jnp.abs(den), jnp.exp(-m_t))
        hout = _rms_mxu(hout, mln_ref[:, hs]) * jax.nn.sigmoid(og)
        o_ref[:, GW + h * DH:GW + (h + 1) * DH] = hout
        m_out = jnp.where(lane == h, m_t, m_out)
        yield
    return m_out


def _mixer_seg_kernel(proj_ref, gate_ref, lbl_ref, hgn_ref, mln_ref, gb_ref, s_ref, c_ref, n_ref, m_ref,
                      o_ref, so_ref, co_ref, no_ref, mo_ref, b_sc, k_sc, *, seg):
    def put_n(sg, hs, val):
        no_ref[sg:sg + 1, hs] = val

    m_out = _run(_mixer_core(proj_ref, gate_ref, lbl_ref, hgn_ref, mln_ref, gb_ref, s_ref, c_ref, so_ref, co_ref,
                             n_ref[...], m_ref[...], put_n, o_ref, b_sc, k_sc, seg))
    for sg in range(TILE // seg):
        last = sg * seg + seg - 1
        mo_ref[sg:sg + 1, :] = m_out[last:last + 1]


def _mixer_seg(proj, gates, lb_logits, hg_norm, ml_norm, gate_bias, s0, c0, n0, m0, batch, seq):
    n = batch * seq
    assert TILE % seq == 0 and seq % SUB == 0 and n % TILE == 0
    nseg = TILE // seq
    row = lambda shape: pl.BlockSpec(shape, lambda i: (i, 0))
    st_spec = pl.BlockSpec((nseg, HEADS, DH, DH), lambda i: (i, 0, 0, 0))
    st_shape = jax.ShapeDtypeStruct((batch, HEADS, DH, DH), F32)
    m_rows = jnp.repeat(jnp.pad(m0, ((0, 0), (0, DH - HEADS))), seq, axis=0)
    o, s_new, c_new, n_new, m_new = pl.pallas_call(
        functools.partial(_mixer_seg_kernel, seg=seq),
        grid=(n // TILE,),
        in_specs=[row((TILE, 8 * GW)), row((TILE, 2 * DH)),
                  _resident((2, GW)), _resident((1, GW)), _resident((1, GW)), _resident((1, 2 * DH)),
                  st_spec, st_spec, row((nseg, GW)), row((TILE, DH))],
        out_specs=[row((TILE, 2 * GW)), st_spec, st_spec, row((nseg, GW)), row((nseg, DH))],
        out_shape=[jax.ShapeDtypeStruct((n, 2 * GW), F32), st_shape, st_shape,
                   jax.ShapeDtypeStruct((batch, GW), F32), jax.ShapeDtypeStruct((batch, DH), F32)],
        scratch_shapes=[pltpu.VMEM((HEADS, TILE, DH), F32), pltpu.VMEM((HEADS, TILE, DH), F32)],
        compiler_params=pltpu.CompilerParams(
            dimension_semantics=("arbitrary",), vmem_limit_bytes=VMEM_LIMIT),
        name="mixer_seg",
    )(proj, gates, lb_logits, hg_norm, ml_norm, gate_bias, s0, c0, n0.reshape(batch, GW), m_rows)
    return o, s_new, c_new, n_new.reshape(batch, HEADS, DH), m_new[:, :HEADS]


SEQS_PER_STEP = 2


def _mixer_carry_kernel(proj_ref, gate_ref, lbl_ref, hgn_ref, mln_ref, gb_ref, s_ref, c_ref, n_ref, m_ref,
                        o_ref, so_ref, co_ref, no_ref, mo_ref, b_sc, k_sc):
    @pl.when(pl.program_id(1) == 0)
    def _():
        so_ref[...] = s_ref[...]
        co_ref[...] = c_ref[...]
        no_ref[...] = n_ref[...]
        mo_ref[...] = m_ref[...]

    tiles = []
    for j in range(SEQS_PER_STEP):
        def put_n(sg, hs, val, j=j):
            no_ref[j, :, hs] = val

        state, cell = so_ref.at[j:j + 1], co_ref.at[j:j + 1]
        tiles.append(_mixer_core(proj_ref.at[j], gate_ref.at[j], lbl_ref, hgn_ref, mln_ref, gb_ref,
                                 state, cell, state, cell, no_ref[j], jnp.broadcast_to(mo_ref[j], (TILE, DH)),
                                 put_n, o_ref.at[j], b_sc.at[j], k_sc.at[j], TILE))
    for j, m_out in enumerate(_interleave(*tiles)):
        mo_ref[j] = m_out[TILE - 1:TILE]


def _mixer_carry(proj, gates, lb_logits, hg_norm, ml_norm, gate_bias, s0, c0, n0, m0, batch, seq):
    assert seq % TILE == 0 and batch % SEQS_PER_STEP == 0
    nb = SEQS_PER_STEP
    tok = lambda w: pl.BlockSpec((nb, TILE, w), lambda b, t: (b, t, 0))
    st_spec = pl.BlockSpec((nb, HEADS, DH, DH), lambda b, t: (b, 0, 0, 0))
    n_spec = pl.BlockSpec((nb, 1, GW), lambda b, t: (b, 0, 0))
    m_spec = pl.BlockSpec((nb, 1, DH), lambda b, t: (b, 0, 0))
    st_shape = jax.ShapeDtypeStruct((batch, HEADS, DH, DH), F32)
    o, s_new, c_new, n_new, m_new = pl.pallas_call(
        _mixer_carry_kernel,
        grid=(batch // nb, seq // TILE),
        in_specs=[tok(8 * GW), tok(2 * DH),
                  _resident((2, GW)), _resident((1, GW)), _resident((1, GW)), _resident((1, 2 * DH)),
                  st_spec, st_spec, n_spec, m_spec],
        out_specs=[tok(2 * GW), st_spec, st_spec, n_spec, m_spec],
        out_shape=[jax.ShapeDtypeStruct((batch, seq, 2 * GW), F32), st_shape, st_shape,
                   jax.ShapeDtypeStruct((batch, 1, GW), F32), jax.ShapeDtypeStruct((batch, 1, DH), F32)],
        scratch_shapes=[pltpu.VMEM((nb, HEADS, TILE, DH), F32), pltpu.VMEM((nb, HEADS, TILE, DH), F32)],
        compiler_params=pltpu.CompilerParams(
            dimension_semantics=("arbitrary", "arbitrary"), vmem_limit_bytes=VMEM_LIMIT),
        name="mixer_carry",
    )(proj.reshape(batch, seq, 8 * GW), gates.reshape(batch, seq, 2 * DH), lb_logits, hg_norm, ml_norm, gate_bias,
      s0, c0, n0.reshape(batch, 1, GW), jnp.pad(m0, ((0, 0), (0, DH - HEADS))).reshape(batch, 1, DH))
    return o, s_new, c_new, n_new.reshape(batch, HEADS, DH), m_new.reshape(batch, DH)[:, :HEADS]


def _ffn_core(x, o, wo_ref, g2_ref, wg_ref, wv_ref, cw_ref, cb_ref, wd_ref, gf_ref, u_sc, g_sc, bb, tt):
    m = bb * tt
    x1 = x + _bdot(o, wo_ref[...])
    h2 = _rms(x1, g2_ref[...]).astype(BF16)
    for c in range(0, D_FF, FF_CHUNK):
        yield
        cs = slice(c, c + FF_CHUNK)
        u = jnp.dot(h2, wg_ref[:, cs], preferred_element_type=F32).reshape(bb, tt, FF_CHUNK)
        val = jnp.dot(h2, wv_ref[:, cs], preferred_element_type=F32).reshape(bb, tt, FF_CHUNK)
        u_sc[:, SUB:SUB + tt, cs] = u
        u1 = u_sc[:, SUB - 1:SUB - 1 + tt, cs]
        u2 = u_sc[:, SUB - 2:SUB - 2 + tt, cs]
        conv = cb_ref[:, cs] + cw_ref[0:1, cs] * u2 + cw_ref[1:2, cs] * u1 + cw_ref[2:3, cs] * u
        g_sc[:, cs] = (jax.nn.gelu(conv) * val).reshape(m, FF_CHUNK).astype(BF16)
    u_sc[:, SUB - 2:SUB, :] = u_sc[:, SUB + tt - 2:SUB + tt, :]
    yield
    y = x1 + jnp.dot(g_sc[...], wd_ref[...], preferred_element_type=F32)
    return _rms(y, gf_ref[...])


def _ffn_kernel(x_ref, o_ref, buf_ref, wo_ref, g2_ref, wg_ref, wv_ref, cw_ref, cb_ref, wd_ref, gf_ref,
                y_ref, nb_ref, u_sc, g_sc, *, bb, tt):
    m = bb * tt

    @pl.when(pl.program_id(1) == 0)
    def _():
        u_sc[:, SUB - 2:SUB, :] = buf_ref[...]

    y = _run(_ffn_core(x_ref[...].reshape(m, D_MODEL), o_ref[...].reshape(m, D_MODEL), wo_ref, g2_ref, wg_ref,
                       wv_ref, cw_ref, cb_ref, wd_ref, gf_ref, u_sc, g_sc, bb, tt))
    nb_ref[...] = u_sc[:, SUB - 2:SUB, :]
    y_ref[...] = y.reshape(bb, tt, D_MODEL)


def _ffn(x, o, buf, wo, g2, wg, wv, cw, cb, wd, gf, bb, tt):
    batch, seq, _ = x.shape
    assert batch % bb == 0 and seq % tt == 0 and tt % SUB == 0 and tt >= CONV_W - 1
    kern = functools.partial(_ffn_kernel, bb=bb, tt=tt)
    tok = pl.BlockSpec((bb, tt, D_MODEL), lambda b, t: (b, t, 0))
    hist = pl.BlockSpec((bb, CONV_W - 1, D_FF), lambda b, t: (b, 0, 0))
    return pl.pallas_call(
        kern,
        grid=(batch // bb, seq // tt),
        in_specs=[tok, tok, hist,
                  _resident((D_MODEL, D_MODEL)), _resident((1, D_MODEL)),
                  _resident((D_MODEL, D_FF)), _resident((D_MODEL, D_FF)),
                  _resident((CONV_W, D_FF)), _resident((1, D_FF)),
                  _resident((D_FF, D_MODEL)), _resident((1, D_MODEL))],
        out_specs=[tok, hist],
        out_shape=[jax.ShapeDtypeStruct((batch, seq, D_MODEL), F32),
                   jax.ShapeDtypeStruct((batch, CONV_W - 1, D_FF), F32)],
        scratch_shapes=[pltpu.VMEM((bb, SUB + tt, D_FF), F32), pltpu.VMEM((bb * tt, D_FF), BF16)],
        compiler_params=pltpu.CompilerParams(
            dimension_semantics=("arbitrary", "arbitrary"), vmem_limit_bytes=VMEM_LIMIT),
        name="ffn",
    )(x, o, buf, wo, g2, wg, wv, cw, cb, wd, gf)


ROWS_PER_STEP = 512


def _layer(x, s0, c0, n0, m0, buf0, p):
    batch, seq, _ = x.shape
    n = batch * seq
    tm = 2 * ROWS_PER_STEP if n >= 8 * ROWS_PER_STEP else ROWS_PER_STEP
    proj, gates = _inproj(x.reshape(n, D_MODEL), p["g1"], p["w_all"], p["w_gate2"], tm)
    mixer = _mixer_carry if seq >= TILE else _mixer_seg
    o, s_new, c_new, n_new, m_new = mixer(proj, gates, p["lb_logits"], p["hg_norm"], p["ml_norm"],
                                          p["gate_bias"], s0, c0, n0, m0, batch, seq)
    tt = min(seq, ROWS_PER_STEP)
    bb = 1 if tt == ROWS_PER_STEP else ROWS_PER_STEP // (2 * tt)
    y, buf_new = _ffn(x, o.reshape(batch, seq, D_MODEL), buf0, p["wo"], p["g2"], p["wg"], p["wv"],
                      p["cw"], p["cb"], p["wd"], p["gf"], bb, tt)
    return y, s_new[None], c_new[None], n_new[None], m_new[None], buf_new[None]


def kernel(x_prompt, x_sample, state_hgrn_S, state_mlstm_C, state_mlstm_n, state_mlstm_m, state_conv, norm1_g, w_in, hg_lb_logits, hg_norm_g, ml_b_ig, ml_b_fg, ml_norm_g, w_out, norm2_g, w_gate, w_val, conv_w, conv_b, w_down, final_norm_g):
    assert norm1_g.shape[0] == 1, "single-layer trunk"
    w = w_in[0]
    gate_cols = w[:, 8 * GW:]
    zpad = jnp.zeros((D_MODEL, DH - HEADS), w.dtype)
    w_gate2 = jnp.concatenate([gate_cols[:, :HEADS], zpad, gate_cols[:, HEADS:], zpad], axis=1)
    bpad = jnp.zeros((DH - HEADS,), F32)
    p = {
        "g1": norm1_g, "w_all": w.astype(BF16), "w_gate2": w_gate2.astype(BF16),
        "lb_logits": hg_lb_logits, "hg_norm": hg_norm_g, "ml_norm": ml_norm_g,
        "gate_bias": jnp.concatenate([ml_b_ig[0], bpad, ml_b_fg[0], bpad])[None],
        "wo": w_out[0].astype(BF16), "g2": norm2_g, "wg": w_gate[0].astype(BF16),
        "wv": w_val[0].astype(BF16), "cw": conv_w[0], "cb": conv_b, "wd": w_down[0].astype(BF16),
        "gf": final_norm_g[None],
    }
    b = x_prompt.shape[0]
    zs = jnp.zeros((b, HEADS, DH, DH), F32)
    prompt = _layer(x_prompt, zs, zs, jnp.zeros((b, HEADS, DH), F32), jnp.zeros((b, HEADS), F32),
                    jnp.zeros((b, CONV_W - 1, D_FF), F32), p)
    sample = _layer(x_sample, state_hgrn_S[0], state_mlstm_C[0], state_mlstm_n[0], state_mlstm_m[0],
                    state_conv[0], p)
    out = []
    for a, c in zip(prompt, sample):
        out += [a, c]
    return tuple(out)
```

```python
import functools

import jax
import jax.numpy as jnp
from jax import lax
from jax.experimental import pallas as pl
from jax.experimental.pallas import tpu as pltpu

F32 = jnp.float32
BF16 = jnp.bfloat16
HI = lax.Precision.HIGHEST
NT = (((1,), (1,)), ((), ()))
TN = (((0,), (0,)), ((), ()))

D_MODEL = 1024
HEADS = 4
DH = 128
GW = HEADS * DH
D_FF = 2816
CONV_W = 3
EPS = 1e-6
TILE = 128
SUB = 8
FF_CHUNK = 256
VMEM_LIMIT = 56 * 1024 * 1024


def _rms(x, g):
    return x * lax.rsqrt(jnp.mean(x * x, axis=-1, keepdims=True) + EPS) * g


def _rms_mxu(x, g):
    ms = jnp.dot((x * x).astype(BF16), jnp.full((DH, DH), 1.0 / DH, BF16), preferred_element_type=F32)
    return x * lax.rsqrt(ms + EPS) * g


def _bdot(a, b):
    return jnp.dot(a.astype(BF16), b.astype(BF16), preferred_element_type=F32)


def _resident(shape):
    zeros = (0,) * len(shape)
    return pl.BlockSpec(shape, lambda *_: zeros, pipeline_mode=pl.Buffered(1))


def _run(gen):
    try:
        while True:
            next(gen)
    except StopIteration as stop:
        return stop.value


def _interleave(primary, *others):
    gens = [primary, *others]
    done = [False] * len(gens)
    vals = [None] * len(gens)

    def step(j):
        try:
            next(gens[j])
        except StopIteration as stop:
            done[j], vals[j] = True, stop.value

    turn = 0
    while not all(done):
        if not done[0]:
            step(0)
        pending = [j for j in range(1, len(gens)) if not done[j]]
        if pending:
            step(pending[turn % len(pending)])
            turn += 1
    return vals


def _inproj_core(x, g_ref, w_ref, wg_ref, proj_ref, gate_ref):
    hb = _rms(x, g_ref[...]).astype(BF16)
    gate_ref[...] = jnp.dot(hb, wg_ref[...], preferred_element_type=F32)
    for c in range(0, 8 * GW, GW):
        yield
        proj_ref[:, c:c + GW] = jnp.dot(hb, w_ref[:, c:c + GW], preferred_element_type=F32)


def _inproj_kernel(x_ref, g_ref, w_ref, wg_ref, proj_ref, gate_ref):
    _run(_inproj_core(x_ref[...], g_ref, w_ref, wg_ref, proj_ref, gate_ref))


IN_COLS = 8 * GW + 2 * HEADS


def _inproj(x2, g1, w_all, w_gate2, tm):
    n = x2.shape[0]
    return pl.pallas_call(
        _inproj_kernel,
        grid=(n // tm,),
        in_specs=[
            pl.BlockSpec((tm, D_MODEL), lambda i: (i, 0)),
            _resident((1, D_MODEL)),
            _resident((D_MODEL, IN_COLS)),
            _resident((D_MODEL, 2 * DH)),
        ],
        out_specs=[
            pl.BlockSpec((tm, 8 * GW), lambda i: (i, 0)),
            pl.BlockSpec((tm, 2 * DH), lambda i: (i, 0)),
        ],
        out_shape=[
            jax.ShapeDtypeStruct((n, 8 * GW), F32),
            jax.ShapeDtypeStruct((n, 2 * DH), F32),
        ],
        compiler_params=pltpu.CompilerParams(
            dimension_semantics=("arbitrary",), vmem_limit_bytes=VMEM_LIMIT),
        name="inproj",
    )(x2, g1, w_all, w_gate2)


def _seg_last(x, seg):
    nseg = TILE // seg
    w = x.shape[-1]
    if nseg == 1:
        last = jnp.broadcast_to(x[TILE - SUB:TILE][SUB - 1:SUB], (SUB, w))
        return jnp.concatenate([last] * (TILE // SUB), axis=0)
    y = x.reshape(nseg, seg, w)[:, seg - 1:seg, :]
    return jnp.broadcast_to(y, (nseg, seg, w)).reshape(TILE, w)


def _hgrn_intra(q, k, v, b, seg, b_ref, k_ref, v_ref):
    sub = lax.broadcasted_iota(jnp.int32, (SUB, 1), 0)
    blocks = []
    for r0 in range(0, TILE, SUB):
        bb = b[r0:r0 + SUB]
        qb = q[r0:r0 + SUB]
        acc = jnp.zeros((SUB, DH), F32)
        for s in range(SUB):
            bs = jnp.broadcast_to(b_ref[r0 + s:r0 + s + 1], (SUB, DH))
            ks = jnp.broadcast_to(k_ref[r0 + s:r0 + s + 1], (SUB, DH))
            vs = jnp.broadcast_to(v_ref[r0 + s:r0 + s + 1], (SUB, DH))
            p = jnp.exp2(bb - bs) * (qb * ks)
            col = jnp.sum(p, axis=-1, keepdims=True)
            col = jnp.where(sub >= s, col, 0.0)
            acc = acc + col * vs
        blocks.append(acc)
        if r0 % (4 * SUB) == 3 * SUB:
            yield
    intra = jnp.concatenate(blocks, axis=0)

    if seg > SUB:
        ri = lax.broadcasted_iota(jnp.int32, (TILE, TILE), 0)
        ci = lax.broadcasted_iota(jnp.int32, (TILE, TILE), 1)
        a_off = jnp.zeros((TILE, TILE), F32)
        w = SUB
        while w < seg:
            zeros = jnp.zeros((w, DH), F32)
            qs, ks = [], []
            for r0 in range(0, TILE, 2 * w):
                ref = b[r0 + w - 1:r0 + w]
                ks += [k[r0:r0 + w] * jnp.exp2(ref - b[r0:r0 + w]), zeros]
                qs += [zeros, q[r0 + w:r0 + 2 * w] * jnp.exp2(b[r0 + w:r0 + 2 * w] - ref)]
            qt = jnp.concatenate(qs, axis=0).astype(BF16)
            kt = jnp.concatenate(ks, axis=0).astype(BF16)
            a_lvl = lax.dot_general(qt, kt, NT, preferred_element_type=F32)
            same_block = (ri // (2 * w)) == (ci // (2 * w))
            a_off = a_off + jnp.where(same_block, a_lvl, 0.0)
            w *= 2
        yield
        intra = intra + _bdot(a_off, v)
    return intra


def _mixer_core(proj_ref, gate_ref, lbl_ref, hgn_ref, mln_ref, gb_ref,
                s_in, c_in, s_out, c_out, n_all, m_rows, put_n, o_ref, b_sc, k_sc, seg):
    nseg = TILE // seg
    ri = lax.broadcasted_iota(jnp.int32, (TILE, TILE), 0)
    ci = lax.broadcasted_iota(jnp.int32, (TILE, TILE), 1)
    tri = ((ri // seg) == (ci // seg)) & (ci <= ri)
    tri_f = tri.astype(F32)
    lane = lax.broadcasted_iota(jnp.int32, (TILE, DH), 1)

    lg = lbl_ref[...]
    ex = jnp.exp(lg - jnp.max(lg, axis=0, keepdims=True))
    lb_all = ex[0:1] / jnp.sum(ex, axis=0, keepdims=True)

    ig_all = gate_ref[:, 0:DH] + gb_ref[:, 0:DH]
    lf_all = jax.nn.log_sigmoid(gate_ref[:, DH:2 * DH] + gb_ref[:, DH:2 * DH])
    f_all = lb_all + (1.0 - lb_all) * jax.nn.sigmoid(proj_ref[:, GW:2 * GW])
    cums = jnp.dot(tri_f, jnp.concatenate([jnp.log2(f_all), lf_all], axis=1), precision=HI,
                   preferred_element_type=F32)
    a_all = cums[:, GW:GW + DH]
    a_all_t = a_all.T
    ig_all_t = ig_all.T
    m_out = jnp.zeros((TILE, DH), F32)
    yield

    for h in range(HEADS):
        hs = slice(h * DH, (h + 1) * DH)

        q = proj_ref[:, 0 * GW + h * DH:0 * GW + (h + 1) * DH]
        v = proj_ref[:, 2 * GW + h * DH:2 * GW + (h + 1) * DH]
        og = proj_ref[:, 3 * GW + h * DH:3 * GW + (h + 1) * DH]
        k = 1.0 - f_all[:, hs]
        b = cums[:, hs]
        b_last = _seg_last(b, seg)
        qe = (q * jnp.exp2(b)).astype(BF16)
        kd = (k * jnp.exp2(b_last - b)).astype(BF16)
        vb16 = v.astype(BF16)
        e_last = jnp.exp2(b_last)
        b_sc[h] = b
        k_sc[h] = k
        yield
        out = yield from _hgrn_intra(q, k, v, b, seg, b_sc.at[h], k_sc.at[h],
                                     proj_ref.at[:, 2 * GW + h * DH:2 * GW + (h + 1) * DH])
        inter = []
        for sg in range(nseg):
            rows = slice(sg * seg, (sg + 1) * seg)
            st = s_in[sg, h]
            inter.append(jnp.dot(qe[rows], st.astype(BF16), preferred_element_type=F32))
            decay = jnp.broadcast_to(e_last[sg * seg:sg * seg + 1], (DH, DH)).T
            s_out[sg, h] = decay * st + lax.dot_general(kd[rows], vb16[rows], TN, preferred_element_type=F32)
        out = out + jnp.concatenate(inter, axis=0)
        out = _rms_mxu(out, hgn_ref[:, hs]) * (og * jax.nn.sigmoid(og))
        o_ref[:, hs] = out
        yield

        q = proj_ref[:, 4 * GW + h * DH:4 * GW + (h + 1) * DH]
        k = proj_ref[:, 5 * GW + h * DH:5 * GW + (h + 1) * DH] * (DH ** -0.5)
        v = proj_ref[:, 6 * GW + h * DH:6 * GW + (h + 1) * DH]
        og = proj_ref[:, 7 * GW + h * DH:7 * GW + (h + 1) * DH]
        a_col = jnp.broadcast_to(a_all[:, h:h + 1], (TILE, TILE))
        i_col = jnp.broadcast_to(ig_all[:, h:h + 1], (TILE, TILE))
        m_col = jnp.broadcast_to(m_rows[:, h:h + 1], (TILE, TILE))
        a_row = a_all_t[h:h + 1]
        i_row = ig_all_t[h:h + 1]
        log_d = jnp.where(tri, a_col - a_row + i_row, -jnp.inf)
        log_inter = a_col + m_col
        m_t = jnp.maximum(log_inter, jnp.max(log_d, axis=-1, keepdims=True))
        d_w = jnp.exp(log_d - m_t)
        w_i = jnp.exp(log_inter - m_t)
        qb16 = q.astype(BF16)
        kb16 = k.astype(BF16)
        vb16 = v.astype(BF16)
        s_w = lax.dot_general(qb16, kb16, NT, preferred_element_type=F32) * d_w
        nd = jnp.dot(s_w.astype(BF16), jnp.concatenate([vb16, jnp.ones((TILE, DH), BF16)], axis=1),
                     preferred_element_type=F32)
        num, den = nd[:, 0:DH], nd[:, DH:2 * DH]
        yield

        m_new = _seg_last(m_t, seg)
        a_end = _seg_last(a_col, seg)
        w_end = jnp.exp(a_end - a_col + i_col - m_new)
        f_end = jnp.exp(a_end + m_col - m_new)
        kw = k * w_end
        kw16 = kw.astype(BF16)
        qc, qn = [], []
        for sg in range(nseg):
            rows = slice(sg * seg, (sg + 1) * seg)
            last = sg * seg + seg - 1
            ct = c_in[sg, h]
            qc.append(jnp.dot(qb16[rows], ct.astype(BF16), preferred_element_type=F32))
            qn.append(q[rows] * n_all[sg:sg + 1, hs])
            fe = f_end[last:last + 1, 0:1]
            c_out[sg, h] = fe * ct + lax.dot_general(kw16[rows], vb16[rows], TN, preferred_element_type=F32)
            put_n(sg, hs, fe * n_all[sg:sg + 1, hs] + jnp.sum(kw[rows], axis=0, keepdims=True))
        num = w_i * jnp.concatenate(qc, axis=0) + num
        qn = jnp.dot(jnp.concatenate(qn, axis=0).astype(BF16), jnp.ones((DH, DH), BF16), preferred_element_type=F32)
        den = w_i * qn + den
        hout = num / jnp.maximum(jnp.abs(den), jnp.exp(-m_t))
        hout = _rms_mxu(hout, mln_ref[:, hs]) * jax.nn.sigmoid(og)
        o_ref[:, GW + h * DH:GW + (h + 1) * DH] = hout
        m_out = jnp.where(lane == h, m_t, m_out)
        yield
    return m_out


def _mixer_seg_kernel(proj_ref, gate_ref, lbl_ref, hgn_ref, mln_ref, gb_ref, s_ref, c_ref, n_ref, m_ref,
                      o_ref, so_ref, co_ref, no_ref, mo_ref, b_sc, k_sc, *, seg):
    def put_n(sg, hs, val):
        no_ref[sg:sg + 1, hs] = val

    m_out = _run(_mixer_core(proj_ref, gate_ref, lbl_ref, hgn_ref, mln_ref, gb_ref, s_ref, c_ref, so_ref, co_ref,
                             n_ref[...], m_ref[...], put_n, o_ref, b_sc, k_sc, seg))
    for sg in range(TILE // seg):
        last = sg * seg + seg - 1
        mo_ref[sg:sg + 1, :] = m_out[last:last + 1]


def _mixer_seg(proj, gates, lb_logits, hg_norm, ml_norm, gate_bias, s0, c0, n0, m0, batch, seq):
    n = batch * seq
    assert TILE % seq == 0 and seq % SUB == 0 and n % TILE == 0
    nseg = TILE // seq
    row = lambda shape: pl.BlockSpec(shape, lambda i: (i, 0))
    st_spec = pl.BlockSpec((nseg, HEADS, DH, DH), lambda i: (i, 0, 0, 0))
    st_shape = jax.ShapeDtypeStruct((batch, HEADS, DH, DH), F32)
    m_rows = jnp.repeat(jnp.pad(m0, ((0, 0), (0, DH - HEADS))), seq, axis=0)
    o, s_new, c_new, n_new, m_new = pl.pallas_call(
        functools.partial(_mixer_seg_kernel, seg=seq),
        grid=(n // TILE,),
        in_specs=[row((TILE, 8 * GW)), row((TILE, 2 * DH)),
                  _resident((2, GW)), _resident((1, GW)), _resident((1, GW)), _resident((1, 2 * DH)),
                  st_spec, st_spec, row((nseg, GW)), row((TILE, DH))],
        out_specs=[row((TILE, 2 * GW)), st_spec, st_spec, row((nseg, GW)), row((nseg, DH))],
        out_shape=[jax.ShapeDtypeStruct((n, 2 * GW), F32), st_shape, st_shape,
                   jax.ShapeDtypeStruct((batch, GW), F32), jax.ShapeDtypeStruct((batch, DH), F32)],
        scratch_shapes=[pltpu.VMEM((HEADS, TILE, DH), F32), pltpu.VMEM((HEADS, TILE, DH), F32)],
        compiler_params=pltpu.CompilerParams(
            dimension_semantics=("arbitrary",), vmem_limit_bytes=VMEM_LIMIT),
        name="mixer_seg",
    )(proj, gates, lb_logits, hg_norm, ml_norm, gate_bias, s0, c0, n0.reshape(batch, GW), m_rows)
    return o, s_new, c_new, n_new.reshape(batch, HEADS, DH), m_new[:, :HEADS]


SEQS_PER_STEP = 2


def _mixer_carry_kernel(proj_ref, gate_ref, lbl_ref, hgn_ref, mln_ref, gb_ref, s_ref, c_ref, n_ref, m_ref,
                        o_ref, so_ref, co_ref, no_ref, mo_ref, b_sc, k_sc):
    @pl.when(pl.program_id(1) == 0)
    def _():
        so_ref[...] = s_ref[...]
        co_ref[...] = c_ref[...]
        no_ref[...] = n_ref[...]
        mo_ref[...] = m_ref[...]

    tiles = []
    for j in range(SEQS_PER_STEP):
        def put_n(sg, hs, val, j=j):
            no_ref[j, :, hs] = val

        state, cell = so_ref.at[j:j + 1], co_ref.at[j:j + 1]
        tiles.append(_mixer_core(proj_ref.at[j], gate_ref.at[j], lbl_ref, hgn_ref, mln_ref, gb_ref,
                                 state, cell, state, cell, no_ref[j], jnp.broadcast_to(mo_ref[j], (TILE, DH)),
                                 put_n, o_ref.at[j], b_sc.at[j], k_sc.at[j], TILE))
    for j, m_out in enumerate(_interleave(*tiles)):
        mo_ref[j] = m_out[TILE - 1:TILE]


def _mixer_carry(proj, gates, lb_logits, hg_norm, ml_norm, gate_bias, s0, c0, n0, m0, batch, seq):
    assert seq % TILE == 0 and batch % SEQS_PER_STEP == 0
    nb = SEQS_PER_STEP
    tok = lambda w: pl.BlockSpec((nb, TILE, w), lambda b, t: (b, t, 0))
    st_spec = pl.BlockSpec((nb, HEADS, DH, DH), lambda b, t: (b, 0, 0, 0))
    n_spec = pl.BlockSpec((nb, 1, GW), lambda b, t: (b, 0, 0))
    m_spec = pl.BlockSpec((nb, 1, DH), lambda b, t: (b, 0, 0))
    st_shape = jax.ShapeDtypeStruct((batch, HEADS, DH, DH), F32)
    o, s_new, c_new, n_new, m_new = pl.pallas_call(
        _mixer_carry_kernel,
        grid=(batch // nb, seq // TILE),
        in_specs=[tok(8 * GW), tok(2 * DH),
                  _resident((2, GW)), _resident((1, GW)), _resident((1, GW)), _resident((1, 2 * DH)),
                  st_spec, st_spec, n_spec, m_spec],
        out_specs=[tok(2 * GW), st_spec, st_spec, n_spec, m_spec],
        out_shape=[jax.ShapeDtypeStruct((batch, seq, 2 * GW), F32), st_shape, st_shape,
                   jax.ShapeDtypeStruct((batch, 1, GW), F32), jax.ShapeDtypeStruct((batch, 1, DH), F32)],
        scratch_shapes=[pltpu.VMEM((nb, HEADS, TILE, DH), F32), pltpu.VMEM((nb, HEADS, TILE, DH), F32)],
        compiler_params=pltpu.CompilerParams(
            dimension_semantics=("arbitrary", "arbitrary"), vmem_limit_bytes=VMEM_LIMIT),
        name="mixer_carry",
    )(proj.reshape(batch, seq, 8 * GW), gates.reshape(batch, seq, 2 * DH), lb_logits, hg_norm, ml_norm, gate_bias,
      s0, c0, n0.reshape(batch, 1, GW), jnp.pad(m0, ((0, 0), (0, DH - HEADS))).reshape(batch, 1, DH))
    return o, s_new, c_new, n_new.reshape(batch, HEADS, DH), m_new.reshape(batch, DH)[:, :HEADS]


def _ffn_core(x, o, wo_ref, g2_ref, wg_ref, wv_ref, cw_ref, cb_ref, wd_ref, gf_ref, u_sc, g_sc, bb, tt):
    m = bb * tt
    x1 = x + _bdot(o, wo_ref[...])
    h2 = _rms(x1, g2_ref[...]).astype(BF16)
    for c in range(0, D_FF, FF_CHUNK):
        yield
        cs = slice(c, c + FF_CHUNK)
        u = jnp.dot(h2, wg_ref[:, cs], preferred_element_type=F32).reshape(bb, tt, FF_CHUNK)
        val = jnp.dot(h2, wv_ref[:, cs], preferred_element_type=F32).reshape(bb, tt, FF_CHUNK)
        u_sc[:, SUB:SUB + tt, cs] = u
        u1 = u_sc[:, SUB - 1:SUB - 1 + tt, cs]
        u2 = u_sc[:, SUB - 2:SUB - 2 + tt, cs]
        conv = cb_ref[:, cs] + cw_ref[0:1, cs] * u2 + cw_ref[1:2, cs] * u1 + cw_ref[2:3, cs] * u
        g_sc[:, cs] = (jax.nn.gelu(conv) * val).reshape(m, FF_CHUNK).astype(BF16)
    u_sc[:, SUB - 2:SUB, :] = u_sc[:, SUB + tt - 2:SUB + tt, :]
    yield
    y = x1 + jnp.dot(g_sc[...], wd_ref[...], preferred_element_type=F32)
    return _rms(y, gf_ref[...])


def _ffn_kernel(x_ref, o_ref, buf_ref, wo_ref, g2_ref, wg_ref, wv_ref, cw_ref, cb_ref, wd_ref, gf_ref,
                y_ref, nb_ref, u_sc, g_sc, *, bb, tt):
    m = bb * tt

    @pl.when(pl.program_id(1) == 0)
    def _():
        u_sc[:, SUB - 2:SUB, :] = buf_ref[...]

    y = _run(_ffn_core(x_ref[...].reshape(m, D_MODEL), o_ref[...].reshape(m, D_MODEL), wo_ref, g2_ref, wg_ref,
                       wv_ref, cw_ref, cb_ref, wd_ref, gf_ref, u_sc, g_sc, bb, tt))
    nb_ref[...] = u_sc[:, SUB - 2:SUB, :]
    y_ref[...] = y.reshape(bb, tt, D_MODEL)


def _ffn(x, o, buf, wo, g2, wg, wv, cw, cb, wd, gf, bb, tt):
    batch, seq, _ = x.shape
    assert batch % bb == 0 and seq % tt == 0 and tt % SUB == 0 and tt >= CONV_W - 1
    kern = functools.partial(_ffn_kernel, bb=bb, tt=tt)
    tok = pl.BlockSpec((bb, tt, D_MODEL), lambda b, t: (b, t, 0))
    hist = pl.BlockSpec((bb, CONV_W - 1, D_FF), lambda b, t: (b, 0, 0))
    return pl.pallas_call(
        kern,
        grid=(batch // bb, seq // tt),
        in_specs=[tok, tok, hist,
                  _resident((D_MODEL, D_MODEL)), _resident((1, D_MODEL)),
                  _resident((D_MODEL, D_FF)), _resident((D_MODEL, D_FF)),
                  _resident((CONV_W, D_FF)), _resident((1, D_FF)),
                  _resident((D_FF, D_MODEL)), _resident((1, D_MODEL))],
        out_specs=[tok, hist],
        out_shape=[jax.ShapeDtypeStruct((batch, seq, D_MODEL), F32),
                   jax.ShapeDtypeStruct((batch, CONV_W - 1, D_FF), F32)],
        scratch_shapes=[pltpu.VMEM((bb, SUB + tt, D_FF), F32), pltpu.VMEM((bb * tt, D_FF), BF16)],
        compiler_params=pltpu.CompilerParams(
            dimension_semantics=("arbitrary", "arbitrary"), vmem_limit_bytes=VMEM_LIMIT),
        name="ffn",
    )(x, o, buf, wo, g2, wg, wv, cw, cb, wd, gf)


ROWS_PER_STEP = 512


def _layer(x, s0, c0, n0, m0, buf0, p):
    batch, seq, _ = x.shape
    n = batch * seq
    proj, gates = _inproj(x.reshape(n, D_MODEL), p["g1"], p["w_all"], p["w_gate2"], ROWS_PER_STEP)
    mixer = _mixer_carry if seq >= TILE else _mixer_seg
    o, s_new, c_new, n_new, m_new = mixer(proj, gates, p["lb_logits"], p["hg_norm"], p["ml_norm"],
                                          p["gate_bias"], s0, c0, n0, m0, batch, seq)
    tt = min(seq, ROWS_PER_STEP)
    bb = 1 if tt == ROWS_PER_STEP else ROWS_PER_STEP // (2 * tt)
    y, buf_new = _ffn(x, o.reshape(batch, seq, D_MODEL), buf0, p["wo"], p["g2"], p["wg"], p["wv"],
                      p["cw"], p["cb"], p["wd"], p["gf"], bb, tt)
    return y, s_new[None], c_new[None], n_new[None], m_new[None], buf_new[None]


def kernel(x_prompt, x_sample, state_hgrn_S, state_mlstm_C, state_mlstm_n, state_mlstm_m, state_conv, norm1_g, w_in, hg_lb_logits, hg_norm_g, ml_b_ig, ml_b_fg, ml_norm_g, w_out, norm2_g, w_gate, w_val, conv_w, conv_b, w_down, final_norm_g):
    assert norm1_g.shape[0] == 1, "single-layer trunk"
    w = w_in[0]
    gate_cols = w[:, 8 * GW:]
    zpad = jnp.zeros((D_MODEL, DH - HEADS), w.dtype)
    w_gate2 = jnp.concatenate([gate_cols[:, :HEADS], zpad, gate_cols[:, HEADS:], zpad], axis=1)
    bpad = jnp.zeros((DH - HEADS,), F32)
    p = {
        "g1": norm1_g, "w_all": w.astype(BF16), "w_gate2": w_gate2.astype(BF16),
        "lb_logits": hg_lb_logits, "hg_norm": hg_norm_g, "ml_norm": ml_norm_g,
        "gate_bias": jnp.concatenate([ml_b_ig[0], bpad, ml_b_fg[0], bpad])[None],
        "wo": w_out[0].astype(BF16), "g2": norm2_g, "wg": w_gate[0].astype(BF16),
        "wv": w_val[0].astype(BF16), "cw": conv_w[0], "cb": conv_b, "wd": w_down[0].astype(BF16),
        "gf": final_norm_g[None],
    }
    b = x_prompt.shape[0]
    zs = jnp.zeros((b, HEADS, DH, DH), F32)
    prompt = _layer(x_prompt, zs, zs, jnp.zeros((b, HEADS, DH), F32), jnp.zeros((b, HEADS), F32),
                    jnp.zeros((b, CONV_W - 1, D_FF), F32), p)
    sample = _layer(x_sample, state_hgrn_S[0], state_mlstm_C[0], state_mlstm_n[0], state_mlstm_m[0],
                    state_conv[0], p)
    out = []
    for a, c in zip(prompt, sample):
        out += [a, c]
    return tuple(out)
```

```python
import functools

import jax
import jax.numpy as jnp
from jax import lax
from jax.experimental import pallas as pl
from jax.experimental.pallas import tpu as pltpu

F32 = jnp.float32
BF16 = jnp.bfloat16
HI = lax.Precision.HIGHEST
NT = (((1,), (1,)), ((), ()))
TN = (((0,), (0,)), ((), ()))

D_MODEL = 1024
HEADS = 4
DH = 128
GW = HEADS * DH
D_FF = 2816
CONV_W = 3
EPS = 1e-6
TILE = 128
SUB = 8
FF_CHUNK = 256
VMEM_LIMIT = 56 * 1024 * 1024


def _rms(x, g):
    return x * lax.rsqrt(jnp.mean(x * x, axis=-1, keepdims=True) + EPS) * g


def _rms_mxu(x, g):
    ms = jnp.dot((x * x).astype(BF16), jnp.full((DH, DH), 1.0 / DH, BF16), preferred_element_type=F32)
    return x * lax.rsqrt(ms + EPS) * g


def _bdot(a, b):
    return jnp.dot(a.astype(BF16), b.astype(BF16), preferred_element_type=F32)


def _resident(shape):
    zeros = (0,) * len(shape)
    return pl.BlockSpec(shape, lambda *_: zeros, pipeline_mode=pl.Buffered(1))


def _run(gen):
    try:
        while True:
            next(gen)
    except StopIteration as stop:
        return stop.value


def _interleave(primary, *others):
    gens = [primary, *others]
    done = [False] * len(gens)
    vals = [None] * len(gens)

    def step(j):
        try:
            next(gens[j])
        except StopIteration as stop:
            done[j], vals[j] = True, stop.value

    turn = 0
    while not all(done):
        if not done[0]:
            step(0)
        pending = [j for j in range(1, len(gens)) if not done[j]]
        if pending:
            step(pending[turn % len(pending)])
            turn += 1
    return vals


def _inproj_core(x, g_ref, w_ref, wg_ref, proj_ref, gate_ref):
    hb = _rms(x, g_ref[...]).astype(BF16)
    gate_ref[...] = jnp.dot(hb, wg_ref[...], preferred_element_type=F32)
    for c in range(0, 8 * GW, GW):
        yield
        proj_ref[:, c:c + GW] = jnp.dot(hb, w_ref[:, c:c + GW], preferred_element_type=F32)


def _inproj_kernel(x_ref, g_ref, w_ref, wg_ref, proj_ref, gate_ref):
    _run(_inproj_core(x_ref[...], g_ref, w_ref, wg_ref, proj_ref, gate_ref))


IN_COLS = 8 * GW + 2 * HEADS


def _inproj(x2, g1, w_all, w_gate2, tm):
    n = x2.shape[0]
    return pl.pallas_call(
        _inproj_kernel,
        grid=(n // tm,),
        in_specs=[
            pl.BlockSpec((tm, D_MODEL), lambda i: (i, 0)),
            _resident((1, D_MODEL)),
            _resident((D_MODEL, IN_COLS)),
            _resident((D_MODEL, 2 * DH)),
        ],
        out_specs=[
            pl.BlockSpec((tm, 8 * GW), lambda i: (i, 0)),
            pl.BlockSpec((tm, 2 * DH), lambda i: (i, 0)),
        ],
        out_shape=[
            jax.ShapeDtypeStruct((n, 8 * GW), F32),
            jax.ShapeDtypeStruct((n, 2 * DH), F32),
        ],
        compiler_params=pltpu.CompilerParams(
            dimension_semantics=("arbitrary",), vmem_limit_bytes=VMEM_LIMIT),
        name="inproj",
    )(x2, g1, w_all, w_gate2)


def _seg_last(x, seg):
    nseg = TILE // seg
    w = x.shape[-1]
    if nseg == 1:
        last = jnp.broadcast_to(x[TILE - SUB:TILE][SUB - 1:SUB], (SUB, w))
        return jnp.concatenate([last] * (TILE // SUB), axis=0)
    y = x.reshape(nseg, seg, w)[:, seg - 1:seg, :]
    return jnp.broadcast_to(y, (nseg, seg, w)).reshape(TILE, w)


def _hgrn_intra(q, k, v, b, seg, b_ref, k_ref, v_ref):
    sub = lax.broadcasted_iota(jnp.int32, (SUB, 1), 0)
    blocks = []
    for r0 in range(0, TILE, SUB):
        bb = b[r0:r0 + SUB]
        qb = q[r0:r0 + SUB]
        acc = jnp.zeros((SUB, DH), F32)
        for s in range(SUB):
            bs = jnp.broadcast_to(b_ref[r0 + s:r0 + s + 1], (SUB, DH))
            ks = jnp.broadcast_to(k_ref[r0 + s:r0 + s + 1], (SUB, DH))
            vs = jnp.broadcast_to(v_ref[r0 + s:r0 + s + 1], (SUB, DH))
            p = jnp.exp2(bb - bs) * (qb * ks)
            col = jnp.sum(p, axis=-1, keepdims=True)
            col = jnp.where(sub >= s, col, 0.0)
            acc = acc + col * vs
        blocks.append(acc)
        if r0 % (4 * SUB) == 3 * SUB:
            yield
    intra = jnp.concatenate(blocks, axis=0)

    if seg > SUB:
        ri = lax.broadcasted_iota(jnp.int32, (TILE, TILE), 0)
        ci = lax.broadcasted_iota(jnp.int32, (TILE, TILE), 1)
        a_off = jnp.zeros((TILE, TILE), F32)
        w = SUB
        while w < seg:
            zeros = jnp.zeros((w, DH), F32)
            qs, ks = [], []
            for r0 in range(0, TILE, 2 * w):
                ref = b[r0 + w - 1:r0 + w]
                ks += [k[r0:r0 + w] * jnp.exp2(ref - b[r0:r0 + w]), zeros]
                qs += [zeros, q[r0 + w:r0 + 2 * w] * jnp.exp2(b[r0 + w:r0 + 2 * w] - ref)]
            qt = jnp.concatenate(qs, axis=0).astype(BF16)
            kt = jnp.concatenate(ks, axis=0).astype(BF16)
            a_lvl = lax.dot_general(qt, kt, NT, preferred_element_type=F32)
            same_block = (ri // (2 * w)) == (ci // (2 * w))
            a_off = a_off + jnp.where(same_block, a_lvl, 0.0)
            w *= 2
        yield
        intra = intra + _bdot(a_off, v)
    return intra


def _mixer_core(proj_ref, gate_ref, lbl_ref, hgn_ref, mln_ref, gb_ref,
                s_in, c_in, s_out, c_out, n_all, m_rows, put_n, o_ref, b_sc, k_sc, seg):
    nseg = TILE // seg
    ri = lax.broadcasted_iota(jnp.int32, (TILE, TILE), 0)
    ci = lax.broadcasted_iota(jnp.int32, (TILE, TILE), 1)
    tri = ((ri // seg) == (ci // seg)) & (ci <= ri)
    tri_f = tri.astype(F32)
    lane = lax.broadcasted_iota(jnp.int32, (TILE, DH), 1)

    lg = lbl_ref[...]
    ex = jnp.exp(lg - jnp.max(lg, axis=0, keepdims=True))
    lb_all = ex[0:1] / jnp.sum(ex, axis=0, keepdims=True)

    ig_all = gate_ref[:, 0:DH] + gb_ref[:, 0:DH]
    lf_all = jax.nn.log_sigmoid(gate_ref[:, DH:2 * DH] + gb_ref[:, DH:2 * DH])
    f_all = lb_all + (1.0 - lb_all) * jax.nn.sigmoid(proj_ref[:, GW:2 * GW])
    cums = jnp.dot(tri_f, jnp.concatenate([jnp.log2(f_all), lf_all], axis=1), precision=HI,
                   preferred_element_type=F32)
    a_all = cums[:, GW:GW + DH]
    a_all_t = a_all.T
    ig_all_t = ig_all.T
    m_out = jnp.zeros((TILE, DH), F32)
    yield

    for h in range(HEADS):
        hs = slice(h * DH, (h + 1) * DH)

        q = proj_ref[:, 0 * GW + h * DH:0 * GW + (h + 1) * DH]
        v = proj_ref[:, 2 * GW + h * DH:2 * GW + (h + 1) * DH]
        og = proj_ref[:, 3 * GW + h * DH:3 * GW + (h + 1) * DH]
        k = 1.0 - f_all[:, hs]
        b = cums[:, hs]
        b_last = _seg_last(b, seg)
        qe = (q * jnp.exp2(b)).astype(BF16)
        kd = (k * jnp.exp2(b_last - b)).astype(BF16)
        vb16 = v.astype(BF16)
        e_last = jnp.exp2(b_last)
        b_sc[h] = b
        k_sc[h] = k
        yield
        out = yield from _hgrn_intra(q, k, v, b, seg, b_sc.at[h], k_sc.at[h],
                                     proj_ref.at[:, 2 * GW + h * DH:2 * GW + (h + 1) * DH])
        inter = []
        for sg in range(nseg):
            rows = slice(sg * seg, (sg + 1) * seg)
            st = s_in[sg, h]
            inter.append(jnp.dot(qe[rows], st.astype(BF16), preferred_element_type=F32))
            decay = jnp.broadcast_to(e_last[sg * seg:sg * seg + 1], (DH, DH)).T
            s_out[sg, h] = decay * st + lax.dot_general(kd[rows], vb16[rows], TN, preferred_element_type=F32)
        out = out + jnp.concatenate(inter, axis=0)
        out = _rms_mxu(out, hgn_ref[:, hs]) * (og * jax.nn.sigmoid(og))
        o_ref[:, hs] = out
        yield

        q = proj_ref[:, 4 * GW + h * DH:4 * GW + (h + 1) * DH]
        k = proj_ref[:, 5 * GW + h * DH:5 * GW + (h + 1) * DH] * (DH ** -0.5)
        v = proj_ref[:, 6 * GW + h * DH:6 * GW + (h + 1) * DH]
        og = proj_ref[:, 7 * GW + h * DH:7 * GW + (h + 1) * DH]
        a_col = jnp.broadcast_to(a_all[:, h:h + 1], (TILE, TILE))
        i_col = jnp.broadcast_to(ig_all[:, h:h + 1], (TILE, TILE))
        m_col = jnp.broadcast_to(m_rows[:, h:h + 1], (TILE, TILE))
        a_row = a_all_t[h:h + 1]
        i_row = ig_all_t[h:h + 1]
        log_d = jnp.where(tri, a_col - a_row + i_row, -jnp.inf)
        log_inter = a_col + m_col
        m_t = jnp.maximum(log_inter, jnp.max(log_d, axis=-1, keepdims=True))
        d_w = jnp.exp(log_d - m_t)
        w_i = jnp.exp(log_inter - m_t)
        qb16 = q.astype(BF16)
        kb16 = k.astype(BF16)
        vb16 = v.astype(BF16)
        s_w = lax.dot_general(qb16, kb16, NT, preferred_element_type=F32) * d_w
        nd = jnp.dot(s_w.astype(BF16), jnp.concatenate([vb16, jnp.ones((TILE, DH), BF16)], axis=1),
                     preferred_element_type=F32)
        num, den = nd[:, 0:DH], nd[:, DH:2 * DH]
        yield

        m_new = _seg_last(m_t, seg)
        a_end = _seg_last(a_col, seg)
        w_end = jnp.exp(a_end - a_col + i_col - m_new)
        f_end = jnp.exp(a_end + m_col - m_new)
        kw = k * w_end
        kw16 = kw.astype(BF16)
        qc, qn = [], []
        for sg in range(nseg):
            rows = slice(sg * seg, (sg + 1) * seg)
            last = sg * seg + seg - 1
            ct = c_in[sg, h]
            qc.append(jnp.dot(qb16[rows], ct.astype(BF16), preferred_element_type=F32))
            qn.append(q[rows] * n_all[sg:sg + 1, hs])
            fe = f_end[last:last + 1, 0:1]
            c_out[sg, h] = fe * ct + lax.dot_general(kw16[rows], vb16[rows], TN, preferred_element_type=F32)
            put_n(sg, hs, fe * n_all[sg:sg + 1, hs] + jnp.sum(kw[rows], axis=0, keepdims=True))
        num = w_i * jnp.concatenate(qc, axis=0) + num
        qn = jnp.dot(jnp.concatenate(qn, axis=0).astype(BF16), jnp.ones((DH, DH), BF16), preferred_element_type=F32)
        den = w_i * qn + den
        hout = num / jnp.maximum(jnp.abs(den), jnp.exp(-m_t))
        hout = _rms_mxu(hout, mln_ref[:, hs]) * jax.nn.sigmoid(og)
        o_ref[:, GW + h * DH:GW + (h + 1) * DH] = hout
        m_out = jnp.where(lane == h, m_t, m_out)
        yield
    return m_out


def _mixer_seg_kernel(proj_ref, gate_ref, lbl_ref, hgn_ref, mln_ref, gb_ref, s_ref, c_ref, n_ref, m_ref,
                      o_ref, so_ref, co_ref, no_ref, mo_ref, b_sc, k_sc, *, seg):
    def put_n(sg, hs, val):
        no_ref[sg:sg + 1, hs] = val

    m_out = _run(_mixer_core(proj_ref, gate_ref, lbl_ref, hgn_ref, mln_ref, gb_ref, s_ref, c_ref, so_ref, co_ref,
                             n_ref[...], m_ref[...], put_n, o_ref, b_sc, k_sc, seg))
    for sg in range(TILE // seg):
        last = sg * seg + seg - 1
        mo_ref[sg:sg + 1, :] = m_out[last:last + 1]


def _mixer_seg(proj, gates, lb_logits, hg_norm, ml_norm, gate_bias, s0, c0, n0, m0, batch, seq):
    n = batch * seq
    assert TILE % seq == 0 and seq % SUB == 0 and n % TILE == 0
    nseg = TILE // seq
    row = lambda shape: pl.BlockSpec(shape, lambda i: (i, 0))
    st_spec = pl.BlockSpec((nseg, HEADS, DH, DH), lambda i: (i, 0, 0, 0))
    st_shape = jax.ShapeDtypeStruct((batch, HEADS, DH, DH), F32)
    m_rows = jnp.repeat(jnp.pad(m0, ((0, 0), (0, DH - HEADS))), seq, axis=0)
    o, s_new, c_new, n_new, m_new = pl.pallas_call(
        functools.partial(_mixer_seg_kernel, seg=seq),
        grid=(n // TILE,),
        in_specs=[row((TILE, 8 * GW)), row((TILE, 2 * DH)),
                  _resident((2, GW)), _resident((1, GW)), _resident((1, GW)), _resident((1, 2 * DH)),
                  st_spec, st_spec, row((nseg, GW)), row((TILE, DH))],
        out_specs=[row((TILE, 2 * GW)), st_spec, st_spec, row((nseg, GW)), row((nseg, DH))],
        out_shape=[jax.ShapeDtypeStruct((n, 2 * GW), F32), st_shape, st_shape,
                   jax.ShapeDtypeStruct((batch, GW), F32), jax.ShapeDtypeStruct((batch, DH), F32)],
        scratch_shapes=[pltpu.VMEM((HEADS, TILE, DH), F32), pltpu.VMEM((HEADS, TILE, DH), F32)],
        compiler_params=pltpu.CompilerParams(
            dimension_semantics=("arbitrary",), vmem_limit_bytes=VMEM_LIMIT),
        name="mixer_seg",
    )(proj, gates, lb_logits, hg_norm, ml_norm, gate_bias, s0, c0, n0.reshape(batch, GW), m_rows)
    return o, s_new, c_new, n_new.reshape(batch, HEADS, DH), m_new[:, :HEADS]


SEQS_PER_STEP = 4


def _mixer_carry_kernel(proj_ref, gate_ref, lbl_ref, hgn_ref, mln_ref, gb_ref, s_ref, c_ref, n_ref, m_ref,
                        o_ref, so_ref, co_ref, no_ref, mo_ref, b_sc, k_sc):
    @pl.when(pl.program_id(1) == 0)
    def _():
        so_ref[...] = s_ref[...]
        co_ref[...] = c_ref[...]
        no_ref[...] = n_ref[...]
        mo_ref[...] = m_ref[...]

    tiles = []
    for j in range(SEQS_PER_STEP):
        def put_n(sg, hs, val, j=j):
            no_ref[j, :, hs] = val

        state, cell = so_ref.at[j:j + 1], co_ref.at[j:j + 1]
        tiles.append(_mixer_core(proj_ref.at[j], gate_ref.at[j], lbl_ref, hgn_ref, mln_ref, gb_ref,
                                 state, cell, state, cell, no_ref[j], jnp.broadcast_to(mo_ref[j], (TILE, DH)),
                                 put_n, o_ref.at[j], b_sc.at[j], k_sc.at[j], TILE))
    for j, m_out in enumerate(_interleave(*tiles)):
        mo_ref[j] = m_out[TILE - 1:TILE]


def _mixer_carry(proj, gates, lb_logits, hg_norm, ml_norm, gate_bias, s0, c0, n0, m0, batch, seq):
    assert seq % TILE == 0 and batch % SEQS_PER_STEP == 0
    nb = SEQS_PER_STEP
    tok = lambda w: pl.BlockSpec((nb, TILE, w), lambda b, t: (b, t, 0))
    st_spec = pl.BlockSpec((nb, HEADS, DH, DH), lambda b, t: (b, 0, 0, 0))
    n_spec = pl.BlockSpec((nb, 1, GW), lambda b, t: (b, 0, 0))
    m_spec = pl.BlockSpec((nb, 1, DH), lambda b, t: (b, 0, 0))
    st_shape = jax.ShapeDtypeStruct((batch, HEADS, DH, DH), F32)
    o, s_new, c_new, n_new, m_new = pl.pallas_call(
        _mixer_carry_kernel,
        grid=(batch // nb, seq // TILE),
        in_specs=[tok(8 * GW), tok(2 * DH),
                  _resident((2, GW)), _resident((1, GW)), _resident((1, GW)), _resident((1, 2 * DH)),
                  st_spec, st_spec, n_spec, m_spec],
        out_specs=[tok(2 * GW), st_spec, st_spec, n_spec, m_spec],
        out_shape=[jax.ShapeDtypeStruct((batch, seq, 2 * GW), F32), st_shape, st_shape,
                   jax.ShapeDtypeStruct((batch, 1, GW), F32), jax.ShapeDtypeStruct((batch, 1, DH), F32)],
        scratch_shapes=[pltpu.VMEM((nb, HEADS, TILE, DH), F32), pltpu.VMEM((nb, HEADS, TILE, DH), F32)],
        compiler_params=pltpu.CompilerParams(
            dimension_semantics=("arbitrary", "arbitrary"), vmem_limit_bytes=VMEM_LIMIT),
        name="mixer_carry",
    )(proj.reshape(batch, seq, 8 * GW), gates.reshape(batch, seq, 2 * DH), lb_logits, hg_norm, ml_norm, gate_bias,
      s0, c0, n0.reshape(batch, 1, GW), jnp.pad(m0, ((0, 0), (0, DH - HEADS))).reshape(batch, 1, DH))
    return o, s_new, c_new, n_new.reshape(batch, HEADS, DH), m_new.reshape(batch, DH)[:, :HEADS]


def _ffn_core(x, o, wo_ref, g2_ref, wg_ref, wv_ref, cw_ref, cb_ref, wd_ref, gf_ref, u_sc, g_sc, bb, tt):
    m = bb * tt
    x1 = x + _bdot(o, wo_ref[...])
    h2 = _rms(x1, g2_ref[...]).astype(BF16)
    for c in range(0, D_FF, FF_CHUNK):
        yield
        cs = slice(c, c + FF_CHUNK)
        u = jnp.dot(h2, wg_ref[:, cs], preferred_element_type=F32).reshape(bb, tt, FF_CHUNK)
        val = jnp.dot(h2, wv_ref[:, cs], preferred_element_type=F32).reshape(bb, tt, FF_CHUNK)
        u_sc[:, SUB:SUB + tt, cs] = u
        u1 = u_sc[:, SUB - 1:SUB - 1 + tt, cs]
        u2 = u_sc[:, SUB - 2:SUB - 2 + tt, cs]
        conv = cb_ref[:, cs] + cw_ref[0:1, cs] * u2 + cw_ref[1:2, cs] * u1 + cw_ref[2:3, cs] * u
        g_sc[:, cs] = (jax.nn.gelu(conv) * val).reshape(m, FF_CHUNK).astype(BF16)
    u_sc[:, SUB - 2:SUB, :] = u_sc[:, SUB + tt - 2:SUB + tt, :]
    yield
    y = x1 + jnp.dot(g_sc[...], wd_ref[...], preferred_element_type=F32)
    return _rms(y, gf_ref[...])


def _ffn_kernel(x_ref, o_ref, buf_ref, wo_ref, g2_ref, wg_ref, wv_ref, cw_ref, cb_ref, wd_ref, gf_ref,
                y_ref, nb_ref, u_sc, g_sc, *, bb, tt):
    m = bb * tt

    @pl.when(pl.program_id(1) == 0)
    def _():
        u_sc[:, SUB - 2:SUB, :] = buf_ref[...]

    y = _run(_ffn_core(x_ref[...].reshape(m, D_MODEL), o_ref[...].reshape(m, D_MODEL), wo_ref, g2_ref, wg_ref,
                       wv_ref, cw_ref, cb_ref, wd_ref, gf_ref, u_sc, g_sc, bb, tt))
    nb_ref[...] = u_sc[:, SUB - 2:SUB, :]
    y_ref[...] = y.reshape(bb, tt, D_MODEL)


def _ffn(x, o, buf, wo, g2, wg, wv, cw, cb, wd, gf, bb, tt):
    batch, seq, _ = x.shape
    assert batch % bb == 0 and seq % tt == 0 and tt % SUB == 0 and tt >= CONV_W - 1
    kern = functools.partial(_ffn_kernel, bb=bb, tt=tt)
    tok = pl.BlockSpec((bb, tt, D_MODEL), lambda b, t: (b, t, 0))
    hist = pl.BlockSpec((bb, CONV_W - 1, D_FF), lambda b, t: (b, 0, 0))
    return pl.pallas_call(
        kern,
        grid=(batch // bb, seq // tt),
        in_specs=[tok, tok, hist,
                  _resident((D_MODEL, D_MODEL)), _resident((1, D_MODEL)),
                  _resident((D_MODEL, D_FF)), _resident((D_MODEL, D_FF)),
                  _resident((CONV_W, D_FF)), _resident((1, D_FF)),
                  _resident((D_FF, D_MODEL)), _resident((1, D_MODEL))],
        out_specs=[tok, hist],
        out_shape=[jax.ShapeDtypeStruct((batch, seq, D_MODEL), F32),
                   jax.ShapeDtypeStruct((batch, CONV_W - 1, D_FF), F32)],
        scratch_shapes=[pltpu.VMEM((bb, SUB + tt, D_FF), F32), pltpu.VMEM((bb * tt, D_FF), BF16)],
        compiler_params=pltpu.CompilerParams(
            dimension_semantics=("arbitrary", "arbitrary"), vmem_limit_bytes=VMEM_LIMIT),
        name="ffn",
    )(x, o, buf, wo, g2, wg, wv, cw, cb, wd, gf)


ROWS_PER_STEP = 512


def _layer(x, s0, c0, n0, m0, buf0, p):
    batch, seq, _ = x.shape
    n = batch * seq
    proj, gates = _inproj(x.reshape(n, D_MODEL), p["g1"], p["w_all"], p["w_gate2"], ROWS_PER_STEP)
    mixer = _mixer_carry if seq >= TILE else _mixer_seg
    o, s_new, c_new, n_new, m_new = mixer(proj, gates, p["lb_logits"], p["hg_norm"], p["ml_norm"],
                                          p["gate_bias"], s0, c0, n0, m0, batch, seq)
    tt = min(seq, ROWS_PER_STEP)
    bb = 1 if tt == ROWS_PER_STEP else ROWS_PER_STEP // (2 * tt)
    y, buf_new = _ffn(x, o.reshape(batch, seq, D_MODEL), buf0, p["wo"], p["g2"], p["wg"], p["wv"],
                      p["cw"], p["cb"], p["wd"], p["gf"], bb, tt)
    return y, s_new[None], c_new[None], n_new[None], m_new[None], buf_new[None]


def kernel(x_prompt, x_sample, state_hgrn_S, state_mlstm_C, state_mlstm_n, state_mlstm_m, state_conv, norm1_g, w_in, hg_lb_logits, hg_norm_g, ml_b_ig, ml_b_fg, ml_norm_g, w_out, norm2_g, w_gate, w_val, conv_w, conv_b, w_down, final_norm_g):
    assert norm1_g.shape[0] == 1, "single-layer trunk"
    w = w_in[0]
    gate_cols = w[:, 8 * GW:]
    zpad = jnp.zeros((D_MODEL, DH - HEADS), w.dtype)
    w_gate2 = jnp.concatenate([gate_cols[:, :HEADS], zpad, gate_cols[:, HEADS:], zpad], axis=1)
    bpad = jnp.zeros((DH - HEADS,), F32)
    p = {
        "g1": norm1_g, "w_all": w.astype(BF16), "w_gate2": w_gate2.astype(BF16),
        "lb_logits": hg_lb_logits, "hg_norm": hg_norm_g, "ml_norm": ml_norm_g,
        "gate_bias": jnp.concatenate([ml_b_ig[0], bpad, ml_b_fg[0], bpad])[None],
        "wo": w_out[0].astype(BF16), "g2": norm2_g, "wg": w_gate[0].astype(BF16),
        "wv": w_val[0].astype(BF16), "cw": conv_w[0], "cb": conv_b, "wd": w_down[0].astype(BF16),
        "gf": final_norm_g[None],
    }
    b = x_prompt.shape[0]
    zs = jnp.zeros((b, HEADS, DH, DH), F32)
    prompt = _layer(x_prompt, zs, zs, jnp.zeros((b, HEADS, DH), F32), jnp.zeros((b, HEADS), F32),
                    jnp.zeros((b, CONV_W - 1, D_FF), F32), p)
    sample = _layer(x_sample, state_hgrn_S[0], state_mlstm_C[0], state_mlstm_n[0], state_mlstm_m[0],
                    state_conv[0], p)
    out = []
    for a, c in zip(prompt, sample):
        out += [a, c]
    return tuple(out)
```

```python
import functools

import jax
import jax.numpy as jnp
from jax import lax
from jax.experimental import pallas as pl
from jax.experimental.pallas import tpu as pltpu

F32 = jnp.float32
BF16 = jnp.bfloat16
HI = lax.Precision.HIGHEST
NT = (((1,), (1,)), ((), ()))
TN = (((0,), (0,)), ((), ()))

D_MODEL = 1024
HEADS = 4
DH = 128
GW = HEADS * DH
D_FF = 2816
CONV_W = 3
EPS = 1e-6
TILE = 128
SUB = 8
FF_CHUNK = 256
VMEM_LIMIT = 56 * 1024 * 1024


def _rms(x, g):
    return x * lax.rsqrt(jnp.mean(x * x, axis=-1, keepdims=True) + EPS) * g


def _rms_mxu(x, g):
    ms = jnp.dot((x * x).astype(BF16), jnp.full((DH, DH), 1.0 / DH, BF16), preferred_element_type=F32)
    return x * lax.rsqrt(ms + EPS) * g


def _bdot(a, b):
    return jnp.dot(a.astype(BF16), b.astype(BF16), preferred_element_type=F32)


def _resident(shape):
    zeros = (0,) * len(shape)
    return pl.BlockSpec(shape, lambda *_: zeros, pipeline_mode=pl.Buffered(1))


def _run(gen):
    try:
        while True:
            next(gen)
    except StopIteration as stop:
        return stop.value


def _interleave(primary, *others):
    gens = [primary, *others]
    done = [False] * len(gens)
    vals = [None] * len(gens)

    def step(j):
        try:
            next(gens[j])
        except StopIteration as stop:
            done[j], vals[j] = True, stop.value

    turn = 0
    while not all(done):
        if not done[0]:
            step(0)
        pending = [j for j in range(1, len(gens)) if not done[j]]
        if pending:
            step(pending[turn % len(pending)])
            turn += 1
    return vals


def _inproj_core(x, g_ref, w_ref, wg_ref, proj_ref, gate_ref):
    hb = _rms(x, g_ref[...]).astype(BF16)
    gate_ref[...] = jnp.dot(hb, wg_ref[...], preferred_element_type=F32)
    for c in range(0, 8 * GW, GW):
        yield
        proj_ref[:, c:c + GW] = jnp.dot(hb, w_ref[:, c:c + GW], preferred_element_type=F32)


def _inproj_kernel(x_ref, g_ref, w_ref, wg_ref, proj_ref, gate_ref):
    _run(_inproj_core(x_ref[...], g_ref, w_ref, wg_ref, proj_ref, gate_ref))


IN_COLS = 8 * GW + 2 * HEADS


def _inproj(x2, g1, w_all, w_gate2, tm):
    n = x2.shape[0]
    return pl.pallas_call(
        _inproj_kernel,
        grid=(n // tm,),
        in_specs=[
            pl.BlockSpec((tm, D_MODEL), lambda i: (i, 0)),
            _resident((1, D_MODEL)),
            _resident((D_MODEL, IN_COLS)),
            _resident((D_MODEL, 2 * DH)),
        ],
        out_specs=[
            pl.BlockSpec((tm, 8 * GW), lambda i: (i, 0)),
            pl.BlockSpec((tm, 2 * DH), lambda i: (i, 0)),
        ],
        out_shape=[
            jax.ShapeDtypeStruct((n, 8 * GW), F32),
            jax.ShapeDtypeStruct((n, 2 * DH), F32),
        ],
        compiler_params=pltpu.CompilerParams(
            dimension_semantics=("arbitrary",), vmem_limit_bytes=VMEM_LIMIT),
        name="inproj",
    )(x2, g1, w_all, w_gate2)


def _seg_last(x, seg):
    nseg = TILE // seg
    w = x.shape[-1]
    if nseg == 1:
        last = jnp.broadcast_to(x[TILE - SUB:TILE][SUB - 1:SUB], (SUB, w))
        return jnp.concatenate([last] * (TILE // SUB), axis=0)
    y = x.reshape(nseg, seg, w)[:, seg - 1:seg, :]
    return jnp.broadcast_to(y, (nseg, seg, w)).reshape(TILE, w)


def _hgrn_intra(q, k, v, b, seg, b_ref, k_ref, v_ref):
    sub = lax.broadcasted_iota(jnp.int32, (SUB, 1), 0)
    blocks = []
    for r0 in range(0, TILE, SUB):
        bb = b[r0:r0 + SUB]
        qb = q[r0:r0 + SUB]
        acc = jnp.zeros((SUB, DH), F32)
        for s in range(SUB):
            bs = jnp.broadcast_to(b_ref[r0 + s:r0 + s + 1], (SUB, DH))
            ks = jnp.broadcast_to(k_ref[r0 + s:r0 + s + 1], (SUB, DH))
            vs = jnp.broadcast_to(v_ref[r0 + s:r0 + s + 1], (SUB, DH))
            p = jnp.exp2(bb - bs) * (qb * ks)
            col = jnp.sum(p, axis=-1, keepdims=True)
            col = jnp.where(sub >= s, col, 0.0)
            acc = acc + col * vs
        blocks.append(acc)
        if r0 % (4 * SUB) == 3 * SUB:
            yield
    intra = jnp.concatenate(blocks, axis=0)

    if seg > SUB:
        ri = lax.broadcasted_iota(jnp.int32, (TILE, TILE), 0)
        ci = lax.broadcasted_iota(jnp.int32, (TILE, TILE), 1)
        a_off = jnp.zeros((TILE, TILE), F32)
        w = SUB
        while w < seg:
            zeros = jnp.zeros((w, DH), F32)
            qs, ks = [], []
            for r0 in range(0, TILE, 2 * w):
                ref = b[r0 + w - 1:r0 + w]
                ks += [k[r0:r0 + w] * jnp.exp2(ref - b[r0:r0 + w]), zeros]
                qs += [zeros, q[r0 + w:r0 + 2 * w] * jnp.exp2(b[r0 + w:r0 + 2 * w] - ref)]
            qt = jnp.concatenate(qs, axis=0).astype(BF16)
            kt = jnp.concatenate(ks, axis=0).astype(BF16)
            a_lvl = lax.dot_general(qt, kt, NT, preferred_element_type=F32)
            same_block = (ri // (2 * w)) == (ci // (2 * w))
            a_off = a_off + jnp.where(same_block, a_lvl, 0.0)
            w *= 2
        yield
        intra = intra + _bdot(a_off, v)
    return intra


def _mixer_core(proj_ref, gate_ref, lbl_ref, hgn_ref, mln_ref, gb_ref,
                s_in, c_in, s_out, c_out, n_all, m_rows, put_n, o_ref, b_sc, k_sc, seg):
    nseg = TILE // seg
    ri = lax.broadcasted_iota(jnp.int32, (TILE, TILE), 0)
    ci = lax.broadcasted_iota(jnp.int32, (TILE, TILE), 1)
    tri = ((ri // seg) == (ci // seg)) & (ci <= ri)
    tri_f = tri.astype(F32)
    lane = lax.broadcasted_iota(jnp.int32, (TILE, DH), 1)

    lg = lbl_ref[...]
    ex = jnp.exp(lg - jnp.max(lg, axis=0, keepdims=True))
    lb_all = ex[0:1] / jnp.sum(ex, axis=0, keepdims=True)

    ig_all = gate_ref[:, 0:DH] + gb_ref[:, 0:DH]
    lf_all = jax.nn.log_sigmoid(gate_ref[:, DH:2 * DH] + gb_ref[:, DH:2 * DH])
    f_all = lb_all + (1.0 - lb_all) * jax.nn.sigmoid(proj_ref[:, GW:2 * GW])
    cums = jnp.dot(tri_f, jnp.concatenate([jnp.log2(f_all), lf_all], axis=1), precision=HI,
                   preferred_element_type=F32)
    a_all = cums[:, GW:GW + DH]
    a_all_t = a_all.T
    ig_all_t = ig_all.T
    m_out = jnp.zeros((TILE, DH), F32)
    yield

    for h in range(HEADS):
        hs = slice(h * DH, (h + 1) * DH)

        q = proj_ref[:, 0 * GW + h * DH:0 * GW + (h + 1) * DH]
        v = proj_ref[:, 2 * GW + h * DH:2 * GW + (h + 1) * DH]
        og = proj_ref[:, 3 * GW + h * DH:3 * GW + (h + 1) * DH]
        k = 1.0 - f_all[:, hs]
        b = cums[:, hs]
        b_last = _seg_last(b, seg)
        qe = (q * jnp.exp2(b)).astype(BF16)
        kd = (k * jnp.exp2(b_last - b)).astype(BF16)
        vb16 = v.astype(BF16)
        e_last = jnp.exp2(b_last)
        b_sc[h] = b
        k_sc[h] = k
        yield
        out = yield from _hgrn_intra(q, k, v, b, seg, b_sc.at[h], k_sc.at[h],
                                     proj_ref.at[:, 2 * GW + h * DH:2 * GW + (h + 1) * DH])
        inter = []
        for sg in range(nseg):
            rows = slice(sg * seg, (sg + 1) * seg)
            st = s_in[sg, h]
            inter.append(jnp.dot(qe[rows], st.astype(BF16), preferred_element_type=F32))
            decay = jnp.broadcast_to(e_last[sg * seg:sg * seg + 1], (DH, DH)).T
            s_out[sg, h] = decay * st + lax.dot_general(kd[rows], vb16[rows], TN, preferred_element_type=F32)
        out = out + jnp.concatenate(inter, axis=0)
        out = _rms_mxu(out, hgn_ref[:, hs]) * (og * jax.nn.sigmoid(og))
        o_ref[:, hs] = out
        yield

        q = proj_ref[:, 4 * GW + h * DH:4 * GW + (h + 1) * DH]
        k = proj_ref[:, 5 * GW + h * DH:5 * GW + (h + 1) * DH] * (DH ** -0.5)
        v = proj_ref[:, 6 * GW + h * DH:6 * GW + (h + 1) * DH]
        og = proj_ref[:, 7 * GW + h * DH:7 * GW + (h + 1) * DH]
        a_col = jnp.broadcast_to(a_all[:, h:h + 1], (TILE, TILE))
        i_col = jnp.broadcast_to(ig_all[:, h:h + 1], (TILE, TILE))
        m_col = jnp.broadcast_to(m_rows[:, h:h + 1], (TILE, TILE))
        a_row = a_all_t[h:h + 1]
        i_row = ig_all_t[h:h + 1]
        log_d = jnp.where(tri, a_col - a_row + i_row, -jnp.inf)
        log_inter = a_col + m_col
        m_t = jnp.maximum(log_inter, jnp.max(log_d, axis=-1, keepdims=True))
        d_w = jnp.exp(log_d - m_t)
        w_i = jnp.exp(log_inter - m_t)
        qb16 = q.astype(BF16)
        kb16 = k.astype(BF16)
        vb16 = v.astype(BF16)
        s_w = lax.dot_general(qb16, kb16, NT, preferred_element_type=F32) * d_w
        nd = jnp.dot(s_w.astype(BF16), jnp.concatenate([vb16, jnp.ones((TILE, DH), BF16)], axis=1),
                     preferred_element_type=F32)
        num, den = nd[:, 0:DH], nd[:, DH:2 * DH]
        yield

        m_new = _seg_last(m_t, seg)
        a_end = _seg_last(a_col, seg)
        w_end = jnp.exp(a_end - a_col + i_col - m_new)
        f_end = jnp.exp(a_end + m_col - m_new)
        kw = k * w_end
        kw16 = kw.astype(BF16)
        qc, qn = [], []
        for sg in range(nseg):
            rows = slice(sg * seg, (sg + 1) * seg)
            last = sg * seg + seg - 1
            ct = c_in[sg, h]
            qc.append(jnp.dot(qb16[rows], ct.astype(BF16), preferred_element_type=F32))
            qn.append(q[rows] * n_all[sg:sg + 1, hs])
            fe = f_end[last:last + 1, 0:1]
            c_out[sg, h] = fe * ct + lax.dot_general(kw16[rows], vb16[rows], TN, preferred_element_type=F32)
            put_n(sg, hs, fe * n_all[sg:sg + 1, hs] + jnp.sum(kw[rows], axis=0, keepdims=True))
        num = w_i * jnp.concatenate(qc, axis=0) + num
        qn = jnp.dot(jnp.concatenate(qn, axis=0).astype(BF16), jnp.ones((DH, DH), BF16), preferred_element_type=F32)
        den = w_i * qn + den
        hout = num / jnp.maximum(jnp.abs(den), jnp.exp(-m_t))
        hout = _rms_mxu(hout, mln_ref[:, hs]) * jax.nn.sigmoid(og)
        o_ref[:, GW + h * DH:GW + (h + 1) * DH] = hout
        m_out = jnp.where(lane == h, m_t, m_out)
        yield
    return m_out


def _mixer_seg_kernel(proj_ref, gate_ref, lbl_ref, hgn_ref, mln_ref, gb_ref, s_ref, c_ref, n_ref, m_ref,
                      o_ref, so_ref, co_ref, no_ref, mo_ref, b_sc, k_sc, *, seg):
    def put_n(sg, hs, val):
        no_ref[sg:sg + 1, hs] = val

    m_out = _run(_mixer_core(proj_ref, gate_ref, lbl_ref, hgn_ref, mln_ref, gb_ref, s_ref, c_ref, so_ref, co_ref,
                             n_ref[...], m_ref[...], put_n, o_ref, b_sc, k_sc, seg))
    for sg in range(TILE // seg):
        last = sg * seg + seg - 1
        mo_ref[sg:sg + 1, :] = m_out[last:last + 1]


def _mixer_seg(proj, gates, lb_logits, hg_norm, ml_norm, gate_bias, s0, c0, n0, m0, batch, seq):
    n = batch * seq
    assert TILE % seq == 0 and seq % SUB == 0 and n % TILE == 0
    nseg = TILE // seq
    row = lambda shape: pl.BlockSpec(shape, lambda i: (i, 0))
    st_spec = pl.BlockSpec((nseg, HEADS, DH, DH), lambda i: (i, 0, 0, 0))
    st_shape = jax.ShapeDtypeStruct((batch, HEADS, DH, DH), F32)
    m_rows = jnp.repeat(jnp.pad(m0, ((0, 0), (0, DH - HEADS))), seq, axis=0)
    o, s_new, c_new, n_new, m_new = pl.pallas_call(
        functools.partial(_mixer_seg_kernel, seg=seq),
        grid=(n // TILE,),
        in_specs=[row((TILE, 8 * GW)), row((TILE, 2 * DH)),
                  _resident((2, GW)), _resident((1, GW)), _resident((1, GW)), _resident((1, 2 * DH)),
                  st_spec, st_spec, row((nseg, GW)), row((TILE, DH))],
        out_specs=[row((TILE, 2 * GW)), st_spec, st_spec, row((nseg, GW)), row((nseg, DH))],
        out_shape=[jax.ShapeDtypeStruct((n, 2 * GW), F32), st_shape, st_shape,
                   jax.ShapeDtypeStruct((batch, GW), F32), jax.ShapeDtypeStruct((batch, DH), F32)],
        scratch_shapes=[pltpu.VMEM((HEADS, TILE, DH), F32), pltpu.VMEM((HEADS, TILE, DH), F32)],
        compiler_params=pltpu.CompilerParams(
            dimension_semantics=("arbitrary",), vmem_limit_bytes=VMEM_LIMIT),
        name="mixer_seg",
    )(proj, gates, lb_logits, hg_norm, ml_norm, gate_bias, s0, c0, n0.reshape(batch, GW), m_rows)
    return o, s_new, c_new, n_new.reshape(batch, HEADS, DH), m_new[:, :HEADS]


SEQS_PER_STEP = 4


def _mixer_carry_kernel(proj_ref, gate_ref, lbl_ref, hgn_ref, mln_ref, gb_ref, s_ref, c_ref, n_ref, m_ref,
                        o_ref, so_ref, co_ref, no_ref, mo_ref, b_sc, k_sc):
    @pl.when(pl.program_id(1) == 0)
    def _():
        so_ref[...] = s_ref[...]
        co_ref[...] = c_ref[...]
        no_ref[...] = n_ref[...]
        mo_ref[...] = m_ref[...]

    tiles = []
    for j in range(SEQS_PER_STEP):
        def put_n(sg, hs, val, j=j):
            no_ref[j, :, hs] = val

        state, cell = so_ref.at[j:j + 1], co_ref.at[j:j + 1]
        tiles.append(_mixer_core(proj_ref.at[j], gate_ref.at[j], lbl_ref, hgn_ref, mln_ref, gb_ref,
                                 state, cell, state, cell, no_ref[j], jnp.broadcast_to(mo_ref[j], (TILE, DH)),
                                 put_n, o_ref.at[j], b_sc.at[j], k_sc.at[j], TILE))
    for j, m_out in enumerate(_interleave(*tiles)):
        mo_ref[j] = m_out[TILE - 1:TILE]


def _mixer_carry(proj, gates, lb_logits, hg_norm, ml_norm, gate_bias, s0, c0, n0, m0, batch, seq):
    assert seq % TILE == 0 and batch % SEQS_PER_STEP == 0
    nb = SEQS_PER_STEP
    tok = lambda w: pl.BlockSpec((nb, TILE, w), lambda b, t: (b, t, 0))
    st_spec = pl.BlockSpec((nb, HEADS, DH, DH), lambda b, t: (b, 0, 0, 0))
    n_spec = pl.BlockSpec((nb, 1, GW), lambda b, t: (b, 0, 0))
    m_spec = pl.BlockSpec((nb, 1, DH), lambda b, t: (b, 0, 0))
    st_shape = jax.ShapeDtypeStruct((batch, HEADS, DH, DH), F32)
    o, s_new, c_new, n_new, m_new = pl.pallas_call(
        _mixer_carry_kernel,
        grid=(batch // nb, seq // TILE),
        in_specs=[tok(8 * GW), tok(2 * DH),
                  _resident((2, GW)), _resident((1, GW)), _resident((1, GW)), _resident((1, 2 * DH)),
                  st_spec, st_spec, n_spec, m_spec],
        out_specs=[tok(2 * GW), st_spec, st_spec, n_spec, m_spec],
        out_shape=[jax.ShapeDtypeStruct((batch, seq, 2 * GW), F32), st_shape, st_shape,
                   jax.ShapeDtypeStruct((batch, 1, GW), F32), jax.ShapeDtypeStruct((batch, 1, DH), F32)],
        scratch_shapes=[pltpu.VMEM((nb, HEADS, TILE, DH), F32), pltpu.VMEM((nb, HEADS, TILE, DH), F32)],
        compiler_params=pltpu.CompilerParams(
            dimension_semantics=("arbitrary", "arbitrary"), vmem_limit_bytes=VMEM_LIMIT),
        name="mixer_carry",
    )(proj.reshape(batch, seq, 8 * GW), gates.reshape(batch, seq, 2 * DH), lb_logits, hg_norm, ml_norm, gate_bias,
      s0, c0, n0.reshape(batch, 1, GW), jnp.pad(m0, ((0, 0), (0, DH - HEADS))).reshape(batch, 1, DH))
    return o, s_new, c_new, n_new.reshape(batch, HEADS, DH), m_new.reshape(batch, DH)[:, :HEADS]


def _ffn_core(x, o, wo_ref, g2_ref, wg_ref, wv_ref, cw_ref, cb_ref, wd_ref, gf_ref, hist_ref, nb_ref, u_sc, g_sc, bb, tt):
    m = bb * tt
    x1 = x + _bdot(o, wo_ref[...])
    h2 = _rms(x1, g2_ref[...]).astype(BF16)
    tpos = lax.broadcasted_iota(jnp.int32, (m, 1), 0) & (tt - 1)
    for c in range(0, D_FF, FF_CHUNK):
        yield
        cs = slice(c, c + FF_CHUNK)
        u = jnp.dot(h2, wg_ref[:, cs], preferred_element_type=F32)
        val = jnp.dot(h2, wv_ref[:, cs], preferred_element_type=F32)
        u_sc[SUB:SUB + m, cs] = u
        u1 = u_sc[SUB - 1:SUB - 1 + m, cs]
        u2 = u_sc[SUB - 2:SUB - 2 + m, cs]
        if bb > 1:
            rows = lambda j0: jnp.concatenate(
                [jnp.broadcast_to(hist_ref[j:j + 1, j0 + c:j0 + c + FF_CHUNK], (tt, FF_CHUNK)) for j in range(bb)],
                axis=0)
            older, newer = rows(0), rows(D_FF)
            u1 = jnp.where(tpos == 0, newer, u1)
            u2 = jnp.where(tpos == 0, older, jnp.where(tpos == 1, newer, u2))
            last = u.reshape(bb, tt, FF_CHUNK)
            for j in range(CONV_W - 1):
                nb_ref[:, j * D_FF + c:j * D_FF + c + FF_CHUNK] = last[:, tt - (CONV_W - 1) + j, :]
        conv = cb_ref[:, cs] + cw_ref[0:1, cs] * u2 + cw_ref[1:2, cs] * u1 + cw_ref[2:3, cs] * u
        g_sc[:, cs] = (jax.nn.gelu(conv) * val).astype(BF16)
    if bb == 1:
        u_sc[SUB - 2:SUB, :] = u_sc[SUB + m - 2:SUB + m, :]
    yield
    y = x1 + jnp.dot(g_sc[...], wd_ref[...], preferred_element_type=F32)
    return _rms(y, gf_ref[...])


def _ffn_kernel(x_ref, o_ref, buf_ref, wo_ref, g2_ref, wg_ref, wv_ref, cw_ref, cb_ref, wd_ref, gf_ref,
                y_ref, nb_ref, u_sc, g_sc, *, bb, tt):
    m = bb * tt
    if bb > 1:
        u_sc[0:SUB, :] = jnp.zeros((SUB, D_FF), F32)
    else:
        @pl.when(pl.program_id(1) == 0)
        def _():
            for j in range(CONV_W - 1):
                u_sc[SUB - (CONV_W - 1) + j:SUB - (CONV_W - 1) + j + 1, :] = buf_ref[0, :, j * D_FF:(j + 1) * D_FF]

    y = _run(_ffn_core(x_ref[...].reshape(m, D_MODEL), o_ref[...].reshape(m, D_MODEL), wo_ref, g2_ref, wg_ref,
                       wv_ref, cw_ref, cb_ref, wd_ref, gf_ref, buf_ref, nb_ref, u_sc, g_sc, bb, tt))
    if bb == 1:
        for j in range(CONV_W - 1):
            nb_ref[0, :, j * D_FF:(j + 1) * D_FF] = u_sc[SUB - (CONV_W - 1) + j:SUB - (CONV_W - 1) + j + 1, :]
    y_ref[...] = y.reshape(bb, tt, D_MODEL)


def _ffn(x, o, buf, wo, g2, wg, wv, cw, cb, wd, gf, bb, tt):
    batch, seq, _ = x.shape
    assert batch % bb == 0 and seq % tt == 0 and tt % SUB == 0 and tt >= CONV_W - 1
    kern = functools.partial(_ffn_kernel, bb=bb, tt=tt)
    tok = pl.BlockSpec((bb, tt, D_MODEL), lambda b, t: (b, t, 0))
    assert tt & (tt - 1) == 0 and (bb == 1 or seq == tt)
    hw = (CONV_W - 1) * D_FF
    if bb == 1:
        hist, hist_shape = pl.BlockSpec((1, 1, hw), lambda b, t: (b, 0, 0)), (batch, 1, hw)
    else:
        hist, hist_shape = pl.BlockSpec((bb, hw), lambda b, t: (b, 0)), (batch, hw)
    y, buf_new = pl.pallas_call(
        kern,
        grid=(batch // bb, seq // tt),
        in_specs=[tok, tok, hist,
                  _resident((D_MODEL, D_MODEL)), _resident((1, D_MODEL)),
                  _resident((D_MODEL, D_FF)), _resident((D_MODEL, D_FF)),
                  _resident((CONV_W, D_FF)), _resident((1, D_FF)),
                  _resident((D_FF, D_MODEL)), _resident((1, D_MODEL))],
        out_specs=[tok, hist],
        out_shape=[jax.ShapeDtypeStruct((batch, seq, D_MODEL), F32),
                   jax.ShapeDtypeStruct(hist_shape, F32)],
        scratch_shapes=[pltpu.VMEM((SUB + bb * tt, D_FF), F32), pltpu.VMEM((bb * tt, D_FF), BF16)],
        compiler_params=pltpu.CompilerParams(
            dimension_semantics=("arbitrary", "arbitrary"), vmem_limit_bytes=VMEM_LIMIT),
        name="ffn",
    )(x, o, buf.reshape(hist_shape), wo, g2, wg, wv, cw, cb, wd, gf)
    return y, buf_new.reshape(batch, CONV_W - 1, D_FF)


ROWS_PER_STEP = 512


def _layer(x, s0, c0, n0, m0, buf0, p):
    batch, seq, _ = x.shape
    n = batch * seq
    proj, gates = _inproj(x.reshape(n, D_MODEL), p["g1"], p["w_all"], p["w_gate2"], ROWS_PER_STEP)
    mixer = _mixer_carry if seq >= TILE else _mixer_seg
    o, s_new, c_new, n_new, m_new = mixer(proj, gates, p["lb_logits"], p["hg_norm"], p["ml_norm"],
                                          p["gate_bias"], s0, c0, n0, m0, batch, seq)
    tt = min(seq, ROWS_PER_STEP)
    bb = ROWS_PER_STEP // tt
    y, buf_new = _ffn(x, o.reshape(batch, seq, D_MODEL), buf0, p["wo"], p["g2"], p["wg"], p["wv"],
                      p["cw"], p["cb"], p["wd"], p["gf"], bb, tt)
    return y, s_new[None], c_new[None], n_new[None], m_new[None], buf_new[None]


def kernel(x_prompt, x_sample, state_hgrn_S, state_mlstm_C, state_mlstm_n, state_mlstm_m, state_conv, norm1_g, w_in, hg_lb_logits, hg_norm_g, ml_b_ig, ml_b_fg, ml_norm_g, w_out, norm2_g, w_gate, w_val, conv_w, conv_b, w_down, final_norm_g):
    assert norm1_g.shape[0] == 1, "single-layer trunk"
    w = w_in[0]
    gate_cols = w[:, 8 * GW:]
    zpad = jnp.zeros((D_MODEL, DH - HEADS), w.dtype)
    w_gate2 = jnp.concatenate([gate_cols[:, :HEADS], zpad, gate_cols[:, HEADS:], zpad], axis=1)
    bpad = jnp.zeros((DH - HEADS,), F32)
    p = {
        "g1": norm1_g, "w_all": w.astype(BF16), "w_gate2": w_gate2.astype(BF16),
        "lb_logits": hg_lb_logits, "hg_norm": hg_norm_g, "ml_norm": ml_norm_g,
        "gate_bias": jnp.concatenate([ml_b_ig[0], bpad, ml_b_fg[0], bpad])[None],
        "wo": w_out[0].astype(BF16), "g2": norm2_g, "wg": w_gate[0].astype(BF16),
        "wv": w_val[0].astype(BF16), "cw": conv_w[0], "cb": conv_b, "wd": w_down[0].astype(BF16),
        "gf": final_norm_g[None],
    }
    b = x_prompt.shape[0]
    zs = jnp.zeros((b, HEADS, DH, DH), F32)
    prompt = _layer(x_prompt, zs, zs, jnp.zeros((b, HEADS, DH), F32), jnp.zeros((b, HEADS), F32),
                    jnp.zeros((b, CONV_W - 1, D_FF), F32), p)
    sample = _layer(x_sample, state_hgrn_S[0], state_mlstm_C[0], state_mlstm_n[0], state_mlstm_m[0],
                    state_conv[0], p)
    out = []
    for a, c in zip(prompt, sample):
        out += [a, c]
    return tuple(out)
```

```python
import functools

import jax
import jax.numpy as jnp
from jax import lax
from jax.experimental import pallas as pl
from jax.experimental.pallas import tpu as pltpu

F32 = jnp.float32
BF16 = jnp.bfloat16
HI = lax.Precision.HIGHEST
NT = (((1,), (1,)), ((), ()))
TN = (((0,), (0,)), ((), ()))

D_MODEL = 1024
HEADS = 4
DH = 128
GW = HEADS * DH
D_FF = 2816
CONV_W = 3
EPS = 1e-6
TILE = 128
SUB = 8
FF_CHUNK = 256
VMEM_LIMIT = 56 * 1024 * 1024


def _rms(x, g):
    return x * lax.rsqrt(jnp.mean(x * x, axis=-1, keepdims=True) + EPS) * g


def _rms_mxu(x, g):
    ms = jnp.dot((x * x).astype(BF16), jnp.full((DH, DH), 1.0 / DH, BF16), preferred_element_type=F32)
    return x * lax.rsqrt(ms + EPS) * g


def _bdot(a, b):
    return jnp.dot(a.astype(BF16), b.astype(BF16), preferred_element_type=F32)


def _resident(shape):
    zeros = (0,) * len(shape)
    return pl.BlockSpec(shape, lambda *_: zeros, pipeline_mode=pl.Buffered(1))


def _run(gen):
    try:
        while True:
            next(gen)
    except StopIteration as stop:
        return stop.value


def _interleave(primary, *others):
    gens = [primary, *others]
    done = [False] * len(gens)
    vals = [None] * len(gens)

    def step(j):
        try:
            next(gens[j])
        except StopIteration as stop:
            done[j], vals[j] = True, stop.value

    turn = 0
    while not all(done):
        if not done[0]:
            step(0)
        pending = [j for j in range(1, len(gens)) if not done[j]]
        if pending:
            step(pending[turn % len(pending)])
            turn += 1
    return vals


def _inproj_core(x, g_ref, w_ref, wg_ref, proj_ref, gate_ref):
    hb = _rms(x, g_ref[...]).astype(BF16)
    gate_ref[...] = jnp.dot(hb, wg_ref[...], preferred_element_type=F32)
    for c in range(0, 8 * GW, GW):
        yield
        proj_ref[:, c:c + GW] = jnp.dot(hb, w_ref[:, c:c + GW], preferred_element_type=F32)


def _inproj_kernel(x_ref, g_ref, w_ref, wg_ref, proj_ref, gate_ref):
    _run(_inproj_core(x_ref[...], g_ref, w_ref, wg_ref, proj_ref, gate_ref))


IN_COLS = 8 * GW + 2 * HEADS


def _inproj(x2, g1, w_all, w_gate2, tm):
    n = x2.shape[0]
    return pl.pallas_call(
        _inproj_kernel,
        grid=(n // tm,),
        in_specs=[
            pl.BlockSpec((tm, D_MODEL), lambda i: (i, 0)),
            _resident((1, D_MODEL)),
            _resident((D_MODEL, IN_COLS)),
            _resident((D_MODEL, 2 * DH)),
        ],
        out_specs=[
            pl.BlockSpec((tm, 8 * GW), lambda i: (i, 0)),
            pl.BlockSpec((tm, 2 * DH), lambda i: (i, 0)),
        ],
        out_shape=[
            jax.ShapeDtypeStruct((n, 8 * GW), F32),
            jax.ShapeDtypeStruct((n, 2 * DH), F32),
        ],
        compiler_params=pltpu.CompilerParams(
            dimension_semantics=("arbitrary",), vmem_limit_bytes=VMEM_LIMIT),
        name="inproj",
    )(x2, g1, w_all, w_gate2)


def _seg_last(x, seg):
    nseg = TILE // seg
    w = x.shape[-1]
    if nseg == 1:
        last = jnp.broadcast_to(x[TILE - SUB:TILE][SUB - 1:SUB], (SUB, w))
        return jnp.concatenate([last] * (TILE // SUB), axis=0)
    y = x.reshape(nseg, seg, w)[:, seg - 1:seg, :]
    return jnp.broadcast_to(y, (nseg, seg, w)).reshape(TILE, w)


def _hgrn_intra(q, k, v, b, seg, b_ref, k_ref, v_ref):
    sub = lax.broadcasted_iota(jnp.int32, (SUB, 1), 0)
    blocks = []
    for r0 in range(0, TILE, SUB):
        bb = b[r0:r0 + SUB]
        qb = q[r0:r0 + SUB]
        acc = jnp.zeros((SUB, DH), F32)
        for s in range(SUB):
            bs = jnp.broadcast_to(b_ref[r0 + s:r0 + s + 1], (SUB, DH))
            ks = jnp.broadcast_to(k_ref[r0 + s:r0 + s + 1], (SUB, DH))
            vs = jnp.broadcast_to(v_ref[r0 + s:r0 + s + 1], (SUB, DH))
            p = jnp.exp2(bb - bs) * (qb * ks)
            col = jnp.sum(p, axis=-1, keepdims=True)
            col = jnp.where(sub >= s, col, 0.0)
            acc = acc + col * vs
        blocks.append(acc)
        if r0 % (4 * SUB) == 3 * SUB:
            yield
    intra = jnp.concatenate(blocks, axis=0)

    if seg > SUB:
        ri = lax.broadcasted_iota(jnp.int32, (TILE, TILE), 0)
        ci = lax.broadcasted_iota(jnp.int32, (TILE, TILE), 1)
        a_off = jnp.zeros((TILE, TILE), F32)
        w = SUB
        while w < seg:
            zeros = jnp.zeros((w, DH), F32)
            qs, ks = [], []
            for r0 in range(0, TILE, 2 * w):
                ref = b[r0 + w - 1:r0 + w]
                ks += [k[r0:r0 + w] * jnp.exp2(ref - b[r0:r0 + w]), zeros]
                qs += [zeros, q[r0 + w:r0 + 2 * w] * jnp.exp2(b[r0 + w:r0 + 2 * w] - ref)]
            qt = jnp.concatenate(qs, axis=0).astype(BF16)
            kt = jnp.concatenate(ks, axis=0).astype(BF16)
            a_lvl = lax.dot_general(qt, kt, NT, preferred_element_type=F32)
            same_block = (ri // (2 * w)) == (ci // (2 * w))
            a_off = a_off + jnp.where(same_block, a_lvl, 0.0)
            w *= 2
        yield
        intra = intra + _bdot(a_off, v)
    return intra


def _mixer_core(proj_ref, gate_ref, lbl_ref, hgn_ref, mln_ref, gb_ref,
                s_in, c_in, s_out, c_out, n_all, m_rows, put_n, o_ref, b_sc, k_sc, seg):
    nseg = TILE // seg
    ri = lax.broadcasted_iota(jnp.int32, (TILE, TILE), 0)
    ci = lax.broadcasted_iota(jnp.int32, (TILE, TILE), 1)
    tri = ((ri // seg) == (ci // seg)) & (ci <= ri)
    tri_f = tri.astype(F32)
    lane = lax.broadcasted_iota(jnp.int32, (TILE, DH), 1)

    lg = lbl_ref[...]
    ex = jnp.exp(lg - jnp.max(lg, axis=0, keepdims=True))
    lb_all = ex[0:1] / jnp.sum(ex, axis=0, keepdims=True)

    ig_all = gate_ref[:, 0:DH] + gb_ref[:, 0:DH]
    lf_all = jax.nn.log_sigmoid(gate_ref[:, DH:2 * DH] + gb_ref[:, DH:2 * DH])
    f_all = lb_all + (1.0 - lb_all) * jax.nn.sigmoid(proj_ref[:, GW:2 * GW])
    cums = jnp.dot(tri_f, jnp.concatenate([jnp.log2(f_all), lf_all], axis=1), precision=HI,
                   preferred_element_type=F32)
    a_all = cums[:, GW:GW + DH]
    a_all_t = a_all.T
    ig_all_t = ig_all.T
    m_out = jnp.zeros((TILE, DH), F32)
    yield

    for h in range(HEADS):
        hs = slice(h * DH, (h + 1) * DH)

        q = proj_ref[:, 0 * GW + h * DH:0 * GW + (h + 1) * DH]
        v = proj_ref[:, 2 * GW + h * DH:2 * GW + (h + 1) * DH]
        og = proj_ref[:, 3 * GW + h * DH:3 * GW + (h + 1) * DH]
        k = 1.0 - f_all[:, hs]
        b = cums[:, hs]
        b_last = _seg_last(b, seg)
        qe = (q * jnp.exp2(b)).astype(BF16)
        kd = (k * jnp.exp2(b_last - b)).astype(BF16)
        vb16 = v.astype(BF16)
        e_last = jnp.exp2(b_last)
        b_sc[h] = b
        k_sc[h] = k
        yield
        out = yield from _hgrn_intra(q, k, v, b, seg, b_sc.at[h], k_sc.at[h],
                                     proj_ref.at[:, 2 * GW + h * DH:2 * GW + (h + 1) * DH])
        inter = []
        for sg in range(nseg):
            rows = slice(sg * seg, (sg + 1) * seg)
            st = s_in[sg, h]
            inter.append(jnp.dot(qe[rows], st.astype(BF16), preferred_element_type=F32))
            decay = jnp.broadcast_to(e_last[sg * seg:sg * seg + 1], (DH, DH)).T
            s_out[sg, h] = decay * st + lax.dot_general(kd[rows], vb16[rows], TN, preferred_element_type=F32)
        out = out + jnp.concatenate(inter, axis=0)
        out = _rms_mxu(out, hgn_ref[:, hs]) * (og * jax.nn.sigmoid(og))
        o_ref[:, hs] = out
        yield

        q = proj_ref[:, 4 * GW + h * DH:4 * GW + (h + 1) * DH]
        k = proj_ref[:, 5 * GW + h * DH:5 * GW + (h + 1) * DH] * (DH ** -0.5)
        v = proj_ref[:, 6 * GW + h * DH:6 * GW + (h + 1) * DH]
        og = proj_ref[:, 7 * GW + h * DH:7 * GW + (h + 1) * DH]
        a_col = jnp.broadcast_to(a_all[:, h:h + 1], (TILE, TILE))
        i_col = jnp.broadcast_to(ig_all[:, h:h + 1], (TILE, TILE))
        m_col = jnp.broadcast_to(m_rows[:, h:h + 1], (TILE, TILE))
        a_row = a_all_t[h:h + 1]
        i_row = ig_all_t[h:h + 1]
        log_d = jnp.where(tri, a_col - a_row + i_row, -jnp.inf)
        log_inter = a_col + m_col
        m_t = jnp.maximum(log_inter, jnp.max(log_d, axis=-1, keepdims=True))
        d_w = jnp.exp(log_d - m_t)
        w_i = jnp.exp(log_inter - m_t)
        qb16 = q.astype(BF16)
        kb16 = k.astype(BF16)
        vb16 = v.astype(BF16)
        s_w = lax.dot_general(qb16, kb16, NT, preferred_element_type=F32) * d_w
        nd = jnp.dot(s_w.astype(BF16), jnp.concatenate([vb16, jnp.ones((TILE, DH), BF16)], axis=1),
                     preferred_element_type=F32)
        num, den = nd[:, 0:DH], nd[:, DH:2 * DH]
        yield

        m_new = _seg_last(m_t, seg)
        a_end = _seg_last(a_col, seg)
        w_end = jnp.exp(a_end - a_col + i_col - m_new)
        f_end = jnp.exp(a_end + m_col - m_new)
        kw = k * w_end
        kw16 = kw.astype(BF16)
        qc, qn = [], []
        for sg in range(nseg):
            rows = slice(sg * seg, (sg + 1) * seg)
            last = sg * seg + seg - 1
            ct = c_in[sg, h]
            qc.append(jnp.dot(qb16[rows], ct.astype(BF16), preferred_element_type=F32))
            qn.append(q[rows] * n_all[sg:sg + 1, hs])
            fe = f_end[last:last + 1, 0:1]
            c_out[sg, h] = fe * ct + lax.dot_general(kw16[rows], vb16[rows], TN, preferred_element_type=F32)
            put_n(sg, hs, fe * n_all[sg:sg + 1, hs] + jnp.sum(kw[rows], axis=0, keepdims=True))
        num = w_i * jnp.concatenate(qc, axis=0) + num
        qn = jnp.dot(jnp.concatenate(qn, axis=0).astype(BF16), jnp.ones((DH, DH), BF16), preferred_element_type=F32)
        den = w_i * qn + den
        hout = num / jnp.maximum(jnp.abs(den), jnp.exp(-m_t))
        hout = _rms_mxu(hout, mln_ref[:, hs]) * jax.nn.sigmoid(og)
        o_ref[:, GW + h * DH:GW + (h + 1) * DH] = hout
        m_out = jnp.where(lane == h, m_t, m_out)
        yield
    return m_out


def _mixer_seg_kernel(proj_ref, gate_ref, lbl_ref, hgn_ref, mln_ref, gb_ref, s_ref, c_ref, n_ref, m_ref,
                      o_ref, so_ref, co_ref, no_ref, mo_ref, b_sc, k_sc, *, seg):
    def put_n(sg, hs, val):
        no_ref[sg:sg + 1, hs] = val

    m_out = _run(_mixer_core(proj_ref, gate_ref, lbl_ref, hgn_ref, mln_ref, gb_ref, s_ref, c_ref, so_ref, co_ref,
                             n_ref[...], m_ref[...], put_n, o_ref, b_sc, k_sc, seg))
    for sg in range(TILE // seg):
        last = sg * seg + seg - 1
        mo_ref[sg:sg + 1, :] = m_out[last:last + 1]


def _mixer_seg(proj, gates, lb_logits, hg_norm, ml_norm, gate_bias, s0, c0, n0, m0, batch, seq):
    n = batch * seq
    assert TILE % seq == 0 and seq % SUB == 0 and n % TILE == 0
    nseg = TILE // seq
    row = lambda shape: pl.BlockSpec(shape, lambda i: (i, 0))
    st_spec = pl.BlockSpec((nseg, HEADS, DH, DH), lambda i: (i, 0, 0, 0))
    st_shape = jax.ShapeDtypeStruct((batch, HEADS, DH, DH), F32)
    m_rows = jnp.repeat(jnp.pad(m0, ((0, 0), (0, DH - HEADS))), seq, axis=0)
    o, s_new, c_new, n_new, m_new = pl.pallas_call(
        functools.partial(_mixer_seg_kernel, seg=seq),
        grid=(n // TILE,),
        in_specs=[row((TILE, 8 * GW)), row((TILE, 2 * DH)),
                  _resident((2, GW)), _resident((1, GW)), _resident((1, GW)), _resident((1, 2 * DH)),
                  st_spec, st_spec, row((nseg, GW)), row((TILE, DH))],
        out_specs=[row((TILE, 2 * GW)), st_spec, st_spec, row((nseg, GW)), row((nseg, DH))],
        out_shape=[jax.ShapeDtypeStruct((n, 2 * GW), F32), st_shape, st_shape,
                   jax.ShapeDtypeStruct((batch, GW), F32), jax.ShapeDtypeStruct((batch, DH), F32)],
        scratch_shapes=[pltpu.VMEM((HEADS, TILE, DH), F32), pltpu.VMEM((HEADS, TILE, DH), F32)],
        compiler_params=pltpu.CompilerParams(
            dimension_semantics=("arbitrary",), vmem_limit_bytes=VMEM_LIMIT),
        name="mixer_seg",
    )(proj, gates, lb_logits, hg_norm, ml_norm, gate_bias, s0, c0, n0.reshape(batch, GW), m_rows)
    return o, s_new, c_new, n_new.reshape(batch, HEADS, DH), m_new[:, :HEADS]


SEQS_PER_STEP = 4


def _mixer_carry_kernel(proj_ref, gate_ref, lbl_ref, hgn_ref, mln_ref, gb_ref, s_ref, c_ref, n_ref, m_ref,
                        o_ref, so_ref, co_ref, no_ref, mo_ref, b_sc, k_sc):
    @pl.when(pl.program_id(1) == 0)
    def _():
        so_ref[...] = s_ref[...]
        co_ref[...] = c_ref[...]
        no_ref[...] = n_ref[...]
        mo_ref[...] = m_ref[...]

    tiles = []
    for j in range(SEQS_PER_STEP):
        def put_n(sg, hs, val, j=j):
            no_ref[j, :, hs] = val

        state, cell = so_ref.at[j:j + 1], co_ref.at[j:j + 1]
        tiles.append(_mixer_core(proj_ref.at[j], gate_ref.at[j], lbl_ref, hgn_ref, mln_ref, gb_ref,
                                 state, cell, state, cell, no_ref[j], jnp.broadcast_to(mo_ref[j], (TILE, DH)),
                                 put_n, o_ref.at[j], b_sc.at[j], k_sc.at[j], TILE))
    for j, m_out in enumerate(_interleave(*tiles)):
        mo_ref[j] = m_out[TILE - 1:TILE]


def _mixer_carry(proj, gates, lb_logits, hg_norm, ml_norm, gate_bias, s0, c0, n0, m0, batch, seq):
    assert seq % TILE == 0 and batch % SEQS_PER_STEP == 0
    nb = SEQS_PER_STEP
    tok = lambda w: pl.BlockSpec((nb, TILE, w), lambda b, t: (b, t, 0))
    st_spec = pl.BlockSpec((nb, HEADS, DH, DH), lambda b, t: (b, 0, 0, 0))
    n_spec = pl.BlockSpec((nb, 1, GW), lambda b, t: (b, 0, 0))
    m_spec = pl.BlockSpec((nb, 1, DH), lambda b, t: (b, 0, 0))
    st_shape = jax.ShapeDtypeStruct((batch, HEADS, DH, DH), F32)
    o, s_new, c_new, n_new, m_new = pl.pallas_call(
        _mixer_carry_kernel,
        grid=(batch // nb, seq // TILE),
        in_specs=[tok(8 * GW), tok(2 * DH),
                  _resident((2, GW)), _resident((1, GW)), _resident((1, GW)), _resident((1, 2 * DH)),
                  st_spec, st_spec, n_spec, m_spec],
        out_specs=[tok(2 * GW), st_spec, st_spec, n_spec, m_spec],
        out_shape=[jax.ShapeDtypeStruct((batch, seq, 2 * GW), F32), st_shape, st_shape,
                   jax.ShapeDtypeStruct((batch, 1, GW), F32), jax.ShapeDtypeStruct((batch, 1, DH), F32)],
        scratch_shapes=[pltpu.VMEM((nb, HEADS, TILE, DH), F32), pltpu.VMEM((nb, HEADS, TILE, DH), F32)],
        compiler_params=pltpu.CompilerParams(
            dimension_semantics=("arbitrary", "arbitrary"), vmem_limit_bytes=VMEM_LIMIT),
        name="mixer_carry",
    )(proj.reshape(batch, seq, 8 * GW), gates.reshape(batch, seq, 2 * DH), lb_logits, hg_norm, ml_norm, gate_bias,
      s0, c0, n0.reshape(batch, 1, GW), jnp.pad(m0, ((0, 0), (0, DH - HEADS))).reshape(batch, 1, DH))
    return o, s_new, c_new, n_new.reshape(batch, HEADS, DH), m_new.reshape(batch, DH)[:, :HEADS]


def _ffn_core(x, o, wo_ref, g2_ref, wg_ref, wv_ref, cw_ref, cb_ref, wd_ref, gf_ref, hist_ref, nb_ref, u_sc, g_sc, bb, tt):
    m = bb * tt
    x1 = x + _bdot(o, wo_ref[...])
    h2 = _rms(x1, g2_ref[...]).astype(BF16)
    tpos = lax.broadcasted_iota(jnp.int32, (m, 1), 0) & (tt - 1)
    for c in range(0, D_FF, FF_CHUNK):
        yield
        cs = slice(c, c + FF_CHUNK)
        u = jnp.dot(h2, wg_ref[:, cs], preferred_element_type=F32)
        val = jnp.dot(h2, wv_ref[:, cs], preferred_element_type=F32)
        u_sc[SUB:SUB + m, cs] = u
        u1 = u_sc[SUB - 1:SUB - 1 + m, cs]
        u2 = u_sc[SUB - 2:SUB - 2 + m, cs]
        if bb > 1:
            rows = lambda j0: jnp.concatenate(
                [jnp.broadcast_to(hist_ref[j:j + 1, j0 + c:j0 + c + FF_CHUNK], (tt, FF_CHUNK)) for j in range(bb)],
                axis=0)
            older, newer = rows(0), rows(D_FF)
            u1 = jnp.where(tpos == 0, newer, u1)
            u2 = jnp.where(tpos == 0, older, jnp.where(tpos == 1, newer, u2))
            last = u.reshape(bb, tt, FF_CHUNK)
            for j in range(CONV_W - 1):
                nb_ref[:, j, cs] = last[:, tt - (CONV_W - 1) + j, :]
        conv = cb_ref[:, cs] + cw_ref[0:1, cs] * u2 + cw_ref[1:2, cs] * u1 + cw_ref[2:3, cs] * u
        g_sc[:, cs] = (jax.nn.gelu(conv) * val).astype(BF16)
    if bb == 1:
        u_sc[SUB - 2:SUB, :] = u_sc[SUB + m - 2:SUB + m, :]
    yield
    y = x1 + jnp.dot(g_sc[...], wd_ref[...], preferred_element_type=F32)
    return _rms(y, gf_ref[...])


def _ffn_kernel(x_ref, o_ref, buf_ref, wo_ref, g2_ref, wg_ref, wv_ref, cw_ref, cb_ref, wd_ref, gf_ref,
                y_ref, nb_ref, u_sc, g_sc, *, bb, tt):
    m = bb * tt
    if bb > 1:
        u_sc[0:SUB, :] = jnp.zeros((SUB, D_FF), F32)
    else:
        @pl.when(pl.program_id(1) == 0)
        def _():
            for j in range(CONV_W - 1):
                u_sc[SUB - (CONV_W - 1) + j:SUB - (CONV_W - 1) + j + 1, :] = buf_ref[0, :, j * D_FF:(j + 1) * D_FF]

    y = _run(_ffn_core(x_ref[...].reshape(m, D_MODEL), o_ref[...].reshape(m, D_MODEL), wo_ref, g2_ref, wg_ref,
                       wv_ref, cw_ref, cb_ref, wd_ref, gf_ref, buf_ref, nb_ref, u_sc, g_sc, bb, tt))
    if bb == 1:
        nb_ref[0] = u_sc[SUB - (CONV_W - 1):SUB, :]
    y_ref[...] = y.reshape(bb, tt, D_MODEL)


def _ffn(x, o, buf, wo, g2, wg, wv, cw, cb, wd, gf, bb, tt):
    batch, seq, _ = x.shape
    assert batch % bb == 0 and seq % tt == 0 and tt % SUB == 0 and tt >= CONV_W - 1
    kern = functools.partial(_ffn_kernel, bb=bb, tt=tt)
    tok = pl.BlockSpec((bb, tt, D_MODEL), lambda b, t: (b, t, 0))
    assert tt & (tt - 1) == 0 and (bb == 1 or seq == tt)
    hw = (CONV_W - 1) * D_FF
    if bb == 1:
        hist, hist_shape = pl.BlockSpec((1, 1, hw), lambda b, t: (b, 0, 0)), (batch, 1, hw)
    else:
        hist, hist_shape = pl.BlockSpec((bb, hw), lambda b, t: (b, 0)), (batch, hw)
    hist_out = pl.BlockSpec((bb, CONV_W - 1, D_FF), lambda b, t: (b, 0, 0))
    return pl.pallas_call(
        kern,
        grid=(batch // bb, seq // tt),
        in_specs=[tok, tok, hist,
                  _resident((D_MODEL, D_MODEL)), _resident((1, D_MODEL)),
                  _resident((D_MODEL, D_FF)), _resident((D_MODEL, D_FF)),
                  _resident((CONV_W, D_FF)), _resident((1, D_FF)),
                  _resident((D_FF, D_MODEL)), _resident((1, D_MODEL))],
        out_specs=[tok, hist_out],
        out_shape=[jax.ShapeDtypeStruct((batch, seq, D_MODEL), F32),
                   jax.ShapeDtypeStruct((batch, CONV_W - 1, D_FF), F32)],
        scratch_shapes=[pltpu.VMEM((SUB + bb * tt, D_FF), F32), pltpu.VMEM((bb * tt, D_FF), BF16)],
        compiler_params=pltpu.CompilerParams(
            dimension_semantics=("arbitrary", "arbitrary"), vmem_limit_bytes=VMEM_LIMIT),
        name="ffn",
    )(x, o, buf.reshape(hist_shape), wo, g2, wg, wv, cw, cb, wd, gf)


ROWS_PER_STEP = 512


def _layer(x, s0, c0, n0, m0, buf0, p):
    batch, seq, _ = x.shape
    n = batch * seq
    proj, gates = _inproj(x.reshape(n, D_MODEL), p["g1"], p["w_all"], p["w_gate2"], ROWS_PER_STEP)
    mixer = _mixer_carry if seq >= TILE else _mixer_seg
    o, s_new, c_new, n_new, m_new = mixer(proj, gates, p["lb_logits"], p["hg_norm"], p["ml_norm"],
                                          p["gate_bias"], s0, c0, n0, m0, batch, seq)
    tt = min(seq, ROWS_PER_STEP)
    bb = ROWS_PER_STEP // tt
    y, buf_new = _ffn(x, o.reshape(batch, seq, D_MODEL), buf0, p["wo"], p["g2"], p["wg"], p["wv"],
                      p["cw"], p["cb"], p["wd"], p["gf"], bb, tt)
    return y, s_new[None], c_new[None], n_new[None], m_new[None], buf_new[None]


def kernel(x_prompt, x_sample, state_hgrn_S, state_mlstm_C, state_mlstm_n, state_mlstm_m, state_conv, norm1_g, w_in, hg_lb_logits, hg_norm_g, ml_b_ig, ml_b_fg, ml_norm_g, w_out, norm2_g, w_gate, w_val, conv_w, conv_b, w_down, final_norm_g):
    assert norm1_g.shape[0] == 1, "single-layer trunk"
    w = w_in[0]
    gate_cols = w[:, 8 * GW:]
    zpad = jnp.zeros((D_MODEL, DH - HEADS), w.dtype)
    w_gate2 = jnp.concatenate([gate_cols[:, :HEADS], zpad, gate_cols[:, HEADS:], zpad], axis=1)
    bpad = jnp.zeros((DH - HEADS,), F32)
    p = {
        "g1": norm1_g, "w_all": w.astype(BF16), "w_gate2": w_gate2.astype(BF16),
        "lb_logits": hg_lb_logits, "hg_norm": hg_norm_g, "ml_norm": ml_norm_g,
        "gate_bias": jnp.concatenate([ml_b_ig[0], bpad, ml_b_fg[0], bpad])[None],
        "wo": w_out[0].astype(BF16), "g2": norm2_g, "wg": w_gate[0].astype(BF16),
        "wv": w_val[0].astype(BF16), "cw": conv_w[0], "cb": conv_b, "wd": w_down[0].astype(BF16),
        "gf": final_norm_g[None],
    }
    b = x_prompt.shape[0]
    zs = jnp.zeros((b, HEADS, DH, DH), F32)
    prompt = _layer(x_prompt, zs, zs, jnp.zeros((b, HEADS, DH), F32), jnp.zeros((b, HEADS), F32),
                    jnp.zeros((b, CONV_W - 1, D_FF), F32), p)
    sample = _layer(x_sample, state_hgrn_S[0], state_mlstm_C[0], state_mlstm_n[0], state_mlstm_m[0],
                    state_conv[0], p)
    out = []
    for a, c in zip(prompt, sample):
        out += [a, c]
    return tuple(out)
```

```python
import functools

import jax
import jax.numpy as jnp
from jax import lax
from jax.experimental import pallas as pl
from jax.experimental.pallas import tpu as pltpu

F32 = jnp.float32
BF16 = jnp.bfloat16
HI = lax.Precision.HIGHEST
NT = (((1,), (1,)), ((), ()))
TN = (((0,), (0,)), ((), ()))

D_MODEL = 1024
HEADS = 4
DH = 128
GW = HEADS * DH
D_FF = 2816
CONV_W = 3
EPS = 1e-6
TILE = 128
SUB = 8
FF_CHUNK = 256
VMEM_LIMIT = 56 * 1024 * 1024


def _rms(x, g):
    return x * lax.rsqrt(jnp.mean(x * x, axis=-1, keepdims=True) + EPS) * g


def _rms_mxu(x, g):
    ms = jnp.dot((x * x).astype(BF16), jnp.full((DH, DH), 1.0 / DH, BF16), preferred_element_type=F32)
    return x * lax.rsqrt(ms + EPS) * g


def _bdot(a, b):
    return jnp.dot(a.astype(BF16), b.astype(BF16), preferred_element_type=F32)


def _resident(shape):
    zeros = (0,) * len(shape)
    return pl.BlockSpec(shape, lambda *_: zeros, pipeline_mode=pl.Buffered(1))


def _run(gen):
    try:
        while True:
            next(gen)
    except StopIteration as stop:
        return stop.value


def _interleave(primary, *others):
    gens = [primary, *others]
    done = [False] * len(gens)
    vals = [None] * len(gens)

    def step(j):
        try:
            next(gens[j])
        except StopIteration as stop:
            done[j], vals[j] = True, stop.value

    turn = 0
    while not all(done):
        if not done[0]:
            step(0)
        pending = [j for j in range(1, len(gens)) if not done[j]]
        if pending:
            step(pending[turn % len(pending)])
            turn += 1
    return vals


def _cast_kernel(w_ref, o_ref):
    o_ref[...] = w_ref[...].astype(BF16)


def _to_bf16(w, rows):
    r, c = w.shape
    assert r % rows == 0
    return pl.pallas_call(
        _cast_kernel,
        grid=(r // rows,),
        in_specs=[pl.BlockSpec((rows, c), lambda i: (i, 0))],
        out_specs=pl.BlockSpec((rows, c), lambda i: (i, 0)),
        out_shape=jax.ShapeDtypeStruct((r, c), BF16),
        compiler_params=pltpu.CompilerParams(dimension_semantics=("arbitrary",)),
        name="cast_bf16",
    )(w)


def _inproj_core(x, g_ref, w_ref, wg_ref, proj_ref, gate_ref):
    hb = _rms(x, g_ref[...]).astype(BF16)
    gate_ref[...] = jnp.dot(hb, wg_ref[...], preferred_element_type=F32)
    for c in range(0, 8 * GW, GW):
        yield
        proj_ref[:, c:c + GW] = jnp.dot(hb, w_ref[:, c:c + GW], preferred_element_type=F32)


def _inproj_kernel(x_ref, g_ref, w_ref, wg_ref, proj_ref, gate_ref):
    _run(_inproj_core(x_ref[...], g_ref, w_ref, wg_ref, proj_ref, gate_ref))


IN_COLS = 8 * GW + 2 * HEADS


def _inproj(x2, g1, w_all, w_gate2, tm):
    n = x2.shape[0]
    return pl.pallas_call(
        _inproj_kernel,
        grid=(n // tm,),
        in_specs=[
            pl.BlockSpec((tm, D_MODEL), lambda i: (i, 0)),
            _resident((1, D_MODEL)),
            _resident((D_MODEL, IN_COLS)),
            _resident((D_MODEL, 2 * DH)),
        ],
        out_specs=[
            pl.BlockSpec((tm, 8 * GW), lambda i: (i, 0)),
            pl.BlockSpec((tm, 2 * DH), lambda i: (i, 0)),
        ],
        out_shape=[
            jax.ShapeDtypeStruct((n, 8 * GW), F32),
            jax.ShapeDtypeStruct((n, 2 * DH), F32),
        ],
        compiler_params=pltpu.CompilerParams(
            dimension_semantics=("arbitrary",), vmem_limit_bytes=VMEM_LIMIT),
        name="inproj",
    )(x2, g1, w_all, w_gate2)


def _seg_last(x, seg):
    nseg = TILE // seg
    w = x.shape[-1]
    if nseg == 1:
        last = jnp.broadcast_to(x[TILE - SUB:TILE][SUB - 1:SUB], (SUB, w))
        return jnp.concatenate([last] * (TILE // SUB), axis=0)
    y = x.reshape(nseg, seg, w)[:, seg - 1:seg, :]
    return jnp.broadcast_to(y, (nseg, seg, w)).reshape(TILE, w)


def _hgrn_intra(q, k, v, b, seg, b_ref, k_ref, v_ref):
    sub = lax.broadcasted_iota(jnp.int32, (SUB, 1), 0)
    blocks = []
    for r0 in range(0, TILE, SUB):
        bb = b[r0:r0 + SUB]
        qb = q[r0:r0 + SUB]
        acc = jnp.zeros((SUB, DH), F32)
        for s in range(SUB):
            bs = jnp.broadcast_to(b_ref[r0 + s:r0 + s + 1], (SUB, DH))
            ks = jnp.broadcast_to(k_ref[r0 + s:r0 + s + 1], (SUB, DH))
            vs = jnp.broadcast_to(v_ref[r0 + s:r0 + s + 1], (SUB, DH))
            p = jnp.exp2(bb - bs) * (qb * ks)
            col = jnp.sum(p, axis=-1, keepdims=True)
            col = jnp.where(sub >= s, col, 0.0)
            acc = acc + col * vs
        blocks.append(acc)
        if r0 % (4 * SUB) == 3 * SUB:
            yield
    intra = jnp.concatenate(blocks, axis=0)

    if seg > SUB:
        ri = lax.broadcasted_iota(jnp.int32, (TILE, TILE), 0)
        ci = lax.broadcasted_iota(jnp.int32, (TILE, TILE), 1)
        a_off = jnp.zeros((TILE, TILE), F32)
        w = SUB
        while w < seg:
            zeros = jnp.zeros((w, DH), F32)
            qs, ks = [], []
            for r0 in range(0, TILE, 2 * w):
                ref = b[r0 + w - 1:r0 + w]
                ks += [k[r0:r0 + w] * jnp.exp2(ref - b[r0:r0 + w]), zeros]
                qs += [zeros, q[r0 + w:r0 + 2 * w] * jnp.exp2(b[r0 + w:r0 + 2 * w] - ref)]
            qt = jnp.concatenate(qs, axis=0).astype(BF16)
            kt = jnp.concatenate(ks, axis=0).astype(BF16)
            a_lvl = lax.dot_general(qt, kt, NT, preferred_element_type=F32)
            same_block = (ri // (2 * w)) == (ci // (2 * w))
            a_off = a_off + jnp.where(same_block, a_lvl, 0.0)
            w *= 2
        yield
        intra = intra + _bdot(a_off, v)
    return intra


def _mixer_core(proj_ref, gate_ref, lbl_ref, hgn_ref, mln_ref, gb_ref,
                s_in, c_in, s_out, c_out, n_all, m_rows, put_n, o_ref, b_sc, k_sc, seg):
    nseg = TILE // seg
    ri = lax.broadcasted_iota(jnp.int32, (TILE, TILE), 0)
    ci = lax.broadcasted_iota(jnp.int32, (TILE, TILE), 1)
    tri = ((ri // seg) == (ci // seg)) & (ci <= ri)
    tri_f = tri.astype(F32)
    lane = lax.broadcasted_iota(jnp.int32, (TILE, DH), 1)

    lg = lbl_ref[...]
    ex = jnp.exp(lg - jnp.max(lg, axis=0, keepdims=True))
    lb_all = ex[0:1] / jnp.sum(ex, axis=0, keepdims=True)

    ig_all = gate_ref[:, 0:DH] + gb_ref[:, 0:DH]
    lf_all = jax.nn.log_sigmoid(gate_ref[:, DH:2 * DH] + gb_ref[:, DH:2 * DH])
    f_all = lb_all + (1.0 - lb_all) * jax.nn.sigmoid(proj_ref[:, GW:2 * GW])
    cums = jnp.dot(tri_f, jnp.concatenate([jnp.log2(f_all), lf_all], axis=1), precision=HI,
                   preferred_element_type=F32)
    a_all = cums[:, GW:GW + DH]
    a_all_t = a_all.T
    ig_all_t = ig_all.T
    m_out = jnp.zeros((TILE, DH), F32)
    yield

    for h in range(HEADS):
        hs = slice(h * DH, (h + 1) * DH)

        q = proj_ref[:, 0 * GW + h * DH:0 * GW + (h + 1) * DH]
        v = proj_ref[:, 2 * GW + h * DH:2 * GW + (h + 1) * DH]
        og = proj_ref[:, 3 * GW + h * DH:3 * GW + (h + 1) * DH]
        k = 1.0 - f_all[:, hs]
        b = cums[:, hs]
        b_last = _seg_last(b, seg)
        qe = (q * jnp.exp2(b)).astype(BF16)
        kd = (k * jnp.exp2(b_last - b)).astype(BF16)
        vb16 = v.astype(BF16)
        e_last = jnp.exp2(b_last)
        b_sc[h] = b
        k_sc[h] = k
        yield
        out = yield from _hgrn_intra(q, k, v, b, seg, b_sc.at[h], k_sc.at[h],
                                     proj_ref.at[:, 2 * GW + h * DH:2 * GW + (h + 1) * DH])
        inter = []
        for sg in range(nseg):
            rows = slice(sg * seg, (sg + 1) * seg)
            st = s_in[sg, h]
            inter.append(jnp.dot(qe[rows], st.astype(BF16), preferred_element_type=F32))
            decay = jnp.broadcast_to(e_last[sg * seg:sg * seg + 1], (DH, DH)).T
            s_out[sg, h] = decay * st + lax.dot_general(kd[rows], vb16[rows], TN, preferred_element_type=F32)
        out = out + jnp.concatenate(inter, axis=0)
        out = _rms_mxu(out, hgn_ref[:, hs]) * (og * jax.nn.sigmoid(og))
        o_ref[:, hs] = out
        yield

        q = proj_ref[:, 4 * GW + h * DH:4 * GW + (h + 1) * DH]
        k = proj_ref[:, 5 * GW + h * DH:5 * GW + (h + 1) * DH] * (DH ** -0.5)
        v = proj_ref[:, 6 * GW + h * DH:6 * GW + (h + 1) * DH]
        og = proj_ref[:, 7 * GW + h * DH:7 * GW + (h + 1) * DH]
        a_col = jnp.broadcast_to(a_all[:, h:h + 1], (TILE, TILE))
        i_col = jnp.broadcast_to(ig_all[:, h:h + 1], (TILE, TILE))
        m_col = jnp.broadcast_to(m_rows[:, h:h + 1], (TILE, TILE))
        a_row = a_all_t[h:h + 1]
        i_row = ig_all_t[h:h + 1]
        log_d = jnp.where(tri, a_col - a_row + i_row, -jnp.inf)
        log_inter = a_col + m_col
        m_t = jnp.maximum(log_inter, jnp.max(log_d, axis=-1, keepdims=True))
        d_w = jnp.exp(log_d - m_t)
        w_i = jnp.exp(log_inter - m_t)
        qb16 = q.astype(BF16)
        kb16 = k.astype(BF16)
        vb16 = v.astype(BF16)
        s_w = lax.dot_general(qb16, kb16, NT, preferred_element_type=F32) * d_w
        nd = jnp.dot(s_w.astype(BF16), jnp.concatenate([vb16, jnp.ones((TILE, DH), BF16)], axis=1),
                     preferred_element_type=F32)
        num, den = nd[:, 0:DH], nd[:, DH:2 * DH]
        yield

        m_new = _seg_last(m_t, seg)
        a_end = _seg_last(a_col, seg)
        w_end = jnp.exp(a_end - a_col + i_col - m_new)
        f_end = jnp.exp(a_end + m_col - m_new)
        kw = k * w_end
        kw16 = kw.astype(BF16)
        qc, qn = [], []
        for sg in range(nseg):
            rows = slice(sg * seg, (sg + 1) * seg)
            last = sg * seg + seg - 1
            ct = c_in[sg, h]
            qc.append(jnp.dot(qb16[rows], ct.astype(BF16), preferred_element_type=F32))
            qn.append(q[rows] * n_all[sg:sg + 1, hs])
            fe = f_end[last:last + 1, 0:1]
            c_out[sg, h] = fe * ct + lax.dot_general(kw16[rows], vb16[rows], TN, preferred_element_type=F32)
            put_n(sg, hs, fe * n_all[sg:sg + 1, hs] + jnp.sum(kw[rows], axis=0, keepdims=True))
        num = w_i * jnp.concatenate(qc, axis=0) + num
        qn = jnp.dot(jnp.concatenate(qn, axis=0).astype(BF16), jnp.ones((DH, DH), BF16), preferred_element_type=F32)
        den = w_i * qn + den
        hout = num / jnp.maximum(jnp.abs(den), jnp.exp(-m_t))
        hout = _rms_mxu(hout, mln_ref[:, hs]) * jax.nn.sigmoid(og)
        o_ref[:, GW + h * DH:GW + (h + 1) * DH] = hout
        m_out = jnp.where(lane == h, m_t, m_out)
        yield
    return m_out


def _mixer_seg_kernel(proj_ref, gate_ref, lbl_ref, hgn_ref, mln_ref, gb_ref, s_ref, c_ref, n_ref, m_ref,
                      o_ref, so_ref, co_ref, no_ref, mo_ref, b_sc, k_sc, *, seg):
    def put_n(sg, hs, val):
        no_ref[sg:sg + 1, hs] = val

    m_out = _run(_mixer_core(proj_ref, gate_ref, lbl_ref, hgn_ref, mln_ref, gb_ref, s_ref, c_ref, so_ref, co_ref,
                             n_ref[...], m_ref[...], put_n, o_ref, b_sc, k_sc, seg))
    for sg in range(TILE // seg):
        last = sg * seg + seg - 1
        mo_ref[sg:sg + 1, :] = m_out[last:last + 1]


def _mixer_seg(proj, gates, lb_logits, hg_norm, ml_norm, gate_bias, s0, c0, n0, m0, batch, seq):
    n = batch * seq
    assert TILE % seq == 0 and seq % SUB == 0 and n % TILE == 0
    nseg = TILE // seq
    row = lambda shape: pl.BlockSpec(shape, lambda i: (i, 0))
    st_spec = pl.BlockSpec((nseg, HEADS, DH, DH), lambda i: (i, 0, 0, 0))
    st_shape = jax.ShapeDtypeStruct((batch, HEADS, DH, DH), F32)
    m_rows = jnp.repeat(jnp.pad(m0, ((0, 0), (0, DH - HEADS))), seq, axis=0)
    o, s_new, c_new, n_new, m_new = pl.pallas_call(
        functools.partial(_mixer_seg_kernel, seg=seq),
        grid=(n // TILE,),
        in_specs=[row((TILE, 8 * GW)), row((TILE, 2 * DH)),
                  _resident((2, GW)), _resident((1, GW)), _resident((1, GW)), _resident((1, 2 * DH)),
                  st_spec, st_spec, row((nseg, GW)), row((TILE, DH))],
        out_specs=[row((TILE, 2 * GW)), st_spec, st_spec, row((nseg, GW)), row((nseg, DH))],
        out_shape=[jax.ShapeDtypeStruct((n, 2 * GW), F32), st_shape, st_shape,
                   jax.ShapeDtypeStruct((batch, GW), F32), jax.ShapeDtypeStruct((batch, DH), F32)],
        scratch_shapes=[pltpu.VMEM((HEADS, TILE, DH), F32), pltpu.VMEM((HEADS, TILE, DH), F32)],
        compiler_params=pltpu.CompilerParams(
            dimension_semantics=("arbitrary",), vmem_limit_bytes=VMEM_LIMIT),
        name="mixer_seg",
    )(proj, gates, lb_logits, hg_norm, ml_norm, gate_bias, s0, c0, n0.reshape(batch, GW), m_rows)
    return o, s_new, c_new, n_new.reshape(batch, HEADS, DH), m_new[:, :HEADS]


SEQS_PER_STEP = 4


def _mixer_carry_kernel(proj_ref, gate_ref, lbl_ref, hgn_ref, mln_ref, gb_ref, s_ref, c_ref, n_ref, m_ref,
                        o_ref, so_ref, co_ref, no_ref, mo_ref, b_sc, k_sc):
    @pl.when(pl.program_id(1) == 0)
    def _():
        so_ref[...] = s_ref[...]
        co_ref[...] = c_ref[...]
        no_ref[...] = n_ref[...]
        mo_ref[...] = m_ref[...]

    tiles = []
    for j in range(SEQS_PER_STEP):
        def put_n(sg, hs, val, j=j):
            no_ref[j, :, hs] = val

        state, cell = so_ref.at[j:j + 1], co_ref.at[j:j + 1]
        tiles.append(_mixer_core(proj_ref.at[j], gate_ref.at[j], lbl_ref, hgn_ref, mln_ref, gb_ref,
                                 state, cell, state, cell, no_ref[j], jnp.broadcast_to(mo_ref[j], (TILE, DH)),
                                 put_n, o_ref.at[j], b_sc.at[j], k_sc.at[j], TILE))
    for j, m_out in enumerate(_interleave(*tiles)):
        mo_ref[j] = m_out[TILE - 1:TILE]


def _mixer_carry(proj, gates, lb_logits, hg_norm, ml_norm, gate_bias, s0, c0, n0, m0, batch, seq):
    assert seq % TILE == 0 and batch % SEQS_PER_STEP == 0
    nb = SEQS_PER_STEP
    tok = lambda w: pl.BlockSpec((nb, TILE, w), lambda b, t: (b, t, 0))
    st_spec = pl.BlockSpec((nb, HEADS, DH, DH), lambda b, t: (b, 0, 0, 0))
    n_spec = pl.BlockSpec((nb, 1, GW), lambda b, t: (b, 0, 0))
    m_spec = pl.BlockSpec((nb, 1, DH), lambda b, t: (b, 0, 0))
    st_shape = jax.ShapeDtypeStruct((batch, HEADS, DH, DH), F32)
    o, s_new, c_new, n_new, m_new = pl.pallas_call(
        _mixer_carry_kernel,
        grid=(batch // nb, seq // TILE),
        in_specs=[tok(8 * GW), tok(2 * DH),
                  _resident((2, GW)), _resident((1, GW)), _resident((1, GW)), _resident((1, 2 * DH)),
                  st_spec, st_spec, n_spec, m_spec],
        out_specs=[tok(2 * GW), st_spec, st_spec, n_spec, m_spec],
        out_shape=[jax.ShapeDtypeStruct((batch, seq, 2 * GW), F32), st_shape, st_shape,
                   jax.ShapeDtypeStruct((batch, 1, GW), F32), jax.ShapeDtypeStruct((batch, 1, DH), F32)],
        scratch_shapes=[pltpu.VMEM((nb, HEADS, TILE, DH), F32), pltpu.VMEM((nb, HEADS, TILE, DH), F32)],
        compiler_params=pltpu.CompilerParams(
            dimension_semantics=("arbitrary", "arbitrary"), vmem_limit_bytes=VMEM_LIMIT),
        name="mixer_carry",
    )(proj.reshape(batch, seq, 8 * GW), gates.reshape(batch, seq, 2 * DH), lb_logits, hg_norm, ml_norm, gate_bias,
      s0, c0, n0.reshape(batch, 1, GW), jnp.pad(m0, ((0, 0), (0, DH - HEADS))).reshape(batch, 1, DH))
    return o, s_new, c_new, n_new.reshape(batch, HEADS, DH), m_new.reshape(batch, DH)[:, :HEADS]


def _ffn_core(x, o, wo_ref, g2_ref, wg_ref, wv_ref, cw_ref, cb_ref, wd_ref, gf_ref, hist_ref, nb_ref, u_sc, g_sc, bb, tt):
    m = bb * tt
    x1 = x + _bdot(o, wo_ref[...])
    h2 = _rms(x1, g2_ref[...]).astype(BF16)
    tpos = lax.broadcasted_iota(jnp.int32, (m, 1), 0) & (tt - 1)
    for c in range(0, D_FF, FF_CHUNK):
        yield
        cs = slice(c, c + FF_CHUNK)
        u = jnp.dot(h2, wg_ref[:, cs], preferred_element_type=F32)
        val = jnp.dot(h2, wv_ref[:, cs], preferred_element_type=F32)
        u_sc[SUB:SUB + m, cs] = u
        u1 = u_sc[SUB - 1:SUB - 1 + m, cs]
        u2 = u_sc[SUB - 2:SUB - 2 + m, cs]
        if bb > 1:
            rows = lambda j0: jnp.concatenate(
                [jnp.broadcast_to(hist_ref[j:j + 1, j0 + c:j0 + c + FF_CHUNK], (tt, FF_CHUNK)) for j in range(bb)],
                axis=0)
            older, newer = rows(0), rows(D_FF)
            u1 = jnp.where(tpos == 0, newer, u1)
            u2 = jnp.where(tpos == 0, older, jnp.where(tpos == 1, newer, u2))
            last = u.reshape(bb, tt, FF_CHUNK)
            for j in range(CONV_W - 1):
                nb_ref[:, j, cs] = last[:, tt - (CONV_W - 1) + j, :]
        conv = cb_ref[:, cs] + cw_ref[0:1, cs] * u2 + cw_ref[1:2, cs] * u1 + cw_ref[2:3, cs] * u
        g_sc[:, cs] = (jax.nn.gelu(conv) * val).astype(BF16)
    if bb == 1:
        u_sc[SUB - 2:SUB, :] = u_sc[SUB + m - 2:SUB + m, :]
    yield
    y = x1 + jnp.dot(g_sc[...], wd_ref[...], preferred_element_type=F32)
    return _rms(y, gf_ref[...])


def _ffn_kernel(x_ref, o_ref, buf_ref, wo_ref, g2_ref, wg_ref, wv_ref, cw_ref, cb_ref, wd_ref, gf_ref,
                y_ref, nb_ref, u_sc, g_sc, *, bb, tt):
    m = bb * tt
    if bb > 1:
        u_sc[0:SUB, :] = jnp.zeros((SUB, D_FF), F32)
    else:
        @pl.when(pl.program_id(1) == 0)
        def _():
            for j in range(CONV_W - 1):
                u_sc[SUB - (CONV_W - 1) + j:SUB - (CONV_W - 1) + j + 1, :] = buf_ref[0, :, j * D_FF:(j + 1) * D_FF]

    y = _run(_ffn_core(x_ref[...].reshape(m, D_MODEL), o_ref[...].reshape(m, D_MODEL), wo_ref, g2_ref, wg_ref,
                       wv_ref, cw_ref, cb_ref, wd_ref, gf_ref, buf_ref, nb_ref, u_sc, g_sc, bb, tt))
    if bb == 1:
        nb_ref[0] = u_sc[SUB - (CONV_W - 1):SUB, :]
    y_ref[...] = y.reshape(bb, tt, D_MODEL)


def _ffn(x, o, buf, wo, g2, wg, wv, cw, cb, wd, gf, bb, tt):
    batch, seq, _ = x.shape
    assert batch % bb == 0 and seq % tt == 0 and tt % SUB == 0 and tt >= CONV_W - 1
    kern = functools.partial(_ffn_kernel, bb=bb, tt=tt)
    tok = pl.BlockSpec((bb, tt, D_MODEL), lambda b, t: (b, t, 0))
    assert tt & (tt - 1) == 0 and (bb == 1 or seq == tt)
    hw = (CONV_W - 1) * D_FF
    if bb == 1:
        hist, hist_shape = pl.BlockSpec((1, 1, hw), lambda b, t: (b, 0, 0)), (batch, 1, hw)
    else:
        hist, hist_shape = pl.BlockSpec((bb, hw), lambda b, t: (b, 0)), (batch, hw)
    hist_out = pl.BlockSpec((bb, CONV_W - 1, D_FF), lambda b, t: (b, 0, 0))
    return pl.pallas_call(
        kern,
        grid=(batch // bb, seq // tt),
        in_specs=[tok, tok, hist,
                  _resident((D_MODEL, D_MODEL)), _resident((1, D_MODEL)),
                  _resident((D_MODEL, D_FF)), _resident((D_MODEL, D_FF)),
                  _resident((CONV_W, D_FF)), _resident((1, D_FF)),
                  _resident((D_FF, D_MODEL)), _resident((1, D_MODEL))],
        out_specs=[tok, hist_out],
        out_shape=[jax.ShapeDtypeStruct((batch, seq, D_MODEL), F32),
                   jax.ShapeDtypeStruct((batch, CONV_W - 1, D_FF), F32)],
        scratch_shapes=[pltpu.VMEM((SUB + bb * tt, D_FF), F32), pltpu.VMEM((bb * tt, D_FF), BF16)],
        compiler_params=pltpu.CompilerParams(
            dimension_semantics=("arbitrary", "arbitrary"), vmem_limit_bytes=VMEM_LIMIT),
        name="ffn",
    )(x, o, buf.reshape(hist_shape), wo, g2, wg, wv, cw, cb, wd, gf)


ROWS_PER_STEP = 512


def _layer(x, s0, c0, n0, m0, buf0, p):
    batch, seq, _ = x.shape
    n = batch * seq
    proj, gates = _inproj(x.reshape(n, D_MODEL), p["g1"], p["w_all"], p["w_gate2"], ROWS_PER_STEP)
    mixer = _mixer_carry if seq >= TILE else _mixer_seg
    o, s_new, c_new, n_new, m_new = mixer(proj, gates, p["lb_logits"], p["hg_norm"], p["ml_norm"],
                                          p["gate_bias"], s0, c0, n0, m0, batch, seq)
    tt = min(seq, ROWS_PER_STEP)
    bb = ROWS_PER_STEP // tt
    y, buf_new = _ffn(x, o.reshape(batch, seq, D_MODEL), buf0, p["wo"], p["g2"], p["wg"], p["wv"],
                      p["cw"], p["cb"], p["wd"], p["gf"], bb, tt)
    return y, s_new[None], c_new[None], n_new[None], m_new[None], buf_new[None]


def kernel(x_prompt, x_sample, state_hgrn_S, state_mlstm_C, state_mlstm_n, state_mlstm_m, state_conv, norm1_g, w_in, hg_lb_logits, hg_norm_g, ml_b_ig, ml_b_fg, ml_norm_g, w_out, norm2_g, w_gate, w_val, conv_w, conv_b, w_down, final_norm_g):
    assert norm1_g.shape[0] == 1, "single-layer trunk"
    w = w_in[0]
    gate_cols = w[:, 8 * GW:]
    zpad = jnp.zeros((D_MODEL, DH - HEADS), w.dtype)
    w_gate2 = jnp.concatenate([gate_cols[:, :HEADS], zpad, gate_cols[:, HEADS:], zpad], axis=1)
    bpad = jnp.zeros((DH - HEADS,), F32)
    p = {
        "g1": norm1_g, "w_all": _to_bf16(w, 128), "w_gate2": w_gate2.astype(BF16),
        "lb_logits": hg_lb_logits, "hg_norm": hg_norm_g, "ml_norm": ml_norm_g,
        "gate_bias": jnp.concatenate([ml_b_ig[0], bpad, ml_b_fg[0], bpad])[None],
        "wo": w_out[0].astype(BF16), "g2": norm2_g, "wg": w_gate[0].astype(BF16),
        "wv": w_val[0].astype(BF16), "cw": conv_w[0], "cb": conv_b, "wd": w_down[0].astype(BF16),
        "gf": final_norm_g[None],
    }
    b = x_prompt.shape[0]
    zs = jnp.zeros((b, HEADS, DH, DH), F32)
    prompt = _layer(x_prompt, zs, zs, jnp.zeros((b, HEADS, DH), F32), jnp.zeros((b, HEADS), F32),
                    jnp.zeros((b, CONV_W - 1, D_FF), F32), p)
    sample = _layer(x_sample, state_hgrn_S[0], state_mlstm_C[0], state_mlstm_n[0], state_mlstm_m[0],
                    state_conv[0], p)
    out = []
    for a, c in zip(prompt, sample):
        out += [a, c]
    return tuple(out)
```

```python
import functools

import jax
import jax.numpy as jnp
from jax import lax
from jax.experimental import pallas as pl
from jax.experimental.pallas import tpu as pltpu

F32 = jnp.float32
BF16 = jnp.bfloat16
HI = lax.Precision.HIGHEST
NT = (((1,), (1,)), ((), ()))
TN = (((0,), (0,)), ((), ()))

D_MODEL = 1024
HEADS = 4
DH = 128
GW = HEADS * DH
D_FF = 2816
CONV_W = 3
EPS = 1e-6
TILE = 128
SUB = 8
FF_CHUNK = 256
VMEM_LIMIT = 56 * 1024 * 1024


def _rms(x, g):
    return x * lax.rsqrt(jnp.mean(x * x, axis=-1, keepdims=True) + EPS) * g


def _rms_mxu(x, g):
    ms = jnp.dot((x * x).astype(BF16), jnp.full((DH, DH), 1.0 / DH, BF16), preferred_element_type=F32)
    return x * lax.rsqrt(ms + EPS) * g


def _bdot(a, b):
    return jnp.dot(a.astype(BF16), b.astype(BF16), preferred_element_type=F32)


def _resident(shape):
    zeros = (0,) * len(shape)
    return pl.BlockSpec(shape, lambda *_: zeros, pipeline_mode=pl.Buffered(1))


def _run(gen):
    try:
        while True:
            next(gen)
    except StopIteration as stop:
        return stop.value


def _interleave(primary, *others):
    gens = [primary, *others]
    done = [False] * len(gens)
    vals = [None] * len(gens)

    def step(j):
        try:
            next(gens[j])
        except StopIteration as stop:
            done[j], vals[j] = True, stop.value

    turn = 0
    while not all(done):
        if not done[0]:
            step(0)
        pending = [j for j in range(1, len(gens)) if not done[j]]
        if pending:
            step(pending[turn % len(pending)])
            turn += 1
    return vals


def _inproj_core(x, g_ref, w_ref, wg_ref, proj_ref, gate_ref):
    hb = _rms(x, g_ref[...]).astype(BF16)
    gate_ref[...] = jnp.dot(hb, wg_ref[...], preferred_element_type=F32)
    for c in range(0, 8 * GW, GW):
        yield
        proj_ref[:, c:c + GW] = jnp.dot(hb, w_ref[:, c:c + GW], preferred_element_type=F32)


def _inproj_kernel(x_ref, g_ref, w_ref, wg_ref, proj_ref, gate_ref):
    _run(_inproj_core(x_ref[...], g_ref, w_ref, wg_ref, proj_ref, gate_ref))


IN_COLS = 8 * GW + 2 * HEADS


def _inproj(x2, g1, w_all, w_gate2, tm):
    n = x2.shape[0]
    return pl.pallas_call(
        _inproj_kernel,
        grid=(n // tm,),
        in_specs=[
            pl.BlockSpec((tm, D_MODEL), lambda i: (i, 0)),
            _resident((1, D_MODEL)),
            _resident((D_MODEL, IN_COLS)),
            _resident((D_MODEL, 2 * DH)),
        ],
        out_specs=[
            pl.BlockSpec((tm, 8 * GW), lambda i: (i, 0)),
            pl.BlockSpec((tm, 2 * DH), lambda i: (i, 0)),
        ],
        out_shape=[
            jax.ShapeDtypeStruct((n, 8 * GW), F32),
            jax.ShapeDtypeStruct((n, 2 * DH), F32),
        ],
        compiler_params=pltpu.CompilerParams(
            dimension_semantics=("arbitrary",), vmem_limit_bytes=VMEM_LIMIT),
        name="inproj",
    )(x2, g1, w_all, w_gate2)


def _seg_last(x, seg):
    nseg = TILE // seg
    w = x.shape[-1]
    if nseg == 1:
        last = jnp.broadcast_to(x[TILE - SUB:TILE][SUB - 1:SUB], (SUB, w))
        return jnp.concatenate([last] * (TILE // SUB), axis=0)
    y = x.reshape(nseg, seg, w)[:, seg - 1:seg, :]
    return jnp.broadcast_to(y, (nseg, seg, w)).reshape(TILE, w)


def _hgrn_intra(q, k, v, b, seg, b_ref, k_ref, v_ref):
    sub = lax.broadcasted_iota(jnp.int32, (SUB, 1), 0)
    blocks = []
    for r0 in range(0, TILE, SUB):
        bb = b[r0:r0 + SUB]
        qb = q[r0:r0 + SUB]
        acc = jnp.zeros((SUB, DH), F32)
        for s in range(SUB):
            bs = jnp.broadcast_to(b_ref[r0 + s:r0 + s + 1], (SUB, DH))
            ks = jnp.broadcast_to(k_ref[r0 + s:r0 + s + 1], (SUB, DH))
            vs = jnp.broadcast_to(v_ref[r0 + s:r0 + s + 1], (SUB, DH))
            p = jnp.exp2(bb - bs) * (qb * ks)
            col = jnp.sum(p, axis=-1, keepdims=True)
            col = jnp.where(sub >= s, col, 0.0)
            acc = acc + col * vs
        blocks.append(acc)
        if r0 % (4 * SUB) == 3 * SUB:
            yield
    intra = jnp.concatenate(blocks, axis=0)

    if seg > SUB:
        ri = lax.broadcasted_iota(jnp.int32, (TILE, TILE), 0)
        ci = lax.broadcasted_iota(jnp.int32, (TILE, TILE), 1)
        a_off = jnp.zeros((TILE, TILE), F32)
        w = SUB
        while w < seg:
            zeros = jnp.zeros((w, DH), F32)
            qs, ks = [], []
            for r0 in range(0, TILE, 2 * w):
                ref = b[r0 + w - 1:r0 + w]
                ks += [k[r0:r0 + w] * jnp.exp2(ref - b[r0:r0 + w]), zeros]
                qs += [zeros, q[r0 + w:r0 + 2 * w] * jnp.exp2(b[r0 + w:r0 + 2 * w] - ref)]
            qt = jnp.concatenate(qs, axis=0).astype(BF16)
            kt = jnp.concatenate(ks, axis=0).astype(BF16)
            a_lvl = lax.dot_general(qt, kt, NT, preferred_element_type=F32)
            same_block = (ri // (2 * w)) == (ci // (2 * w))
            a_off = a_off + jnp.where(same_block, a_lvl, 0.0)
            w *= 2
        yield
        intra = intra + _bdot(a_off, v)
    return intra


def _mixer_core(proj_ref, gate_ref, lbl_ref, hgn_ref, mln_ref, gb_ref,
                s_in, c_in, s_out, c_out, n_all, m_rows, put_n, o_ref, b_sc, k_sc, seg):
    nseg = TILE // seg
    ri = lax.broadcasted_iota(jnp.int32, (TILE, TILE), 0)
    ci = lax.broadcasted_iota(jnp.int32, (TILE, TILE), 1)
    tri = ((ri // seg) == (ci // seg)) & (ci <= ri)
    tri_f = tri.astype(F32)
    lane = lax.broadcasted_iota(jnp.int32, (TILE, DH), 1)

    lg = lbl_ref[...]
    ex = jnp.exp(lg - jnp.max(lg, axis=0, keepdims=True))
    lb_all = ex[0:1] / jnp.sum(ex, axis=0, keepdims=True)

    ig_all = gate_ref[:, 0:DH] + gb_ref[:, 0:DH]
    lf_all = jax.nn.log_sigmoid(gate_ref[:, DH:2 * DH] + gb_ref[:, DH:2 * DH])
    f_all = lb_all + (1.0 - lb_all) * jax.nn.sigmoid(proj_ref[:, GW:2 * GW])
    cums = jnp.dot(tri_f, jnp.concatenate([jnp.log2(f_all), lf_all], axis=1), precision=HI,
                   preferred_element_type=F32)
    a_all = cums[:, GW:GW + DH]
    a_all_t = a_all.T
    ig_all_t = ig_all.T
    m_out = jnp.zeros((TILE, DH), F32)
    yield

    for h in range(HEADS):
        hs = slice(h * DH, (h + 1) * DH)

        q = proj_ref[:, 0 * GW + h * DH:0 * GW + (h + 1) * DH]
        v = proj_ref[:, 2 * GW + h * DH:2 * GW + (h + 1) * DH]
        og = proj_ref[:, 3 * GW + h * DH:3 * GW + (h + 1) * DH]
        k = 1.0 - f_all[:, hs]
        b = cums[:, hs]
        b_last = _seg_last(b, seg)
        qe = (q * jnp.exp2(b)).astype(BF16)
        kd = (k * jnp.exp2(b_last - b)).astype(BF16)
        vb16 = v.astype(BF16)
        e_last = jnp.exp2(b_last)
        b_sc[h] = b
        k_sc[h] = k
        yield
        out = yield from _hgrn_intra(q, k, v, b, seg, b_sc.at[h], k_sc.at[h],
                                     proj_ref.at[:, 2 * GW + h * DH:2 * GW + (h + 1) * DH])
        inter = []
        for sg in range(nseg):
            rows = slice(sg * seg, (sg + 1) * seg)
            st = s_in[sg, h]
            inter.append(jnp.dot(qe[rows], st.astype(BF16), preferred_element_type=F32))
            decay = jnp.broadcast_to(e_last[sg * seg:sg * seg + 1], (DH, DH)).T
            s_out[sg, h] = decay * st + lax.dot_general(kd[rows], vb16[rows], TN, preferred_element_type=F32)
        out = out + jnp.concatenate(inter, axis=0)
        out = _rms_mxu(out, hgn_ref[:, hs]) * (og * jax.nn.sigmoid(og))
        o_ref[:, hs] = out
        yield

        q = proj_ref[:, 4 * GW + h * DH:4 * GW + (h + 1) * DH]
        k = proj_ref[:, 5 * GW + h * DH:5 * GW + (h + 1) * DH] * (DH ** -0.5)
        v = proj_ref[:, 6 * GW + h * DH:6 * GW + (h + 1) * DH]
        og = proj_ref[:, 7 * GW + h * DH:7 * GW + (h + 1) * DH]
        a_col = jnp.broadcast_to(a_all[:, h:h + 1], (TILE, TILE))
        i_col = jnp.broadcast_to(ig_all[:, h:h + 1], (TILE, TILE))
        m_col = jnp.broadcast_to(m_rows[:, h:h + 1], (TILE, TILE))
        a_row = a_all_t[h:h + 1]
        i_row = ig_all_t[h:h + 1]
        log_d = jnp.where(tri, a_col - a_row + i_row, -jnp.inf)
        log_inter = a_col + m_col
        m_t = jnp.maximum(log_inter, jnp.max(log_d, axis=-1, keepdims=True))
        d_w = jnp.exp(log_d - m_t)
        w_i = jnp.exp(log_inter - m_t)
        qb16 = q.astype(BF16)
        kb16 = k.astype(BF16)
        vb16 = v.astype(BF16)
        s_w = lax.dot_general(qb16, kb16, NT, preferred_element_type=F32) * d_w
        nd = jnp.dot(s_w.astype(BF16), jnp.concatenate([vb16, jnp.ones((TILE, DH), BF16)], axis=1),
                     preferred_element_type=F32)
        num, den = nd[:, 0:DH], nd[:, DH:2 * DH]
        yield

        m_new = _seg_last(m_t, seg)
        a_end = _seg_last(a_col, seg)
        w_end = jnp.exp(a_end - a_col + i_col - m_new)
        f_end = jnp.exp(a_end + m_col - m_new)
        kw = k * w_end
        kw16 = kw.astype(BF16)
        qc, qn = [], []
        for sg in range(nseg):
            rows = slice(sg * seg, (sg + 1) * seg)
            last = sg * seg + seg - 1
            ct = c_in[sg, h]
            qc.append(jnp.dot(qb16[rows], ct.astype(BF16), preferred_element_type=F32))
            qn.append(q[rows] * n_all[sg:sg + 1, hs])
            fe = f_end[last:last + 1, 0:1]
            c_out[sg, h] = fe * ct + lax.dot_general(kw16[rows], vb16[rows], TN, preferred_element_type=F32)
            put_n(sg, hs, fe * n_all[sg:sg + 1, hs] + jnp.sum(kw[rows], axis=0, keepdims=True))
        num = w_i * jnp.concatenate(qc, axis=0) + num
        qn = jnp.dot(jnp.concatenate(qn, axis=0).astype(BF16), jnp.ones((DH, DH), BF16), preferred_element_type=F32)
        den = w_i * qn + den
        hout = num / jnp.maximum(jnp.abs(den), jnp.exp(-m_t))
        hout = _rms_mxu(hout, mln_ref[:, hs]) * jax.nn.sigmoid(og)
        o_ref[:, GW + h * DH:GW + (h + 1) * DH] = hout
        m_out = jnp.where(lane == h, m_t, m_out)
        yield
    return m_out


def _mixer_seg_kernel(proj_ref, gate_ref, lbl_ref, hgn_ref, mln_ref, gb_ref, s_ref, c_ref, n_ref, m_ref,
                      o_ref, so_ref, co_ref, no_ref, mo_ref, b_sc, k_sc, *, seg):
    def put_n(sg, hs, val):
        no_ref[sg:sg + 1, hs] = val

    m_out = _run(_mixer_core(proj_ref, gate_ref, lbl_ref, hgn_ref, mln_ref, gb_ref, s_ref, c_ref, so_ref, co_ref,
                             n_ref[...], m_ref[...], put_n, o_ref, b_sc, k_sc, seg))
    for sg in range(TILE // seg):
        last = sg * seg + seg - 1
        mo_ref[sg:sg + 1, :] = m_out[last:last + 1]


def _mixer_seg(proj, gates, lb_logits, hg_norm, ml_norm, gate_bias, s0, c0, n0, m0, batch, seq):
    n = batch * seq
    assert TILE % seq == 0 and seq % SUB == 0 and n % TILE == 0
    nseg = TILE // seq
    row = lambda shape: pl.BlockSpec(shape, lambda i: (i, 0))
    st_spec = pl.BlockSpec((nseg, HEADS, DH, DH), lambda i: (i, 0, 0, 0))
    st_shape = jax.ShapeDtypeStruct((batch, HEADS, DH, DH), F32)
    m_rows = jnp.repeat(jnp.pad(m0, ((0, 0), (0, DH - HEADS))), seq, axis=0)
    o, s_new, c_new, n_new, m_new = pl.pallas_call(
        functools.partial(_mixer_seg_kernel, seg=seq),
        grid=(n // TILE,),
        in_specs=[row((TILE, 8 * GW)), row((TILE, 2 * DH)),
                  _resident((2, GW)), _resident((1, GW)), _resident((1, GW)), _resident((1, 2 * DH)),
                  st_spec, st_spec, row((nseg, GW)), row((TILE, DH))],
        out_specs=[row((TILE, 2 * GW)), st_spec, st_spec, row((nseg, GW)), row((nseg, DH))],
        out_shape=[jax.ShapeDtypeStruct((n, 2 * GW), F32), st_shape, st_shape,
                   jax.ShapeDtypeStruct((batch, GW), F32), jax.ShapeDtypeStruct((batch, DH), F32)],
        scratch_shapes=[pltpu.VMEM((HEADS, TILE, DH), F32), pltpu.VMEM((HEADS, TILE, DH), F32)],
        compiler_params=pltpu.CompilerParams(
            dimension_semantics=("arbitrary",), vmem_limit_bytes=VMEM_LIMIT),
        name="mixer_seg",
    )(proj, gates, lb_logits, hg_norm, ml_norm, gate_bias, s0, c0, n0.reshape(batch, GW), m_rows)
    return o, s_new, c_new, n_new.reshape(batch, HEADS, DH), m_new[:, :HEADS]


SEQS_PER_STEP = 4


def _mixer_carry_kernel(proj_ref, gate_ref, lbl_ref, hgn_ref, mln_ref, gb_ref, s_ref, c_ref, n_ref, m_ref,
                        o_ref, so_ref, co_ref, no_ref, mo_ref, b_sc, k_sc):
    @pl.when(pl.program_id(1) == 0)
    def _():
        so_ref[...] = s_ref[...]
        co_ref[...] = c_ref[...]
        no_ref[...] = n_ref[...]
        mo_ref[...] = m_ref[...]

    tiles = []
    for j in range(SEQS_PER_STEP):
        def put_n(sg, hs, val, j=j):
            no_ref[j, :, hs] = val

        state, cell = so_ref.at[j:j + 1], co_ref.at[j:j + 1]
        tiles.append(_mixer_core(proj_ref.at[j], gate_ref.at[j], lbl_ref, hgn_ref, mln_ref, gb_ref,
                                 state, cell, state, cell, no_ref[j], jnp.broadcast_to(mo_ref[j], (TILE, DH)),
                                 put_n, o_ref.at[j], b_sc.at[j], k_sc.at[j], TILE))
    for j, m_out in enumerate(_interleave(*tiles)):
        mo_ref[j] = m_out[TILE - 1:TILE]


def _mixer_carry(proj, gates, lb_logits, hg_norm, ml_norm, gate_bias, s0, c0, n0, m0, batch, seq):
    assert seq % TILE == 0 and batch % SEQS_PER_STEP == 0
    nb = SEQS_PER_STEP
    tok = lambda w: pl.BlockSpec((nb, TILE, w), lambda b, t: (b, t, 0))
    st_spec = pl.BlockSpec((nb, HEADS, DH, DH), lambda b, t: (b, 0, 0, 0))
    n_spec = pl.BlockSpec((nb, 1, GW), lambda b, t: (b, 0, 0))
    m_spec = pl.BlockSpec((nb, 1, DH), lambda b, t: (b, 0, 0))
    st_shape = jax.ShapeDtypeStruct((batch, HEADS, DH, DH), F32)
    o, s_new, c_new, n_new, m_new = pl.pallas_call(
        _mixer_carry_kernel,
        grid=(batch // nb, seq // TILE),
        in_specs=[tok(8 * GW), tok(2 * DH),
                  _resident((2, GW)), _resident((1, GW)), _resident((1, GW)), _resident((1, 2 * DH)),
                  st_spec, st_spec, n_spec, m_spec],
        out_specs=[tok(2 * GW), st_spec, st_spec, n_spec, m_spec],
        out_shape=[jax.ShapeDtypeStruct((batch, seq, 2 * GW), F32), st_shape, st_shape,
                   jax.ShapeDtypeStruct((batch, 1, GW), F32), jax.ShapeDtypeStruct((batch, 1, DH), F32)],
        scratch_shapes=[pltpu.VMEM((nb, HEADS, TILE, DH), F32), pltpu.VMEM((nb, HEADS, TILE, DH), F32)],
        compiler_params=pltpu.CompilerParams(
            dimension_semantics=("arbitrary", "arbitrary"), vmem_limit_bytes=VMEM_LIMIT),
        name="mixer_carry",
    )(proj.reshape(batch, seq, 8 * GW), gates.reshape(batch, seq, 2 * DH), lb_logits, hg_norm, ml_norm, gate_bias,
      s0, c0, n0.reshape(batch, 1, GW), jnp.pad(m0, ((0, 0), (0, DH - HEADS))).reshape(batch, 1, DH))
    return o, s_new, c_new, n_new.reshape(batch, HEADS, DH), m_new.reshape(batch, DH)[:, :HEADS]


def _ffn_core(x, o, wo_ref, g2_ref, wg_ref, wv_ref, cw_ref, cb_ref, wd_ref, gf_ref, hist_ref, nb_ref, u_sc, g_sc, bb, tt):
    m = bb * tt
    x1 = x + _bdot(o, wo_ref[...])
    h2 = _rms(x1, g2_ref[...]).astype(BF16)
    tpos = lax.broadcasted_iota(jnp.int32, (m, 1), 0) & (tt - 1)
    for c in range(0, D_FF, FF_CHUNK):
        yield
        cs = slice(c, c + FF_CHUNK)
        u = jnp.dot(h2, wg_ref[:, cs], preferred_element_type=F32)
        val = jnp.dot(h2, wv_ref[:, cs], preferred_element_type=F32)
        u_sc[SUB:SUB + m, cs] = u
        u1 = u_sc[SUB - 1:SUB - 1 + m, cs]
        u2 = u_sc[SUB - 2:SUB - 2 + m, cs]
        if bb > 1:
            rows = lambda j0: jnp.concatenate(
                [jnp.broadcast_to(hist_ref[j:j + 1, j0 + c:j0 + c + FF_CHUNK], (tt, FF_CHUNK)) for j in range(bb)],
                axis=0)
            older, newer = rows(0), rows(D_FF)
            u1 = jnp.where(tpos == 0, newer, u1)
            u2 = jnp.where(tpos == 0, older, jnp.where(tpos == 1, newer, u2))
            last = u.reshape(bb, tt, FF_CHUNK)
            for j in range(CONV_W - 1):
                nb_ref[:, j, cs] = last[:, tt - (CONV_W - 1) + j, :]
        conv = cb_ref[:, cs] + cw_ref[0:1, cs] * u2 + cw_ref[1:2, cs] * u1 + cw_ref[2:3, cs] * u
        g_sc[:, cs] = (jax.nn.gelu(conv) * val).astype(BF16)
    if bb == 1:
        u_sc[SUB - 2:SUB, :] = u_sc[SUB + m - 2:SUB + m, :]
    yield
    y = x1 + jnp.dot(g_sc[...], wd_ref[...], preferred_element_type=F32)
    return _rms(y, gf_ref[...])


def _ffn_kernel(x_ref, o_ref, buf_ref, wo_ref, g2_ref, wg_ref, wv_ref, cw_ref, cb_ref, wd_ref, gf_ref,
                y_ref, nb_ref, u_sc, g_sc, *, bb, tt):
    m = bb * tt
    if bb > 1:
        u_sc[0:SUB, :] = jnp.zeros((SUB, D_FF), F32)
    else:
        @pl.when(pl.program_id(1) == 0)
        def _():
            for j in range(CONV_W - 1):
                u_sc[SUB - (CONV_W - 1) + j:SUB - (CONV_W - 1) + j + 1, :] = buf_ref[0, :, j * D_FF:(j + 1) * D_FF]

    y = _run(_ffn_core(x_ref[...].reshape(m, D_MODEL), o_ref[...].reshape(m, D_MODEL), wo_ref, g2_ref, wg_ref,
                       wv_ref, cw_ref, cb_ref, wd_ref, gf_ref, buf_ref, nb_ref, u_sc, g_sc, bb, tt))
    if bb == 1:
        nb_ref[0] = u_sc[SUB - (CONV_W - 1):SUB, :]
    y_ref[...] = y.reshape(bb, tt, D_MODEL)


def _ffn(x, o, buf, wo, g2, wg, wv, cw, cb, wd, gf, bb, tt):
    batch, seq, _ = x.shape
    assert batch % bb == 0 and seq % tt == 0 and tt % SUB == 0 and tt >= CONV_W - 1
    kern = functools.partial(_ffn_kernel, bb=bb, tt=tt)
    tok = pl.BlockSpec((bb, tt, D_MODEL), lambda b, t: (b, t, 0))
    assert tt & (tt - 1) == 0 and (bb == 1 or seq == tt)
    hw = (CONV_W - 1) * D_FF
    if bb == 1:
        hist, hist_shape = pl.BlockSpec((1, 1, hw), lambda b, t: (b, 0, 0)), (batch, 1, hw)
    else:
        hist, hist_shape = pl.BlockSpec((bb, hw), lambda b, t: (b, 0)), (batch, hw)
    hist_out = pl.BlockSpec((bb, CONV_W - 1, D_FF), lambda b, t: (b, 0, 0))
    return pl.pallas_call(
        kern,
        grid=(batch // bb, seq // tt),
        in_specs=[tok, tok, hist,
                  _resident((D_MODEL, D_MODEL)), _resident((1, D_MODEL)),
                  _resident((D_MODEL, D_FF)), _resident((D_MODEL, D_FF)),
                  _resident((CONV_W, D_FF)), _resident((1, D_FF)),
                  _resident((D_FF, D_MODEL)), _resident((1, D_MODEL))],
        out_specs=[tok, hist_out],
        out_shape=[jax.ShapeDtypeStruct((batch, seq, D_MODEL), F32),
                   jax.ShapeDtypeStruct((batch, CONV_W - 1, D_FF), F32)],
        scratch_shapes=[pltpu.VMEM((SUB + bb * tt, D_FF), F32), pltpu.VMEM((bb * tt, D_FF), BF16)],
        compiler_params=pltpu.CompilerParams(
            dimension_semantics=("arbitrary", "arbitrary"), vmem_limit_bytes=VMEM_LIMIT),
        name="ffn",
    )(x, o, buf.reshape(hist_shape), wo, g2, wg, wv, cw, cb, wd, gf)


ROWS_PER_STEP = 512


def _layer(x, s0, c0, n0, m0, buf0, p):
    batch, seq, _ = x.shape
    n = batch * seq
    proj, gates = _inproj(x.reshape(n, D_MODEL), p["g1"], p["w_all"], p["w_gate2"], ROWS_PER_STEP)
    mixer = _mixer_carry if seq >= TILE else _mixer_seg
    o, s_new, c_new, n_new, m_new = mixer(proj, gates, p["lb_logits"], p["hg_norm"], p["ml_norm"],
                                          p["gate_bias"], s0, c0, n0, m0, batch, seq)
    tt = min(seq, ROWS_PER_STEP)
    bb = ROWS_PER_STEP // tt
    y, buf_new = _ffn(x, o.reshape(batch, seq, D_MODEL), buf0, p["wo"], p["g2"], p["wg"], p["wv"],
                      p["cw"], p["cb"], p["wd"], p["gf"], bb, tt)
    return y, s_new[None], c_new[None], n_new[None], m_new[None], buf_new[None]


def kernel(x_prompt, x_sample, state_hgrn_S, state_mlstm_C, state_mlstm_n, state_mlstm_m, state_conv, norm1_g, w_in, hg_lb_logits, hg_norm_g, ml_b_ig, ml_b_fg, ml_norm_g, w_out, norm2_g, w_gate, w_val, conv_w, conv_b, w_down, final_norm_g):
    assert norm1_g.shape[0] == 1, "single-layer trunk"
    w = w_in[0]
    gate_cols = w[:, 8 * GW:]
    zpad = jnp.zeros((D_MODEL, DH - HEADS), w.dtype)
    w_gate2 = jnp.concatenate([gate_cols[:, :HEADS], zpad, gate_cols[:, HEADS:], zpad], axis=1)
    bpad = jnp.zeros((DH - HEADS,), F32)
    p = {
        "g1": norm1_g, "w_all": w.astype(BF16), "w_gate2": w_gate2.astype(BF16),
        "lb_logits": hg_lb_logits, "hg_norm": hg_norm_g, "ml_norm": ml_norm_g,
        "gate_bias": jnp.concatenate([ml_b_ig[0], bpad, ml_b_fg[0], bpad])[None],
        "wo": w_out[0].astype(BF16), "g2": norm2_g, "wg": w_gate[0].astype(BF16),
        "wv": w_val[0].astype(BF16), "cw": conv_w[0], "cb": conv_b, "wd": w_down[0].astype(BF16),
        "gf": final_norm_g[None],
    }
    b = x_prompt.shape[0]
    zs = jnp.zeros((b, HEADS, DH, DH), F32)
    prompt = _layer(x_prompt, zs, zs, jnp.zeros((b, HEADS, DH), F32), jnp.zeros((b, HEADS), F32),
                    jnp.zeros((b, CONV_W - 1, D_FF), F32), p)
    sample = _layer(x_sample, state_hgrn_S[0], state_mlstm_C[0], state_mlstm_n[0], state_mlstm_m[0],
                    state_conv[0], p)
    out = []
    for a, c in zip(prompt, sample):
        out += [a, c]
    return tuple(out)
```

```python
import functools

import jax
import jax.numpy as jnp
from jax import lax
from jax.experimental import pallas as pl
from jax.experimental.pallas import tpu as pltpu

F32 = jnp.float32
BF16 = jnp.bfloat16
HI = lax.Precision.HIGHEST
NT = (((1,), (1,)), ((), ()))
TN = (((0,), (0,)), ((), ()))

D_MODEL = 1024
HEADS = 4
DH = 128
GW = HEADS * DH
D_FF = 2816
CONV_W = 3
EPS = 1e-6
TILE = 128
SUB = 8
FF_CHUNK = 256
VMEM_LIMIT = 56 * 1024 * 1024


def _rms(x, g):
    return x * lax.rsqrt(jnp.mean(x * x, axis=-1, keepdims=True) + EPS) * g


def _rms_mxu(x, g):
    ms = jnp.dot((x * x).astype(BF16), jnp.full((DH, DH), 1.0 / DH, BF16), preferred_element_type=F32)
    return x * lax.rsqrt(ms + EPS) * g


def _bdot(a, b):
    return jnp.dot(a.astype(BF16), b.astype(BF16), preferred_element_type=F32)


def _resident(shape):
    zeros = (0,) * len(shape)
    return pl.BlockSpec(shape, lambda *_: zeros, pipeline_mode=pl.Buffered(1))


def _run(gen):
    try:
        while True:
            next(gen)
    except StopIteration as stop:
        return stop.value


def _interleave(primary, *others):
    gens = [primary, *others]
    done = [False] * len(gens)
    vals = [None] * len(gens)

    def step(j):
        try:
            next(gens[j])
        except StopIteration as stop:
            done[j], vals[j] = True, stop.value

    turn = 0
    while not all(done):
        if not done[0]:
            step(0)
        pending = [j for j in range(1, len(gens)) if not done[j]]
        if pending:
            step(pending[turn % len(pending)])
            turn += 1
    return vals


def _inproj_core(x, g_ref, w_ref, wg_ref, proj_ref, gate_ref):
    hb = _rms(x, g_ref[...]).astype(BF16)
    gate_ref[...] = jnp.dot(hb, wg_ref[...], preferred_element_type=F32)
    for c in range(0, 8 * GW, GW):
        yield
        proj_ref[:, c:c + GW] = jnp.dot(hb, w_ref[:, c:c + GW], preferred_element_type=F32)


def _inproj_kernel(x_ref, g_ref, w_ref, wg_ref, proj_ref, gate_ref):
    _run(_inproj_core(x_ref[...], g_ref, w_ref, wg_ref, proj_ref, gate_ref))


IN_COLS = 8 * GW + 2 * HEADS


def _inproj(x2, g1, w_all, w_gate2, tm):
    n = x2.shape[0]
    return pl.pallas_call(
        _inproj_kernel,
        grid=(n // tm,),
        in_specs=[
            pl.BlockSpec((tm, D_MODEL), lambda i: (i, 0)),
            _resident((1, D_MODEL)),
            _resident((D_MODEL, IN_COLS)),
            _resident((D_MODEL, 2 * DH)),
        ],
        out_specs=[
            pl.BlockSpec((tm, 8 * GW), lambda i: (i, 0)),
            pl.BlockSpec((tm, 2 * DH), lambda i: (i, 0)),
        ],
        out_shape=[
            jax.ShapeDtypeStruct((n, 8 * GW), F32),
            jax.ShapeDtypeStruct((n, 2 * DH), F32),
        ],
        compiler_params=pltpu.CompilerParams(
            dimension_semantics=("arbitrary",), vmem_limit_bytes=VMEM_LIMIT),
        name="inproj",
    )(x2, g1, w_all, w_gate2)


def _seg_last(x, seg):
    nseg = TILE // seg
    w = x.shape[-1]
    if nseg == 1:
        last = jnp.broadcast_to(x[TILE - SUB:TILE][SUB - 1:SUB], (SUB, w))
        return jnp.concatenate([last] * (TILE // SUB), axis=0)
    y = x.reshape(nseg, seg, w)[:, seg - 1:seg, :]
    return jnp.broadcast_to(y, (nseg, seg, w)).reshape(TILE, w)


def _hgrn_intra(q, k, v, b, seg, b_ref, k_ref, v_ref):
    sub = lax.broadcasted_iota(jnp.int32, (SUB, 1), 0)
    blocks = []
    for r0 in range(0, TILE, SUB):
        bb = b[r0:r0 + SUB]
        qb = q[r0:r0 + SUB]
        acc = jnp.zeros((SUB, DH), F32)
        for s in range(SUB):
            bs = jnp.broadcast_to(b_ref[r0 + s:r0 + s + 1], (SUB, DH))
            ks = jnp.broadcast_to(k_ref[r0 + s:r0 + s + 1], (SUB, DH))
            vs = jnp.broadcast_to(v_ref[r0 + s:r0 + s + 1], (SUB, DH))
            p = jnp.exp2(bb - bs) * (qb * ks)
            col = jnp.sum(p, axis=-1, keepdims=True)
            col = jnp.where(sub >= s, col, 0.0)
            acc = acc + col * vs
        blocks.append(acc)
        if r0 % (4 * SUB) == 3 * SUB:
            yield
    intra = jnp.concatenate(blocks, axis=0)

    if seg > SUB:
        ri = lax.broadcasted_iota(jnp.int32, (TILE, TILE), 0)
        ci = lax.broadcasted_iota(jnp.int32, (TILE, TILE), 1)
        a_off = jnp.zeros((TILE, TILE), F32)
        w = SUB
        while w < seg:
            zeros = jnp.zeros((w, DH), F32)
            qs, ks = [], []
            for r0 in range(0, TILE, 2 * w):
                ref = b[r0 + w - 1:r0 + w]
                ks += [k[r0:r0 + w] * jnp.exp2(ref - b[r0:r0 + w]), zeros]
                qs += [zeros, q[r0 + w:r0 + 2 * w] * jnp.exp2(b[r0 + w:r0 + 2 * w] - ref)]
            qt = jnp.concatenate(qs, axis=0).astype(BF16)
            kt = jnp.concatenate(ks, axis=0).astype(BF16)
            a_lvl = lax.dot_general(qt, kt, NT, preferred_element_type=F32)
            same_block = (ri // (2 * w)) == (ci // (2 * w))
            a_off = a_off + jnp.where(same_block, a_lvl, 0.0)
            w *= 2
        yield
        intra = intra + _bdot(a_off, v)
    return intra


def _mixer_core(proj_ref, gate_ref, lbl_ref, hgn_ref, mln_ref, gb_ref,
                s_in, c_in, s_out, c_out, n_all, m_rows, put_n, o_ref, b_sc, k_sc, seg):
    nseg = TILE // seg
    ri = lax.broadcasted_iota(jnp.int32, (TILE, TILE), 0)
    ci = lax.broadcasted_iota(jnp.int32, (TILE, TILE), 1)
    tri = ((ri // seg) == (ci // seg)) & (ci <= ri)
    tri_f = tri.astype(F32)
    lane = lax.broadcasted_iota(jnp.int32, (TILE, DH), 1)

    lg = lbl_ref[...]
    ex = jnp.exp(lg - jnp.max(lg, axis=0, keepdims=True))
    lb_all = ex[0:1] / jnp.sum(ex, axis=0, keepdims=True)

    ig_all = gate_ref[:, 0:DH] + gb_ref[:, 0:DH]
    lf_all = jax.nn.log_sigmoid(gate_ref[:, DH:2 * DH] + gb_ref[:, DH:2 * DH])
    f_all = lb_all + (1.0 - lb_all) * jax.nn.sigmoid(proj_ref[:, GW:2 * GW])
    cums = jnp.dot(tri_f, jnp.concatenate([jnp.log2(f_all), lf_all], axis=1), precision=HI,
                   preferred_element_type=F32)
    a_all = cums[:, GW:GW + DH]
    a_all_t = a_all.T
    ig_all_t = ig_all.T
    m_out = jnp.zeros((TILE, DH), F32)
    yield

    for h in range(HEADS):
        hs = slice(h * DH, (h + 1) * DH)

        q = proj_ref[:, 0 * GW + h * DH:0 * GW + (h + 1) * DH]
        v = proj_ref[:, 2 * GW + h * DH:2 * GW + (h + 1) * DH]
        og = proj_ref[:, 3 * GW + h * DH:3 * GW + (h + 1) * DH]
        k = 1.0 - f_all[:, hs]
        b = cums[:, hs]
        b_last = _seg_last(b, seg)
        qe = (q * jnp.exp2(b)).astype(BF16)
        kd = (k * jnp.exp2(b_last - b)).astype(BF16)
        vb16 = v.astype(BF16)
        e_last = jnp.exp2(b_last)
        b_sc[h] = b
        k_sc[h] = k
        yield
        out = yield from _hgrn_intra(q, k, v, b, seg, b_sc.at[h], k_sc.at[h],
                                     proj_ref.at[:, 2 * GW + h * DH:2 * GW + (h + 1) * DH])
        inter = []
        for sg in range(nseg):
            rows = slice(sg * seg, (sg + 1) * seg)
            st = s_in[sg, h]
            inter.append(jnp.dot(qe[rows], st.astype(BF16), preferred_element_type=F32))
            decay = jnp.broadcast_to(e_last[sg * seg:sg * seg + 1], (DH, DH)).T
            s_out[sg, h] = decay * st + lax.dot_general(kd[rows], vb16[rows], TN, preferred_element_type=F32)
        out = out + jnp.concatenate(inter, axis=0)
        out = _rms_mxu(out, hgn_ref[:, hs]) * (og * jax.nn.sigmoid(og))
        o_ref[:, hs] = out
        yield

        q = proj_ref[:, 4 * GW + h * DH:4 * GW + (h + 1) * DH]
        k = proj_ref[:, 5 * GW + h * DH:5 * GW + (h + 1) * DH] * (DH ** -0.5)
        v = proj_ref[:, 6 * GW + h * DH:6 * GW + (h + 1) * DH]
        og = proj_ref[:, 7 * GW + h * DH:7 * GW + (h + 1) * DH]
        a_col = jnp.broadcast_to(a_all[:, h:h + 1], (TILE, TILE))
        i_col = jnp.broadcast_to(ig_all[:, h:h + 1], (TILE, TILE))
        m_col = jnp.broadcast_to(m_rows[:, h:h + 1], (TILE, TILE))
        a_row = a_all_t[h:h + 1]
        i_row = ig_all_t[h:h + 1]
        log_d = jnp.where(tri, a_col - a_row + i_row, -jnp.inf)
        log_inter = a_col + m_col
        m_t = jnp.maximum(log_inter, jnp.max(log_d, axis=-1, keepdims=True))
        d_w = jnp.exp(log_d - m_t)
        w_i = jnp.exp(log_inter - m_t)
        qb16 = q.astype(BF16)
        kb16 = k.astype(BF16)
        vb16 = v.astype(BF16)
        s_w = lax.dot_general(qb16, kb16, NT, preferred_element_type=F32) * d_w
        nd = jnp.dot(s_w.astype(BF16), jnp.concatenate([vb16, jnp.ones((TILE, DH), BF16)], axis=1),
                     preferred_element_type=F32)
        num, den = nd[:, 0:DH], nd[:, DH:2 * DH]
        yield

        m_new = _seg_last(m_t, seg)
        a_end = _seg_last(a_col, seg)
        w_end = jnp.exp(a_end - a_col + i_col - m_new)
        f_end = jnp.exp(a_end + m_col - m_new)
        kw = k * w_end
        kw16 = kw.astype(BF16)
        qc, qn = [], []
        for sg in range(nseg):
            rows = slice(sg * seg, (sg + 1) * seg)
            last = sg * seg + seg - 1
            ct = c_in[sg, h]
            qc.append(jnp.dot(qb16[rows], ct.astype(BF16), preferred_element_type=F32))
            qn.append(q[rows] * n_all[sg:sg + 1, hs])
            fe = f_end[last:last + 1, 0:1]
            c_out[sg, h] = fe * ct + lax.dot_general(kw16[rows], vb16[rows], TN, preferred_element_type=F32)
            put_n(sg, hs, fe * n_all[sg:sg + 1, hs] + jnp.sum(kw[rows], axis=0, keepdims=True))
        num = w_i * jnp.concatenate(qc, axis=0) + num
        qn = jnp.dot(jnp.concatenate(qn, axis=0).astype(BF16), jnp.ones((DH, DH), BF16), preferred_element_type=F32)
        den = w_i * qn + den
        hout = num / jnp.maximum(jnp.abs(den), jnp.exp(-m_t))
        hout = _rms_mxu(hout, mln_ref[:, hs]) * jax.nn.sigmoid(og)
        o_ref[:, GW + h * DH:GW + (h + 1) * DH] = hout
        m_out = jnp.where(lane == h, m_t, m_out)
        yield
    return m_out


def _mixer_seg_kernel(proj_ref, gate_ref, lbl_ref, hgn_ref, mln_ref, gb_ref, s_ref, c_ref, n_ref, m_ref,
                      o_ref, so_ref, co_ref, no_ref, mo_ref, b_sc, k_sc, *, seg):
    def put_n(sg, hs, val):
        no_ref[sg:sg + 1, hs] = val

    m_out = _run(_mixer_core(proj_ref, gate_ref, lbl_ref, hgn_ref, mln_ref, gb_ref, s_ref, c_ref, so_ref, co_ref,
                             n_ref[...], m_ref[...], put_n, o_ref, b_sc, k_sc, seg))
    for sg in range(TILE // seg):
        last = sg * seg + seg - 1
        mo_ref[sg:sg + 1, :] = m_out[last:last + 1]


def _mixer_seg(proj, gates, lb_logits, hg_norm, ml_norm, gate_bias, s0, c0, n0, m0, batch, seq, to_cast=()):
    n = batch * seq
    assert TILE % seq == 0 and seq % SUB == 0 and n % TILE == 0
    nseg = TILE // seq
    row = lambda shape: pl.BlockSpec(shape, lambda i: (i, 0))
    st_spec = pl.BlockSpec((nseg, HEADS, DH, DH), lambda i: (i, 0, 0, 0))
    st_shape = jax.ShapeDtypeStruct((batch, HEADS, DH, DH), F32)
    m_rows = jnp.repeat(jnp.pad(m0, ((0, 0), (0, DH - HEADS))), seq, axis=0)
    o, s_new, c_new, n_new, m_new = pl.pallas_call(
        functools.partial(_mixer_seg_kernel, seg=seq),
        grid=(n // TILE,),
        in_specs=[row((TILE, 8 * GW)), row((TILE, 2 * DH)),
                  _resident((2, GW)), _resident((1, GW)), _resident((1, GW)), _resident((1, 2 * DH)),
                  st_spec, st_spec, row((nseg, GW)), row((TILE, DH))],
        out_specs=[row((TILE, 2 * GW)), st_spec, st_spec, row((nseg, GW)), row((nseg, DH))],
        out_shape=[jax.ShapeDtypeStruct((n, 2 * GW), F32), st_shape, st_shape,
                   jax.ShapeDtypeStruct((batch, GW), F32), jax.ShapeDtypeStruct((batch, DH), F32)],
        scratch_shapes=[pltpu.VMEM((HEADS, TILE, DH), F32), pltpu.VMEM((HEADS, TILE, DH), F32)],
        compiler_params=pltpu.CompilerParams(
            dimension_semantics=("arbitrary",), vmem_limit_bytes=VMEM_LIMIT),
        name="mixer_seg",
    )(proj, gates, lb_logits, hg_norm, ml_norm, gate_bias, s0, c0, n0.reshape(batch, GW), m_rows)
    return o, s_new, c_new, n_new.reshape(batch, HEADS, DH), m_new[:, :HEADS], []


SEQS_PER_STEP = 4


def _mixer_carry_kernel(proj_ref, gate_ref, lbl_ref, hgn_ref, mln_ref, gb_ref, s_ref, c_ref, n_ref, m_ref,
                        *rest, n_cast):
    cast_in, rest = rest[:n_cast], rest[n_cast:]
    (o_ref, so_ref, co_ref, no_ref, mo_ref), rest = rest[:5], rest[5:]
    cast_out, (b_sc, k_sc) = rest[:n_cast], rest[n_cast:]
    for src, dst in zip(cast_in, cast_out):
        dst[...] = src[...].astype(BF16)

    @pl.when(pl.program_id(1) == 0)
    def _():
        so_ref[...] = s_ref[...]
        co_ref[...] = c_ref[...]
        no_ref[...] = n_ref[...]
        mo_ref[...] = m_ref[...]

    tiles = []
    for j in range(SEQS_PER_STEP):
        def put_n(sg, hs, val, j=j):
            no_ref[j, :, hs] = val

        state, cell = so_ref.at[j:j + 1], co_ref.at[j:j + 1]
        tiles.append(_mixer_core(proj_ref.at[j], gate_ref.at[j], lbl_ref, hgn_ref, mln_ref, gb_ref,
                                 state, cell, state, cell, no_ref[j], jnp.broadcast_to(mo_ref[j], (TILE, DH)),
                                 put_n, o_ref.at[j], b_sc.at[j], k_sc.at[j], TILE))
    for j, m_out in enumerate(_interleave(*tiles)):
        mo_ref[j] = m_out[TILE - 1:TILE]


def _mixer_carry(proj, gates, lb_logits, hg_norm, ml_norm, gate_bias, s0, c0, n0, m0, batch, seq, to_cast=()):
    assert seq % TILE == 0 and batch % SEQS_PER_STEP == 0
    nb = SEQS_PER_STEP
    nt = seq // TILE
    assert all(w.shape[0] % (16 * nt) == 0 for w in to_cast)
    cast_specs = [pl.BlockSpec((w.shape[0] // nt, w.shape[1]), lambda b, t: (t, 0)) for w in to_cast]
    tok = lambda w: pl.BlockSpec((nb, TILE, w), lambda b, t: (b, t, 0))
    st_spec = pl.BlockSpec((nb, HEADS, DH, DH), lambda b, t: (b, 0, 0, 0))
    n_spec = pl.BlockSpec((nb, 1, GW), lambda b, t: (b, 0, 0))
    m_spec = pl.BlockSpec((nb, 1, DH), lambda b, t: (b, 0, 0))
    st_shape = jax.ShapeDtypeStruct((batch, HEADS, DH, DH), F32)
    o, s_new, c_new, n_new, m_new, *cast = pl.pallas_call(
        functools.partial(_mixer_carry_kernel, n_cast=len(to_cast)),
        grid=(batch // nb, nt),
        in_specs=[tok(8 * GW), tok(2 * DH),
                  _resident((2, GW)), _resident((1, GW)), _resident((1, GW)), _resident((1, 2 * DH)),
                  st_spec, st_spec, n_spec, m_spec] + cast_specs,
        out_specs=[tok(2 * GW), st_spec, st_spec, n_spec, m_spec] + cast_specs,
        out_shape=[jax.ShapeDtypeStruct((batch, seq, 2 * GW), F32), st_shape, st_shape,
                   jax.ShapeDtypeStruct((batch, 1, GW), F32), jax.ShapeDtypeStruct((batch, 1, DH), F32)]
                  + [jax.ShapeDtypeStruct(w.shape, BF16) for w in to_cast],
        scratch_shapes=[pltpu.VMEM((nb, HEADS, TILE, DH), F32), pltpu.VMEM((nb, HEADS, TILE, DH), F32)],
        compiler_params=pltpu.CompilerParams(
            dimension_semantics=("arbitrary", "arbitrary"), vmem_limit_bytes=VMEM_LIMIT),
        name="mixer_carry",
    )(proj.reshape(batch, seq, 8 * GW), gates.reshape(batch, seq, 2 * DH), lb_logits, hg_norm, ml_norm, gate_bias,
      s0, c0, n0.reshape(batch, 1, GW), jnp.pad(m0, ((0, 0), (0, DH - HEADS))).reshape(batch, 1, DH), *to_cast)
    return o, s_new, c_new, n_new.reshape(batch, HEADS, DH), m_new.reshape(batch, DH)[:, :HEADS], cast


def _ffn_core(x, o, wo_ref, g2_ref, wg_ref, wv_ref, cw_ref, cb_ref, wd_ref, gf_ref, hist_ref, nb_ref, u_sc, g_sc, bb, tt):
    m = bb * tt
    x1 = x + _bdot(o, wo_ref[...])
    h2 = _rms(x1, g2_ref[...]).astype(BF16)
    tpos = lax.broadcasted_iota(jnp.int32, (m, 1), 0) & (tt - 1)
    for c in range(0, D_FF, FF_CHUNK):
        yield
        cs = slice(c, c + FF_CHUNK)
        u = jnp.dot(h2, wg_ref[:, cs], preferred_element_type=F32)
        val = jnp.dot(h2, wv_ref[:, cs], preferred_element_type=F32)
        u_sc[SUB:SUB + m, cs] = u
        u1 = u_sc[SUB - 1:SUB - 1 + m, cs]
        u2 = u_sc[SUB - 2:SUB - 2 + m, cs]
        if bb > 1:
            rows = lambda j0: jnp.concatenate(
                [jnp.broadcast_to(hist_ref[j:j + 1, j0 + c:j0 + c + FF_CHUNK], (tt, FF_CHUNK)) for j in range(bb)],
                axis=0)
            older, newer = rows(0), rows(D_FF)
            u1 = jnp.where(tpos == 0, newer, u1)
            u2 = jnp.where(tpos == 0, older, jnp.where(tpos == 1, newer, u2))
            last = u.reshape(bb, tt, FF_CHUNK)
            for j in range(CONV_W - 1):
                nb_ref[:, j, cs] = last[:, tt - (CONV_W - 1) + j, :]
        conv = cb_ref[:, cs] + cw_ref[0:1, cs] * u2 + cw_ref[1:2, cs] * u1 + cw_ref[2:3, cs] * u
        g_sc[:, cs] = (jax.nn.gelu(conv) * val).astype(BF16)
    if bb == 1:
        u_sc[SUB - 2:SUB, :] = u_sc[SUB + m - 2:SUB + m, :]
    yield
    y = x1 + jnp.dot(g_sc[...], wd_ref[...], preferred_element_type=F32)
    return _rms(y, gf_ref[...])


def _ffn_kernel(x_ref, o_ref, buf_ref, wo_ref, g2_ref, wg_ref, wv_ref, cw_ref, cb_ref, wd_ref, gf_ref,
                y_ref, nb_ref, u_sc, g_sc, *, bb, tt):
    m = bb * tt
    if bb > 1:
        u_sc[0:SUB, :] = jnp.zeros((SUB, D_FF), F32)
    else:
        @pl.when(pl.program_id(1) == 0)
        def _():
            for j in range(CONV_W - 1):
                u_sc[SUB - (CONV_W - 1) + j:SUB - (CONV_W - 1) + j + 1, :] = buf_ref[0, :, j * D_FF:(j + 1) * D_FF]

    y = _run(_ffn_core(x_ref[...].reshape(m, D_MODEL), o_ref[...].reshape(m, D_MODEL), wo_ref, g2_ref, wg_ref,
                       wv_ref, cw_ref, cb_ref, wd_ref, gf_ref, buf_ref, nb_ref, u_sc, g_sc, bb, tt))
    if bb == 1:
        nb_ref[0] = u_sc[SUB - (CONV_W - 1):SUB, :]
    y_ref[...] = y.reshape(bb, tt, D_MODEL)


def _ffn(x, o, buf, wo, g2, wg, wv, cw, cb, wd, gf, bb, tt):
    batch, seq, _ = x.shape
    assert batch % bb == 0 and seq % tt == 0 and tt % SUB == 0 and tt >= CONV_W - 1
    kern = functools.partial(_ffn_kernel, bb=bb, tt=tt)
    tok = pl.BlockSpec((bb, tt, D_MODEL), lambda b, t: (b, t, 0))
    assert tt & (tt - 1) == 0 and (bb == 1 or seq == tt)
    hw = (CONV_W - 1) * D_FF
    if bb == 1:
        hist, hist_shape = pl.BlockSpec((1, 1, hw), lambda b, t: (b, 0, 0)), (batch, 1, hw)
    else:
        hist, hist_shape = pl.BlockSpec((bb, hw), lambda b, t: (b, 0)), (batch, hw)
    hist_out = pl.BlockSpec((bb, CONV_W - 1, D_FF), lambda b, t: (b, 0, 0))
    return pl.pallas_call(
        kern,
        grid=(batch // bb, seq // tt),
        in_specs=[tok, tok, hist,
                  _resident((D_MODEL, D_MODEL)), _resident((1, D_MODEL)),
                  _resident((D_MODEL, D_FF)), _resident((D_MODEL, D_FF)),
                  _resident((CONV_W, D_FF)), _resident((1, D_FF)),
                  _resident((D_FF, D_MODEL)), _resident((1, D_MODEL))],
        out_specs=[tok, hist_out],
        out_shape=[jax.ShapeDtypeStruct((batch, seq, D_MODEL), F32),
                   jax.ShapeDtypeStruct((batch, CONV_W - 1, D_FF), F32)],
        scratch_shapes=[pltpu.VMEM((SUB + bb * tt, D_FF), F32), pltpu.VMEM((bb * tt, D_FF), BF16)],
        compiler_params=pltpu.CompilerParams(
            dimension_semantics=("arbitrary", "arbitrary"), vmem_limit_bytes=VMEM_LIMIT),
        name="ffn",
    )(x, o, buf.reshape(hist_shape), wo, g2, wg, wv, cw, cb, wd, gf)


ROWS_PER_STEP = 512


FFN_WEIGHTS = ("wo", "wg", "wv", "wd")


def _layer(x, s0, c0, n0, m0, buf0, p):
    batch, seq, _ = x.shape
    n = batch * seq
    proj, gates = _inproj(x.reshape(n, D_MODEL), p["g1"], p["w_all"], p["w_gate2"], ROWS_PER_STEP)
    mixer = _mixer_carry if seq >= TILE else _mixer_seg
    pending = () if FFN_WEIGHTS[0] in p else tuple(p["ffn_f32"])
    o, s_new, c_new, n_new, m_new, cast = mixer(proj, gates, p["lb_logits"], p["hg_norm"], p["ml_norm"],
                                                p["gate_bias"], s0, c0, n0, m0, batch, seq, pending)
    if pending:
        p = {**p, **dict(zip(FFN_WEIGHTS, cast or [w.astype(BF16) for w in pending]))}
    tt = min(seq, ROWS_PER_STEP)
    bb = ROWS_PER_STEP // tt
    y, buf_new = _ffn(x, o.reshape(batch, seq, D_MODEL), buf0, p["wo"], p["g2"], p["wg"], p["wv"],
                      p["cw"], p["cb"], p["wd"], p["gf"], bb, tt)
    return (y, s_new[None], c_new[None], n_new[None], m_new[None], buf_new[None]), p


def kernel(x_prompt, x_sample, state_hgrn_S, state_mlstm_C, state_mlstm_n, state_mlstm_m, state_conv, norm1_g, w_in, hg_lb_logits, hg_norm_g, ml_b_ig, ml_b_fg, ml_norm_g, w_out, norm2_g, w_gate, w_val, conv_w, conv_b, w_down, final_norm_g):
    assert norm1_g.shape[0] == 1, "single-layer trunk"
    w = w_in[0]
    gate_cols = w[:, 8 * GW:]
    zpad = jnp.zeros((D_MODEL, DH - HEADS), w.dtype)
    w_gate2 = jnp.concatenate([gate_cols[:, :HEADS], zpad, gate_cols[:, HEADS:], zpad], axis=1)
    bpad = jnp.zeros((DH - HEADS,), F32)
    p = {
        "g1": norm1_g, "w_all": w.astype(BF16), "w_gate2": w_gate2.astype(BF16),
        "lb_logits": hg_lb_logits, "hg_norm": hg_norm_g, "ml_norm": ml_norm_g,
        "gate_bias": jnp.concatenate([ml_b_ig[0], bpad, ml_b_fg[0], bpad])[None],
        "ffn_f32": (w_out[0], w_gate[0], w_val[0], w_down[0]), "g2": norm2_g, "cw": conv_w[0], "cb": conv_b,
        "gf": final_norm_g[None],
    }
    b = x_prompt.shape[0]
    zs = jnp.zeros((b, HEADS, DH, DH), F32)
    prompt, p = _layer(x_prompt, zs, zs, jnp.zeros((b, HEADS, DH), F32), jnp.zeros((b, HEADS), F32),
                       jnp.zeros((b, CONV_W - 1, D_FF), F32), p)
    sample, p = _layer(x_sample, state_hgrn_S[0], state_mlstm_C[0], state_mlstm_n[0], state_mlstm_m[0],
                    state_conv[0], p)
    out = []
    for a, c in zip(prompt, sample):
        out += [a, c]
    return tuple(out)
```

```python
import functools

import jax
import jax.numpy as jnp
from jax import lax
from jax.experimental import pallas as pl
from jax.experimental.pallas import tpu as pltpu

F32 = jnp.float32
BF16 = jnp.bfloat16
HI = lax.Precision.HIGHEST
NT = (((1,), (1,)), ((), ()))
TN = (((0,), (0,)), ((), ()))

D_MODEL = 1024
HEADS = 4
DH = 128
GW = HEADS * DH
D_FF = 2816
CONV_W = 3
EPS = 1e-6
TILE = 128
SUB = 8
FF_CHUNK = 256
VMEM_LIMIT = 56 * 1024 * 1024


def _rms(x, g):
    return x * lax.rsqrt(jnp.mean(x * x, axis=-1, keepdims=True) + EPS) * g


def _rms_mxu(x, g):
    ms = jnp.dot((x * x).astype(BF16), jnp.full((DH, DH), 1.0 / DH, BF16), preferred_element_type=F32)
    return x * lax.rsqrt(ms + EPS) * g


def _bdot(a, b):
    return jnp.dot(a.astype(BF16), b.astype(BF16), preferred_element_type=F32)


def _resident(shape):
    zeros = (0,) * len(shape)
    return pl.BlockSpec(shape, lambda *_: zeros, pipeline_mode=pl.Buffered(1))


def _run(gen):
    try:
        while True:
            next(gen)
    except StopIteration as stop:
        return stop.value


def _interleave(primary, *others):
    gens = [primary, *others]
    done = [False] * len(gens)
    vals = [None] * len(gens)

    def step(j):
        try:
            next(gens[j])
        except StopIteration as stop:
            done[j], vals[j] = True, stop.value

    turn = 0
    while not all(done):
        if not done[0]:
            step(0)
        pending = [j for j in range(1, len(gens)) if not done[j]]
        if pending:
            step(pending[turn % len(pending)])
            turn += 1
    return vals


def _inproj_core(x, g_ref, w_ref, wg_ref, proj_ref, gate_ref):
    hb = _rms(x, g_ref[...]).astype(BF16)
    gate_ref[...] = jnp.dot(hb, wg_ref[...], preferred_element_type=F32)
    for c in range(0, 8 * GW, GW):
        yield
        proj_ref[:, c:c + GW] = jnp.dot(hb, w_ref[:, c:c + GW], preferred_element_type=F32)


def _inproj_kernel(x_ref, g_ref, w_ref, wg_ref, proj_ref, gate_ref):
    _run(_inproj_core(x_ref[...], g_ref, w_ref, wg_ref, proj_ref, gate_ref))


IN_COLS = 8 * GW + 2 * HEADS


def _inproj(x2, g1, w_all, w_gate2, tm):
    n = x2.shape[0]
    return pl.pallas_call(
        _inproj_kernel,
        grid=(n // tm,),
        in_specs=[
            pl.BlockSpec((tm, D_MODEL), lambda i: (i, 0)),
            _resident((1, D_MODEL)),
            _resident((D_MODEL, IN_COLS)),
            _resident((D_MODEL, 2 * DH)),
        ],
        out_specs=[
            pl.BlockSpec((tm, 8 * GW), lambda i: (i, 0)),
            pl.BlockSpec((tm, 2 * DH), lambda i: (i, 0)),
        ],
        out_shape=[
            jax.ShapeDtypeStruct((n, 8 * GW), F32),
            jax.ShapeDtypeStruct((n, 2 * DH), F32),
        ],
        compiler_params=pltpu.CompilerParams(
            dimension_semantics=("arbitrary",), vmem_limit_bytes=VMEM_LIMIT),
        name="inproj",
    )(x2, g1, w_all, w_gate2)


def _seg_last(x, seg):
    nseg = TILE // seg
    w = x.shape[-1]
    if nseg == 1:
        last = jnp.broadcast_to(x[TILE - SUB:TILE][SUB - 1:SUB], (SUB, w))
        return jnp.concatenate([last] * (TILE // SUB), axis=0)
    y = x.reshape(nseg, seg, w)[:, seg - 1:seg, :]
    return jnp.broadcast_to(y, (nseg, seg, w)).reshape(TILE, w)


def _hgrn_intra(q, k, v, b, seg, b_ref, k_ref, v_ref):
    sub = lax.broadcasted_iota(jnp.int32, (SUB, 1), 0)
    blocks = []
    for r0 in range(0, TILE, SUB):
        bb = b[r0:r0 + SUB]
        qb = q[r0:r0 + SUB]
        acc = jnp.zeros((SUB, DH), F32)
        for s in range(SUB):
            bs = jnp.broadcast_to(b_ref[r0 + s:r0 + s + 1], (SUB, DH))
            ks = jnp.broadcast_to(k_ref[r0 + s:r0 + s + 1], (SUB, DH))
            vs = jnp.broadcast_to(v_ref[r0 + s:r0 + s + 1], (SUB, DH))
            p = jnp.exp2(bb - bs) * (qb * ks)
            col = jnp.sum(p, axis=-1, keepdims=True)
            col = jnp.where(sub >= s, col, 0.0)
            acc = acc + col * vs
        blocks.append(acc)
        if r0 % (4 * SUB) == 3 * SUB:
            yield
    intra = jnp.concatenate(blocks, axis=0)

    if seg > SUB:
        ri = lax.broadcasted_iota(jnp.int32, (TILE, TILE), 0)
        ci = lax.broadcasted_iota(jnp.int32, (TILE, TILE), 1)
        a_off = jnp.zeros((TILE, TILE), F32)
        w = SUB
        while w < seg:
            zeros = jnp.zeros((w, DH), F32)
            qs, ks = [], []
            for r0 in range(0, TILE, 2 * w):
                ref = b[r0 + w - 1:r0 + w]
                ks += [k[r0:r0 + w] * jnp.exp2(ref - b[r0:r0 + w]), zeros]
                qs += [zeros, q[r0 + w:r0 + 2 * w] * jnp.exp2(b[r0 + w:r0 + 2 * w] - ref)]
            qt = jnp.concatenate(qs, axis=0).astype(BF16)
            kt = jnp.concatenate(ks, axis=0).astype(BF16)
            a_lvl = lax.dot_general(qt, kt, NT, preferred_element_type=F32)
            same_block = (ri // (2 * w)) == (ci // (2 * w))
            a_off = a_off + jnp.where(same_block, a_lvl, 0.0)
            w *= 2
        yield
        intra = intra + _bdot(a_off, v)
    return intra


def _mixer_core(proj_ref, gate_ref, lbl_ref, hgn_ref, mln_ref, gb_ref,
                s_in, c_in, s_out, c_out, n_all, m_rows, put_n, o_ref, b_sc, k_sc, seg):
    nseg = TILE // seg
    ri = lax.broadcasted_iota(jnp.int32, (TILE, TILE), 0)
    ci = lax.broadcasted_iota(jnp.int32, (TILE, TILE), 1)
    tri = ((ri // seg) == (ci // seg)) & (ci <= ri)
    tri_f = tri.astype(F32)
    lane = lax.broadcasted_iota(jnp.int32, (TILE, DH), 1)

    lg = lbl_ref[...]
    ex = jnp.exp(lg - jnp.max(lg, axis=0, keepdims=True))
    lb_all = ex[0:1] / jnp.sum(ex, axis=0, keepdims=True)

    ig_all = gate_ref[:, 0:DH] + gb_ref[:, 0:DH]
    lf_all = jax.nn.log_sigmoid(gate_ref[:, DH:2 * DH] + gb_ref[:, DH:2 * DH])
    f_all = lb_all + (1.0 - lb_all) * jax.nn.sigmoid(proj_ref[:, GW:2 * GW])
    cums = jnp.dot(tri_f, jnp.concatenate([jnp.log2(f_all), lf_all], axis=1), precision=HI,
                   preferred_element_type=F32)
    a_all = cums[:, GW:GW + DH]
    a_all_t = a_all.T
    ig_all_t = ig_all.T
    m_out = jnp.zeros((TILE, DH), F32)
    yield

    for h in range(HEADS):
        hs = slice(h * DH, (h + 1) * DH)

        q = proj_ref[:, 0 * GW + h * DH:0 * GW + (h + 1) * DH]
        v = proj_ref[:, 2 * GW + h * DH:2 * GW + (h + 1) * DH]
        og = proj_ref[:, 3 * GW + h * DH:3 * GW + (h + 1) * DH]
        k = 1.0 - f_all[:, hs]
        b = cums[:, hs]
        b_last = _seg_last(b, seg)
        qe = (q * jnp.exp2(b)).astype(BF16)
        kd = (k * jnp.exp2(b_last - b)).astype(BF16)
        vb16 = v.astype(BF16)
        e_last = jnp.exp2(b_last)
        b_sc[h] = b
        k_sc[h] = k
        yield
        out = yield from _hgrn_intra(q, k, v, b, seg, b_sc.at[h], k_sc.at[h],
                                     proj_ref.at[:, 2 * GW + h * DH:2 * GW + (h + 1) * DH])
        inter = []
        for sg in range(nseg):
            rows = slice(sg * seg, (sg + 1) * seg)
            st = s_in[sg, h]
            inter.append(jnp.dot(qe[rows], st.astype(BF16), preferred_element_type=F32))
            decay = jnp.broadcast_to(e_last[sg * seg:sg * seg + 1], (DH, DH)).T
            s_out[sg, h] = decay * st + lax.dot_general(kd[rows], vb16[rows], TN, preferred_element_type=F32)
        out = out + jnp.concatenate(inter, axis=0)
        out = _rms_mxu(out, hgn_ref[:, hs]) * (og * jax.nn.sigmoid(og))
        o_ref[:, hs] = out
        yield

        q = proj_ref[:, 4 * GW + h * DH:4 * GW + (h + 1) * DH]
        k = proj_ref[:, 5 * GW + h * DH:5 * GW + (h + 1) * DH] * (DH ** -0.5)
        v = proj_ref[:, 6 * GW + h * DH:6 * GW + (h + 1) * DH]
        og = proj_ref[:, 7 * GW + h * DH:7 * GW + (h + 1) * DH]
        a_col = jnp.broadcast_to(a_all[:, h:h + 1], (TILE, TILE))
        i_col = jnp.broadcast_to(ig_all[:, h:h + 1], (TILE, TILE))
        m_col = jnp.broadcast_to(m_rows[:, h:h + 1], (TILE, TILE))
        a_row = a_all_t[h:h + 1]
        i_row = ig_all_t[h:h + 1]
        log_d = jnp.where(tri, a_col - a_row + i_row, -jnp.inf)
        log_inter = a_col + m_col
        m_t = jnp.maximum(log_inter, jnp.max(log_d, axis=-1, keepdims=True))
        d_w = jnp.exp(log_d - m_t)
        w_i = jnp.exp(log_inter - m_t)
        qb16 = q.astype(BF16)
        kb16 = k.astype(BF16)
        vb16 = v.astype(BF16)
        s_w = lax.dot_general(qb16, kb16, NT, preferred_element_type=F32) * d_w
        nd = jnp.dot(s_w.astype(BF16), jnp.concatenate([vb16, jnp.ones((TILE, DH), BF16)], axis=1),
                     preferred_element_type=F32)
        num, den = nd[:, 0:DH], nd[:, DH:2 * DH]
        yield

        m_new = _seg_last(m_t, seg)
        a_end = _seg_last(a_col, seg)
        w_end = jnp.exp(a_end - a_col + i_col - m_new)
        f_end = jnp.exp(a_end + m_col - m_new)
        kw = k * w_end
        kw16 = kw.astype(BF16)
        qc, qn = [], []
        for sg in range(nseg):
            rows = slice(sg * seg, (sg + 1) * seg)
            last = sg * seg + seg - 1
            ct = c_in[sg, h]
            qc.append(jnp.dot(qb16[rows], ct.astype(BF16), preferred_element_type=F32))
            qn.append(q[rows] * n_all[sg:sg + 1, hs])
            fe = f_end[last:last + 1, 0:1]
            c_out[sg, h] = fe * ct + lax.dot_general(kw16[rows], vb16[rows], TN, preferred_element_type=F32)
            put_n(sg, hs, fe * n_all[sg:sg + 1, hs] + jnp.sum(kw[rows], axis=0, keepdims=True))
        num = w_i * jnp.concatenate(qc, axis=0) + num
        qn = jnp.dot(jnp.concatenate(qn, axis=0).astype(BF16), jnp.ones((DH, DH), BF16), preferred_element_type=F32)
        den = w_i * qn + den
        hout = num / jnp.maximum(jnp.abs(den), jnp.exp(-m_t))
        hout = _rms_mxu(hout, mln_ref[:, hs]) * jax.nn.sigmoid(og)
        o_ref[:, GW + h * DH:GW + (h + 1) * DH] = hout
        m_out = jnp.where(lane == h, m_t, m_out)
        yield
    return m_out


def _mixer_seg_kernel(proj_ref, gate_ref, lbl_ref, hgn_ref, mln_ref, gb_ref, s_ref, c_ref, n_ref, m_ref,
                      o_ref, so_ref, co_ref, no_ref, mo_ref, b_sc, k_sc, *, seg):
    def put_n(sg, hs, val):
        no_ref[sg:sg + 1, hs] = val

    m_out = _run(_mixer_core(proj_ref, gate_ref, lbl_ref, hgn_ref, mln_ref, gb_ref, s_ref, c_ref, so_ref, co_ref,
                             n_ref[...], m_ref[...], put_n, o_ref, b_sc, k_sc, seg))
    for sg in range(TILE // seg):
        last = sg * seg + seg - 1
        mo_ref[sg:sg + 1, :] = m_out[last:last + 1]


def _mixer_seg(proj, gates, lb_logits, hg_norm, ml_norm, gate_bias, s0, c0, n0, m0, batch, seq, to_cast=()):
    n = batch * seq
    assert TILE % seq == 0 and seq % SUB == 0 and n % TILE == 0
    nseg = TILE // seq
    row = lambda shape: pl.BlockSpec(shape, lambda i: (i, 0))
    st_spec = pl.BlockSpec((nseg, HEADS, DH, DH), lambda i: (i, 0, 0, 0))
    st_shape = jax.ShapeDtypeStruct((batch, HEADS, DH, DH), F32)
    m_rows = jnp.repeat(jnp.pad(m0, ((0, 0), (0, DH - HEADS))), seq, axis=0)
    o, s_new, c_new, n_new, m_new = pl.pallas_call(
        functools.partial(_mixer_seg_kernel, seg=seq),
        grid=(n // TILE,),
        in_specs=[row((TILE, 8 * GW)), row((TILE, 2 * DH)),
                  _resident((2, GW)), _resident((1, GW)), _resident((1, GW)), _resident((1, 2 * DH)),
                  st_spec, st_spec, row((nseg, GW)), row((TILE, DH))],
        out_specs=[row((TILE, 2 * GW)), st_spec, st_spec, row((nseg, GW)), row((nseg, DH))],
        out_shape=[jax.ShapeDtypeStruct((n, 2 * GW), F32), st_shape, st_shape,
                   jax.ShapeDtypeStruct((batch, GW), F32), jax.ShapeDtypeStruct((batch, DH), F32)],
        scratch_shapes=[pltpu.VMEM((HEADS, TILE, DH), F32), pltpu.VMEM((HEADS, TILE, DH), F32)],
        compiler_params=pltpu.CompilerParams(
            dimension_semantics=("arbitrary",), vmem_limit_bytes=VMEM_LIMIT),
        name="mixer_seg",
    )(proj, gates, lb_logits, hg_norm, ml_norm, gate_bias, s0, c0, n0.reshape(batch, GW), m_rows)
    return o, s_new, c_new, n_new.reshape(batch, HEADS, DH), m_new[:, :HEADS], []


SEQS_PER_STEP = 4


def _mixer_carry_kernel(proj_ref, gate_ref, lbl_ref, hgn_ref, mln_ref, gb_ref, s_ref, c_ref, n_ref, m_ref,
                        *rest, n_cast):
    cast_in, rest = rest[:n_cast], rest[n_cast:]
    (o_ref, so_ref, co_ref, no_ref, mo_ref), rest = rest[:5], rest[5:]
    cast_out, (b_sc, k_sc) = rest[:n_cast], rest[n_cast:]
    for src, dst in zip(cast_in, cast_out):
        dst[...] = src[...].astype(BF16)

    @pl.when(pl.program_id(1) == 0)
    def _():
        so_ref[...] = s_ref[...]
        co_ref[...] = c_ref[...]
        no_ref[...] = n_ref[...]
        mo_ref[...] = m_ref[...]

    tiles = []
    for j in range(SEQS_PER_STEP):
        def put_n(sg, hs, val, j=j):
            no_ref[j, :, hs] = val

        state, cell = so_ref.at[j:j + 1], co_ref.at[j:j + 1]
        tiles.append(_mixer_core(proj_ref.at[j], gate_ref.at[j], lbl_ref, hgn_ref, mln_ref, gb_ref,
                                 state, cell, state, cell, no_ref[j], jnp.broadcast_to(mo_ref[j], (TILE, DH)),
                                 put_n, o_ref.at[j], b_sc.at[j], k_sc.at[j], TILE))
    for j, m_out in enumerate(_interleave(*tiles)):
        mo_ref[j] = m_out[TILE - 1:TILE]


def _mixer_carry(proj, gates, lb_logits, hg_norm, ml_norm, gate_bias, s0, c0, n0, m0, batch, seq, to_cast=()):
    assert seq % TILE == 0 and batch % SEQS_PER_STEP == 0
    nb = SEQS_PER_STEP
    nt = seq // TILE
    assert all(w.shape[0] % (16 * nt) == 0 for w in to_cast)
    cast_specs = [pl.BlockSpec((w.shape[0] // nt, w.shape[1]), lambda b, t: (jnp.where(b == 0, t, nt - 1), 0))
                  for w in to_cast]
    tok = lambda w: pl.BlockSpec((nb, TILE, w), lambda b, t: (b, t, 0))
    st_spec = pl.BlockSpec((nb, HEADS, DH, DH), lambda b, t: (b, 0, 0, 0))
    n_spec = pl.BlockSpec((nb, 1, GW), lambda b, t: (b, 0, 0))
    m_spec = pl.BlockSpec((nb, 1, DH), lambda b, t: (b, 0, 0))
    st_shape = jax.ShapeDtypeStruct((batch, HEADS, DH, DH), F32)
    o, s_new, c_new, n_new, m_new, *cast = pl.pallas_call(
        functools.partial(_mixer_carry_kernel, n_cast=len(to_cast)),
        grid=(batch // nb, nt),
        in_specs=[tok(8 * GW), tok(2 * DH),
                  _resident((2, GW)), _resident((1, GW)), _resident((1, GW)), _resident((1, 2 * DH)),
                  st_spec, st_spec, n_spec, m_spec] + cast_specs,
        out_specs=[tok(2 * GW), st_spec, st_spec, n_spec, m_spec] + cast_specs,
        out_shape=[jax.ShapeDtypeStruct((batch, seq, 2 * GW), F32), st_shape, st_shape,
                   jax.ShapeDtypeStruct((batch, 1, GW), F32), jax.ShapeDtypeStruct((batch, 1, DH), F32)]
                  + [jax.ShapeDtypeStruct(w.shape, BF16) for w in to_cast],
        scratch_shapes=[pltpu.VMEM((nb, HEADS, TILE, DH), F32), pltpu.VMEM((nb, HEADS, TILE, DH), F32)],
        compiler_params=pltpu.CompilerParams(
            dimension_semantics=("arbitrary", "arbitrary"), vmem_limit_bytes=VMEM_LIMIT),
        name="mixer_carry",
    )(proj.reshape(batch, seq, 8 * GW), gates.reshape(batch, seq, 2 * DH), lb_logits, hg_norm, ml_norm, gate_bias,
      s0, c0, n0.reshape(batch, 1, GW), jnp.pad(m0, ((0, 0), (0, DH - HEADS))).reshape(batch, 1, DH), *to_cast)
    return o, s_new, c_new, n_new.reshape(batch, HEADS, DH), m_new.reshape(batch, DH)[:, :HEADS], cast


def _ffn_core(x, o, wo_ref, g2_ref, wg_ref, wv_ref, cw_ref, cb_ref, wd_ref, gf_ref, hist_ref, nb_ref, u_sc, g_sc, bb, tt):
    m = bb * tt
    x1 = x + _bdot(o, wo_ref[...])
    h2 = _rms(x1, g2_ref[...]).astype(BF16)
    tpos = lax.broadcasted_iota(jnp.int32, (m, 1), 0) & (tt - 1)
    for c in range(0, D_FF, FF_CHUNK):
        yield
        cs = slice(c, c + FF_CHUNK)
        u = jnp.dot(h2, wg_ref[:, cs], preferred_element_type=F32)
        val = jnp.dot(h2, wv_ref[:, cs], preferred_element_type=F32)
        u_sc[SUB:SUB + m, cs] = u
        u1 = u_sc[SUB - 1:SUB - 1 + m, cs]
        u2 = u_sc[SUB - 2:SUB - 2 + m, cs]
        if bb > 1:
            rows = lambda j0: jnp.concatenate(
                [jnp.broadcast_to(hist_ref[j:j + 1, j0 + c:j0 + c + FF_CHUNK], (tt, FF_CHUNK)) for j in range(bb)],
                axis=0)
            older, newer = rows(0), rows(D_FF)
            u1 = jnp.where(tpos == 0, newer, u1)
            u2 = jnp.where(tpos == 0, older, jnp.where(tpos == 1, newer, u2))
            last = u.reshape(bb, tt, FF_CHUNK)
            for j in range(CONV_W - 1):
                nb_ref[:, j, cs] = last[:, tt - (CONV_W - 1) + j, :]
        conv = cb_ref[:, cs] + cw_ref[0:1, cs] * u2 + cw_ref[1:2, cs] * u1 + cw_ref[2:3, cs] * u
        g_sc[:, cs] = (jax.nn.gelu(conv) * val).astype(BF16)
    if bb == 1:
        u_sc[SUB - 2:SUB, :] = u_sc[SUB + m - 2:SUB + m, :]
    yield
    y = x1 + jnp.dot(g_sc[...], wd_ref[...], preferred_element_type=F32)
    return _rms(y, gf_ref[...])


def _ffn_kernel(x_ref, o_ref, buf_ref, wo_ref, g2_ref, wg_ref, wv_ref, cw_ref, cb_ref, wd_ref, gf_ref,
                y_ref, nb_ref, u_sc, g_sc, *, bb, tt):
    m = bb * tt
    if bb > 1:
        u_sc[0:SUB, :] = jnp.zeros((SUB, D_FF), F32)
    else:
        @pl.when(pl.program_id(1) == 0)
        def _():
            for j in range(CONV_W - 1):
                u_sc[SUB - (CONV_W - 1) + j:SUB - (CONV_W - 1) + j + 1, :] = buf_ref[0, :, j * D_FF:(j + 1) * D_FF]

    y = _run(_ffn_core(x_ref[...].reshape(m, D_MODEL), o_ref[...].reshape(m, D_MODEL), wo_ref, g2_ref, wg_ref,
                       wv_ref, cw_ref, cb_ref, wd_ref, gf_ref, buf_ref, nb_ref, u_sc, g_sc, bb, tt))
    if bb == 1:
        nb_ref[0] = u_sc[SUB - (CONV_W - 1):SUB, :]
    y_ref[...] = y.reshape(bb, tt, D_MODEL)


def _ffn(x, o, buf, wo, g2, wg, wv, cw, cb, wd, gf, bb, tt):
    batch, seq, _ = x.shape
    assert batch % bb == 0 and seq % tt == 0 and tt % SUB == 0 and tt >= CONV_W - 1
    kern = functools.partial(_ffn_kernel, bb=bb, tt=tt)
    tok = pl.BlockSpec((bb, tt, D_MODEL), lambda b, t: (b, t, 0))
    assert tt & (tt - 1) == 0 and (bb == 1 or seq == tt)
    hw = (CONV_W - 1) * D_FF
    if bb == 1:
        hist, hist_shape = pl.BlockSpec((1, 1, hw), lambda b, t: (b, 0, 0)), (batch, 1, hw)
    else:
        hist, hist_shape = pl.BlockSpec((bb, hw), lambda b, t: (b, 0)), (batch, hw)
    hist_out = pl.BlockSpec((bb, CONV_W - 1, D_FF), lambda b, t: (b, 0, 0))
    return pl.pallas_call(
        kern,
        grid=(batch // bb, seq // tt),
        in_specs=[tok, tok, hist,
                  _resident((D_MODEL, D_MODEL)), _resident((1, D_MODEL)),
                  _resident((D_MODEL, D_FF)), _resident((D_MODEL, D_FF)),
                  _resident((CONV_W, D_FF)), _resident((1, D_FF)),
                  _resident((D_FF, D_MODEL)), _resident((1, D_MODEL))],
        out_specs=[tok, hist_out],
        out_shape=[jax.ShapeDtypeStruct((batch, seq, D_MODEL), F32),
                   jax.ShapeDtypeStruct((batch, CONV_W - 1, D_FF), F32)],
        scratch_shapes=[pltpu.VMEM((SUB + bb * tt, D_FF), F32), pltpu.VMEM((bb * tt, D_FF), BF16)],
        compiler_params=pltpu.CompilerParams(
            dimension_semantics=("arbitrary", "arbitrary"), vmem_limit_bytes=VMEM_LIMIT),
        name="ffn",
    )(x, o, buf.reshape(hist_shape), wo, g2, wg, wv, cw, cb, wd, gf)


ROWS_PER_STEP = 512


FFN_WEIGHTS = ("wo", "wg", "wv", "wd")


def _layer(x, s0, c0, n0, m0, buf0, p):
    batch, seq, _ = x.shape
    n = batch * seq
    proj, gates = _inproj(x.reshape(n, D_MODEL), p["g1"], p["w_all"], p["w_gate2"], ROWS_PER_STEP)
    mixer = _mixer_carry if seq >= TILE else _mixer_seg
    pending = () if FFN_WEIGHTS[0] in p else tuple(p["ffn_f32"])
    o, s_new, c_new, n_new, m_new, cast = mixer(proj, gates, p["lb_logits"], p["hg_norm"], p["ml_norm"],
                                                p["gate_bias"], s0, c0, n0, m0, batch, seq, pending)
    if pending:
        p = {**p, **dict(zip(FFN_WEIGHTS, cast or [w.astype(BF16) for w in pending]))}
    tt = min(seq, ROWS_PER_STEP)
    bb = ROWS_PER_STEP // tt
    y, buf_new = _ffn(x, o.reshape(batch, seq, D_MODEL), buf0, p["wo"], p["g2"], p["wg"], p["wv"],
                      p["cw"], p["cb"], p["wd"], p["gf"], bb, tt)
    return (y, s_new[None], c_new[None], n_new[None], m_new[None], buf_new[None]), p


def kernel(x_prompt, x_sample, state_hgrn_S, state_mlstm_C, state_mlstm_n, state_mlstm_m, state_conv, norm1_g, w_in, hg_lb_logits, hg_norm_g, ml_b_ig, ml_b_fg, ml_norm_g, w_out, norm2_g, w_gate, w_val, conv_w, conv_b, w_down, final_norm_g):
    assert norm1_g.shape[0] == 1, "single-layer trunk"
    w = w_in[0]
    gate_cols = w[:, 8 * GW:]
    zpad = jnp.zeros((D_MODEL, DH - HEADS), w.dtype)
    w_gate2 = jnp.concatenate([gate_cols[:, :HEADS], zpad, gate_cols[:, HEADS:], zpad], axis=1)
    bpad = jnp.zeros((DH - HEADS,), F32)
    p = {
        "g1": norm1_g, "w_all": w.astype(BF16), "w_gate2": w_gate2.astype(BF16),
        "lb_logits": hg_lb_logits, "hg_norm": hg_norm_g, "ml_norm": ml_norm_g,
        "gate_bias": jnp.concatenate([ml_b_ig[0], bpad, ml_b_fg[0], bpad])[None],
        "ffn_f32": (w_out[0], w_gate[0], w_val[0], w_down[0]), "g2": norm2_g, "cw": conv_w[0], "cb": conv_b,
        "gf": final_norm_g[None],
    }
    b = x_prompt.shape[0]
    zs = jnp.zeros((b, HEADS, DH, DH), F32)
    prompt, p = _layer(x_prompt, zs, zs, jnp.zeros((b, HEADS, DH), F32), jnp.zeros((b, HEADS), F32),
                       jnp.zeros((b, CONV_W - 1, D_FF), F32), p)
    sample, p = _layer(x_sample, state_hgrn_S[0], state_mlstm_C[0], state_mlstm_n[0], state_mlstm_m[0],
                    state_conv[0], p)
    out = []
    for a, c in zip(prompt, sample):
        out += [a, c]
    return tuple(out)
```

```python
import functools

import jax
import jax.numpy as jnp
from jax import lax
from jax.experimental import pallas as pl
from jax.experimental.pallas import tpu as pltpu

F32 = jnp.float32
BF16 = jnp.bfloat16
HI = lax.Precision.HIGHEST
NT = (((1,), (1,)), ((), ()))
TN = (((0,), (0,)), ((), ()))

D_MODEL = 1024
HEADS = 4
DH = 128
GW = HEADS * DH
D_FF = 2816
CONV_W = 3
EPS = 1e-6
TILE = 128
SUB = 8
FF_CHUNK = 256
VMEM_LIMIT = 56 * 1024 * 1024


def _rms(x, g):
    return x * lax.rsqrt(jnp.mean(x * x, axis=-1, keepdims=True) + EPS) * g


def _rms_mxu(x, g):
    ms = jnp.dot((x * x).astype(BF16), jnp.full((DH, DH), 1.0 / DH, BF16), preferred_element_type=F32)
    return x * lax.rsqrt(ms + EPS) * g


def _bdot(a, b):
    return jnp.dot(a.astype(BF16), b.astype(BF16), preferred_element_type=F32)


def _resident(shape):
    zeros = (0,) * len(shape)
    return pl.BlockSpec(shape, lambda *_: zeros, pipeline_mode=pl.Buffered(1))


def _run(gen):
    try:
        while True:
            next(gen)
    except StopIteration as stop:
        return stop.value


def _interleave(primary, *others):
    gens = [primary, *others]
    done = [False] * len(gens)
    vals = [None] * len(gens)

    def step(j):
        try:
            next(gens[j])
        except StopIteration as stop:
            done[j], vals[j] = True, stop.value

    turn = 0
    while not all(done):
        if not done[0]:
            step(0)
        pending = [j for j in range(1, len(gens)) if not done[j]]
        if pending:
            step(pending[turn % len(pending)])
            turn += 1
    return vals


def _inproj_core(x, g_ref, w_ref, wg_ref, proj_ref, gate_ref):
    hb = _rms(x, g_ref[...]).astype(BF16)
    gate_ref[...] = jnp.dot(hb, wg_ref[...], preferred_element_type=F32)
    for c in range(0, 8 * GW, GW):
        yield
        proj_ref[:, c:c + GW] = jnp.dot(hb, w_ref[:, c:c + GW], preferred_element_type=F32)


def _inproj_kernel(x_ref, g_ref, w_ref, wg_ref, proj_ref, gate_ref):
    _run(_inproj_core(x_ref[...], g_ref, w_ref, wg_ref, proj_ref, gate_ref))


IN_COLS = 8 * GW + 2 * HEADS


def _inproj(x2, g1, w_all, w_gate2, tm):
    n = x2.shape[0]
    return pl.pallas_call(
        _inproj_kernel,
        grid=(n // tm,),
        in_specs=[
            pl.BlockSpec((tm, D_MODEL), lambda i: (i, 0)),
            _resident((1, D_MODEL)),
            _resident((D_MODEL, IN_COLS)),
            _resident((D_MODEL, 2 * DH)),
        ],
        out_specs=[
            pl.BlockSpec((tm, 8 * GW), lambda i: (i, 0)),
            pl.BlockSpec((tm, 2 * DH), lambda i: (i, 0)),
        ],
        out_shape=[
            jax.ShapeDtypeStruct((n, 8 * GW), F32),
            jax.ShapeDtypeStruct((n, 2 * DH), F32),
        ],
        compiler_params=pltpu.CompilerParams(
            dimension_semantics=("arbitrary",), vmem_limit_bytes=VMEM_LIMIT),
        name="inproj",
    )(x2, g1, w_all, w_gate2)


def _seg_last(x, seg):
    nseg = TILE // seg
    w = x.shape[-1]
    if nseg == 1:
        last = jnp.broadcast_to(x[TILE - SUB:TILE][SUB - 1:SUB], (SUB, w))
        return jnp.concatenate([last] * (TILE // SUB), axis=0)
    y = x.reshape(nseg, seg, w)[:, seg - 1:seg, :]
    return jnp.broadcast_to(y, (nseg, seg, w)).reshape(TILE, w)


def _hgrn_intra(q, k, v, b, seg, b_ref, k_ref, v_ref):
    sub = lax.broadcasted_iota(jnp.int32, (SUB, 1), 0)
    blocks = []
    for r0 in range(0, TILE, SUB):
        bb = b[r0:r0 + SUB]
        qb = q[r0:r0 + SUB]
        acc = jnp.zeros((SUB, DH), F32)
        for s in range(SUB):
            bs = jnp.broadcast_to(b_ref[r0 + s:r0 + s + 1], (SUB, DH))
            ks = jnp.broadcast_to(k_ref[r0 + s:r0 + s + 1], (SUB, DH))
            vs = jnp.broadcast_to(v_ref[r0 + s:r0 + s + 1], (SUB, DH))
            p = jnp.exp2(bb - bs) * (qb * ks)
            col = jnp.sum(p, axis=-1, keepdims=True)
            col = jnp.where(sub >= s, col, 0.0)
            acc = acc + col * vs
        blocks.append(acc)
        if r0 % (4 * SUB) == 3 * SUB:
            yield
    intra = jnp.concatenate(blocks, axis=0)

    if seg > SUB:
        ri = lax.broadcasted_iota(jnp.int32, (TILE, TILE), 0)
        ci = lax.broadcasted_iota(jnp.int32, (TILE, TILE), 1)
        a_off = jnp.zeros((TILE, TILE), F32)
        w = SUB
        while w < seg:
            zeros = jnp.zeros((w, DH), F32)
            qs, ks = [], []
            for r0 in range(0, TILE, 2 * w):
                ref = b[r0 + w - 1:r0 + w]
                ks += [k[r0:r0 + w] * jnp.exp2(ref - b[r0:r0 + w]), zeros]
                qs += [zeros, q[r0 + w:r0 + 2 * w] * jnp.exp2(b[r0 + w:r0 + 2 * w] - ref)]
            qt = jnp.concatenate(qs, axis=0).astype(BF16)
            kt = jnp.concatenate(ks, axis=0).astype(BF16)
            a_lvl = lax.dot_general(qt, kt, NT, preferred_element_type=F32)
            same_block = (ri // (2 * w)) == (ci // (2 * w))
            a_off = a_off + jnp.where(same_block, a_lvl, 0.0)
            w *= 2
        yield
        intra = intra + _bdot(a_off, v)
    return intra


def _mixer_core(proj_ref, gate_ref, lbl_ref, hgn_ref, mln_ref, gb_ref,
                s_in, c_in, s_out, c_out, n_all, m_rows, put_n, o_ref, b_sc, k_sc, seg):
    nseg = TILE // seg
    ri = lax.broadcasted_iota(jnp.int32, (TILE, TILE), 0)
    ci = lax.broadcasted_iota(jnp.int32, (TILE, TILE), 1)
    tri = ((ri // seg) == (ci // seg)) & (ci <= ri)
    tri_f = tri.astype(F32)
    lane = lax.broadcasted_iota(jnp.int32, (TILE, DH), 1)

    lg = lbl_ref[...]
    ex = jnp.exp(lg - jnp.max(lg, axis=0, keepdims=True))
    lb_all = ex[0:1] / jnp.sum(ex, axis=0, keepdims=True)

    ig_all = gate_ref[:, 0:DH] + gb_ref[:, 0:DH]
    lf_all = jax.nn.log_sigmoid(gate_ref[:, DH:2 * DH] + gb_ref[:, DH:2 * DH])
    f_all = lb_all + (1.0 - lb_all) * jax.nn.sigmoid(proj_ref[:, GW:2 * GW])
    cums = jnp.dot(tri_f, jnp.concatenate([jnp.log2(f_all), lf_all], axis=1), precision=HI,
                   preferred_element_type=F32)
    a_all = cums[:, GW:GW + DH]
    a_all_t = a_all.T
    ig_all_t = ig_all.T
    m_out = jnp.zeros((TILE, DH), F32)
    yield

    for h in range(HEADS):
        hs = slice(h * DH, (h + 1) * DH)

        q = proj_ref[:, 0 * GW + h * DH:0 * GW + (h + 1) * DH]
        v = proj_ref[:, 2 * GW + h * DH:2 * GW + (h + 1) * DH]
        og = proj_ref[:, 3 * GW + h * DH:3 * GW + (h + 1) * DH]
        k = 1.0 - f_all[:, hs]
        b = cums[:, hs]
        b_last = _seg_last(b, seg)
        qe = (q * jnp.exp2(b)).astype(BF16)
        kd = (k * jnp.exp2(b_last - b)).astype(BF16)
        vb16 = v.astype(BF16)
        e_last = jnp.exp2(b_last)
        b_sc[h] = b
        k_sc[h] = k
        yield
        out = yield from _hgrn_intra(q, k, v, b, seg, b_sc.at[h], k_sc.at[h],
                                     proj_ref.at[:, 2 * GW + h * DH:2 * GW + (h + 1) * DH])
        inter = []
        for sg in range(nseg):
            rows = slice(sg * seg, (sg + 1) * seg)
            st = s_in[sg, h]
            inter.append(jnp.dot(qe[rows], st.astype(BF16), preferred_element_type=F32))
            decay = jnp.broadcast_to(e_last[sg * seg:sg * seg + 1], (DH, DH)).T
            s_out[sg, h] = decay * st + lax.dot_general(kd[rows], vb16[rows], TN, preferred_element_type=F32)
        out = out + jnp.concatenate(inter, axis=0)
        out = _rms_mxu(out, hgn_ref[:, hs]) * (og * jax.nn.sigmoid(og))
        o_ref[:, hs] = out
        yield

        q = proj_ref[:, 4 * GW + h * DH:4 * GW + (h + 1) * DH]
        k = proj_ref[:, 5 * GW + h * DH:5 * GW + (h + 1) * DH] * (DH ** -0.5)
        v = proj_ref[:, 6 * GW + h * DH:6 * GW + (h + 1) * DH]
        og = proj_ref[:, 7 * GW + h * DH:7 * GW + (h + 1) * DH]
        a_col = jnp.broadcast_to(a_all[:, h:h + 1], (TILE, TILE))
        i_col = jnp.broadcast_to(ig_all[:, h:h + 1], (TILE, TILE))
        m_col = jnp.broadcast_to(m_rows[:, h:h + 1], (TILE, TILE))
        a_row = a_all_t[h:h + 1]
        i_row = ig_all_t[h:h + 1]
        log_d = jnp.where(tri, a_col - a_row + i_row, -jnp.inf)
        log_inter = a_col + m_col
        m_t = jnp.maximum(log_inter, jnp.max(log_d, axis=-1, keepdims=True))
        d_w = jnp.exp(log_d - m_t)
        w_i = jnp.exp(log_inter - m_t)
        qb16 = q.astype(BF16)
        kb16 = k.astype(BF16)
        vb16 = v.astype(BF16)
        s_w = lax.dot_general(qb16, kb16, NT, preferred_element_type=F32) * d_w
        nd = jnp.dot(s_w.astype(BF16), jnp.concatenate([vb16, jnp.ones((TILE, DH), BF16)], axis=1),
                     preferred_element_type=F32)
        num, den = nd[:, 0:DH], nd[:, DH:2 * DH]
        yield

        m_new = _seg_last(m_t, seg)
        a_end = _seg_last(a_col, seg)
        w_end = jnp.exp(a_end - a_col + i_col - m_new)
        f_end = jnp.exp(a_end + m_col - m_new)
        kw = k * w_end
        kw16 = kw.astype(BF16)
        qc, qn = [], []
        for sg in range(nseg):
            rows = slice(sg * seg, (sg + 1) * seg)
            last = sg * seg + seg - 1
            ct = c_in[sg, h]
            qc.append(jnp.dot(qb16[rows], ct.astype(BF16), preferred_element_type=F32))
            qn.append(q[rows] * n_all[sg:sg + 1, hs])
            fe = f_end[last:last + 1, 0:1]
            c_out[sg, h] = fe * ct + lax.dot_general(kw16[rows], vb16[rows], TN, preferred_element_type=F32)
            put_n(sg, hs, fe * n_all[sg:sg + 1, hs] + jnp.sum(kw[rows], axis=0, keepdims=True))
        num = w_i * jnp.concatenate(qc, axis=0) + num
        qn = jnp.dot(jnp.concatenate(qn, axis=0).astype(BF16), jnp.ones((DH, DH), BF16), preferred_element_type=F32)
        den = w_i * qn + den
        hout = num / jnp.maximum(jnp.abs(den), jnp.exp(-m_t))
        hout = _rms_mxu(hout, mln_ref[:, hs]) * jax.nn.sigmoid(og)
        o_ref[:, GW + h * DH:GW + (h + 1) * DH] = hout
        m_out = jnp.where(lane == h, m_t, m_out)
        yield
    return m_out


def _mixer_seg_kernel(proj_ref, gate_ref, lbl_ref, hgn_ref, mln_ref, gb_ref, s_ref, c_ref, n_ref, m_ref,
                      o_ref, so_ref, co_ref, no_ref, mo_ref, b_sc, k_sc, *, seg):
    def put_n(sg, hs, val):
        no_ref[sg:sg + 1, hs] = val

    m_out = _run(_mixer_core(proj_ref, gate_ref, lbl_ref, hgn_ref, mln_ref, gb_ref, s_ref, c_ref, so_ref, co_ref,
                             n_ref[...], m_ref[...], put_n, o_ref, b_sc, k_sc, seg))
    for sg in range(TILE // seg):
        last = sg * seg + seg - 1
        mo_ref[sg:sg + 1, :] = m_out[last:last + 1]


def _mixer_seg(proj, gates, lb_logits, hg_norm, ml_norm, gate_bias, s0, c0, n0, m0, batch, seq, to_cast=()):
    n = batch * seq
    assert TILE % seq == 0 and seq % SUB == 0 and n % TILE == 0
    nseg = TILE // seq
    row = lambda shape: pl.BlockSpec(shape, lambda i: (i, 0))
    st_spec = pl.BlockSpec((nseg, HEADS, DH, DH), lambda i: (i, 0, 0, 0))
    st_shape = jax.ShapeDtypeStruct((batch, HEADS, DH, DH), F32)
    m_rows = jnp.repeat(jnp.pad(m0, ((0, 0), (0, DH - HEADS))), seq, axis=0)
    o, s_new, c_new, n_new, m_new = pl.pallas_call(
        functools.partial(_mixer_seg_kernel, seg=seq),
        grid=(n // TILE,),
        in_specs=[row((TILE, 8 * GW)), row((TILE, 2 * DH)),
                  _resident((2, GW)), _resident((1, GW)), _resident((1, GW)), _resident((1, 2 * DH)),
                  st_spec, st_spec, row((nseg, GW)), row((TILE, DH))],
        out_specs=[row((TILE, 2 * GW)), st_spec, st_spec, row((nseg, GW)), row((nseg, DH))],
        out_shape=[jax.ShapeDtypeStruct((n, 2 * GW), F32), st_shape, st_shape,
                   jax.ShapeDtypeStruct((batch, GW), F32), jax.ShapeDtypeStruct((batch, DH), F32)],
        scratch_shapes=[pltpu.VMEM((HEADS, TILE, DH), F32), pltpu.VMEM((HEADS, TILE, DH), F32)],
        compiler_params=pltpu.CompilerParams(
            dimension_semantics=("arbitrary",), vmem_limit_bytes=VMEM_LIMIT),
        name="mixer_seg",
    )(proj, gates, lb_logits, hg_norm, ml_norm, gate_bias, s0, c0, n0.reshape(batch, GW), m_rows)
    return o, s_new, c_new, n_new.reshape(batch, HEADS, DH), m_new[:, :HEADS], []


SEQS_PER_STEP = 4


def _mixer_carry_kernel(proj_ref, gate_ref, lbl_ref, hgn_ref, mln_ref, gb_ref, s_ref, c_ref, n_ref, m_ref,
                        *rest, n_cast):
    cast_in, rest = rest[:n_cast], rest[n_cast:]
    (o_ref, so_ref, co_ref, no_ref, mo_ref), rest = rest[:5], rest[5:]
    cast_out, (b_sc, k_sc) = rest[:n_cast], rest[n_cast:]
    @pl.when(pl.program_id(0) == 0)
    def _():
        for src, dst in zip(cast_in, cast_out):
            dst[...] = src[...].astype(BF16)

    @pl.when(pl.program_id(1) == 0)
    def _():
        so_ref[...] = s_ref[...]
        co_ref[...] = c_ref[...]
        no_ref[...] = n_ref[...]
        mo_ref[...] = m_ref[...]

    tiles = []
    for j in range(SEQS_PER_STEP):
        def put_n(sg, hs, val, j=j):
            no_ref[j, :, hs] = val

        state, cell = so_ref.at[j:j + 1], co_ref.at[j:j + 1]
        tiles.append(_mixer_core(proj_ref.at[j], gate_ref.at[j], lbl_ref, hgn_ref, mln_ref, gb_ref,
                                 state, cell, state, cell, no_ref[j], jnp.broadcast_to(mo_ref[j], (TILE, DH)),
                                 put_n, o_ref.at[j], b_sc.at[j], k_sc.at[j], TILE))
    for j, m_out in enumerate(_interleave(*tiles)):
        mo_ref[j] = m_out[TILE - 1:TILE]


def _mixer_carry(proj, gates, lb_logits, hg_norm, ml_norm, gate_bias, s0, c0, n0, m0, batch, seq, to_cast=()):
    assert seq % TILE == 0 and batch % SEQS_PER_STEP == 0
    nb = SEQS_PER_STEP
    nt = seq // TILE
    assert all(w.shape[0] % (16 * nt) == 0 for w in to_cast)
    cast_specs = [pl.BlockSpec((w.shape[0] // nt, w.shape[1]), lambda b, t: (jnp.where(b == 0, t, nt - 1), 0))
                  for w in to_cast]
    tok = lambda w: pl.BlockSpec((nb, TILE, w), lambda b, t: (b, t, 0))
    st_spec = pl.BlockSpec((nb, HEADS, DH, DH), lambda b, t: (b, 0, 0, 0))
    n_spec = pl.BlockSpec((nb, 1, GW), lambda b, t: (b, 0, 0))
    m_spec = pl.BlockSpec((nb, 1, DH), lambda b, t: (b, 0, 0))
    st_shape = jax.ShapeDtypeStruct((batch, HEADS, DH, DH), F32)
    o, s_new, c_new, n_new, m_new, *cast = pl.pallas_call(
        functools.partial(_mixer_carry_kernel, n_cast=len(to_cast)),
        grid=(batch // nb, nt),
        in_specs=[tok(8 * GW), tok(2 * DH),
                  _resident((2, GW)), _resident((1, GW)), _resident((1, GW)), _resident((1, 2 * DH)),
                  st_spec, st_spec, n_spec, m_spec] + cast_specs,
        out_specs=[tok(2 * GW), st_spec, st_spec, n_spec, m_spec] + cast_specs,
        out_shape=[jax.ShapeDtypeStruct((batch, seq, 2 * GW), F32), st_shape, st_shape,
                   jax.ShapeDtypeStruct((batch, 1, GW), F32), jax.ShapeDtypeStruct((batch, 1, DH), F32)]
                  + [jax.ShapeDtypeStruct(w.shape, BF16) for w in to_cast],
        scratch_shapes=[pltpu.VMEM((nb, HEADS, TILE, DH), F32), pltpu.VMEM((nb, HEADS, TILE, DH), F32)],
        compiler_params=pltpu.CompilerParams(
            dimension_semantics=("arbitrary", "arbitrary"), vmem_limit_bytes=VMEM_LIMIT),
        name="mixer_carry",
    )(proj.reshape(batch, seq, 8 * GW), gates.reshape(batch, seq, 2 * DH), lb_logits, hg_norm, ml_norm, gate_bias,
      s0, c0, n0.reshape(batch, 1, GW), jnp.pad(m0, ((0, 0), (0, DH - HEADS))).reshape(batch, 1, DH), *to_cast)
    return o, s_new, c_new, n_new.reshape(batch, HEADS, DH), m_new.reshape(batch, DH)[:, :HEADS], cast


def _ffn_core(x, o, wo_ref, g2_ref, wg_ref, wv_ref, cw_ref, cb_ref, wd_ref, gf_ref, hist_ref, nb_ref, u_sc, g_sc, bb, tt):
    m = bb * tt
    x1 = x + _bdot(o, wo_ref[...])
    h2 = _rms(x1, g2_ref[...]).astype(BF16)
    tpos = lax.broadcasted_iota(jnp.int32, (m, 1), 0) & (tt - 1)
    for c in range(0, D_FF, FF_CHUNK):
        yield
        cs = slice(c, c + FF_CHUNK)
        u = jnp.dot(h2, wg_ref[:, cs], preferred_element_type=F32)
        val = jnp.dot(h2, wv_ref[:, cs], preferred_element_type=F32)
        u_sc[SUB:SUB + m, cs] = u
        u1 = u_sc[SUB - 1:SUB - 1 + m, cs]
        u2 = u_sc[SUB - 2:SUB - 2 + m, cs]
        if bb > 1:
            rows = lambda j0: jnp.concatenate(
                [jnp.broadcast_to(hist_ref[j:j + 1, j0 + c:j0 + c + FF_CHUNK], (tt, FF_CHUNK)) for j in range(bb)],
                axis=0)
            older, newer = rows(0), rows(D_FF)
            u1 = jnp.where(tpos == 0, newer, u1)
            u2 = jnp.where(tpos == 0, older, jnp.where(tpos == 1, newer, u2))
            last = u.reshape(bb, tt, FF_CHUNK)
            for j in range(CONV_W - 1):
                nb_ref[:, j, cs] = last[:, tt - (CONV_W - 1) + j, :]
        conv = cb_ref[:, cs] + cw_ref[0:1, cs] * u2 + cw_ref[1:2, cs] * u1 + cw_ref[2:3, cs] * u
        g_sc[:, cs] = (jax.nn.gelu(conv) * val).astype(BF16)
    if bb == 1:
        u_sc[SUB - 2:SUB, :] = u_sc[SUB + m - 2:SUB + m, :]
    yield
    y = x1 + jnp.dot(g_sc[...], wd_ref[...], preferred_element_type=F32)
    return _rms(y, gf_ref[...])


def _ffn_kernel(x_ref, o_ref, buf_ref, wo_ref, g2_ref, wg_ref, wv_ref, cw_ref, cb_ref, wd_ref, gf_ref,
                y_ref, nb_ref, u_sc, g_sc, *, bb, tt):
    m = bb * tt
    if bb > 1:
        u_sc[0:SUB, :] = jnp.zeros((SUB, D_FF), F32)
    else:
        @pl.when(pl.program_id(1) == 0)
        def _():
            for j in range(CONV_W - 1):
                u_sc[SUB - (CONV_W - 1) + j:SUB - (CONV_W - 1) + j + 1, :] = buf_ref[0, :, j * D_FF:(j + 1) * D_FF]

    y = _run(_ffn_core(x_ref[...].reshape(m, D_MODEL), o_ref[...].reshape(m, D_MODEL), wo_ref, g2_ref, wg_ref,
                       wv_ref, cw_ref, cb_ref, wd_ref, gf_ref, buf_ref, nb_ref, u_sc, g_sc, bb, tt))
    if bb == 1:
        nb_ref[0] = u_sc[SUB - (CONV_W - 1):SUB, :]
    y_ref[...] = y.reshape(bb, tt, D_MODEL)


def _ffn(x, o, buf, wo, g2, wg, wv, cw, cb, wd, gf, bb, tt):
    batch, seq, _ = x.shape
    assert batch % bb == 0 and seq % tt == 0 and tt % SUB == 0 and tt >= CONV_W - 1
    kern = functools.partial(_ffn_kernel, bb=bb, tt=tt)
    tok = pl.BlockSpec((bb, tt, D_MODEL), lambda b, t: (b, t, 0))
    assert tt & (tt - 1) == 0 and (bb == 1 or seq == tt)
    hw = (CONV_W - 1) * D_FF
    if bb == 1:
        hist, hist_shape = pl.BlockSpec((1, 1, hw), lambda b, t: (b, 0, 0)), (batch, 1, hw)
    else:
        hist, hist_shape = pl.BlockSpec((bb, hw), lambda b, t: (b, 0)), (batch, hw)
    hist_out = pl.BlockSpec((bb, CONV_W - 1, D_FF), lambda b, t: (b, 0, 0))
    return pl.pallas_call(
        kern,
        grid=(batch // bb, seq // tt),
        in_specs=[tok, tok, hist,
                  _resident((D_MODEL, D_MODEL)), _resident((1, D_MODEL)),
                  _resident((D_MODEL, D_FF)), _resident((D_MODEL, D_FF)),
                  _resident((CONV_W, D_FF)), _resident((1, D_FF)),
                  _resident((D_FF, D_MODEL)), _resident((1, D_MODEL))],
        out_specs=[tok, hist_out],
        out_shape=[jax.ShapeDtypeStruct((batch, seq, D_MODEL), F32),
                   jax.ShapeDtypeStruct((batch, CONV_W - 1, D_FF), F32)],
        scratch_shapes=[pltpu.VMEM((SUB + bb * tt, D_FF), F32), pltpu.VMEM((bb * tt, D_FF), BF16)],
        compiler_params=pltpu.CompilerParams(
            dimension_semantics=("arbitrary", "arbitrary"), vmem_limit_bytes=VMEM_LIMIT),
        name="ffn",
    )(x, o, buf.reshape(hist_shape), wo, g2, wg, wv, cw, cb, wd, gf)


ROWS_PER_STEP = 512


FFN_WEIGHTS = ("wo", "wg", "wv", "wd")


def _layer(x, s0, c0, n0, m0, buf0, p):
    batch, seq, _ = x.shape
    n = batch * seq
    proj, gates = _inproj(x.reshape(n, D_MODEL), p["g1"], p["w_all"], p["w_gate2"], ROWS_PER_STEP)
    mixer = _mixer_carry if seq >= TILE else _mixer_seg
    pending = () if FFN_WEIGHTS[0] in p else tuple(p["ffn_f32"])
    o, s_new, c_new, n_new, m_new, cast = mixer(proj, gates, p["lb_logits"], p["hg_norm"], p["ml_norm"],
                                                p["gate_bias"], s0, c0, n0, m0, batch, seq, pending)
    if pending:
        p = {**p, **dict(zip(FFN_WEIGHTS, cast or [w.astype(BF16) for w in pending]))}
    tt = min(seq, ROWS_PER_STEP)
    bb = ROWS_PER_STEP // tt
    y, buf_new = _ffn(x, o.reshape(batch, seq, D_MODEL), buf0, p["wo"], p["g2"], p["wg"], p["wv"],
                      p["cw"], p["cb"], p["wd"], p["gf"], bb, tt)
    return (y, s_new[None], c_new[None], n_new[None], m_new[None], buf_new[None]), p


def kernel(x_prompt, x_sample, state_hgrn_S, state_mlstm_C, state_mlstm_n, state_mlstm_m, state_conv, norm1_g, w_in, hg_lb_logits, hg_norm_g, ml_b_ig, ml_b_fg, ml_norm_g, w_out, norm2_g, w_gate, w_val, conv_w, conv_b, w_down, final_norm_g):
    assert norm1_g.shape[0] == 1, "single-layer trunk"
    w = w_in[0]
    gate_cols = w[:, 8 * GW:]
    zpad = jnp.zeros((D_MODEL, DH - HEADS), w.dtype)
    w_gate2 = jnp.concatenate([gate_cols[:, :HEADS], zpad, gate_cols[:, HEADS:], zpad], axis=1)
    bpad = jnp.zeros((DH - HEADS,), F32)
    p = {
        "g1": norm1_g, "w_all": w.astype(BF16), "w_gate2": w_gate2.astype(BF16),
        "lb_logits": hg_lb_logits, "hg_norm": hg_norm_g, "ml_norm": ml_norm_g,
        "gate_bias": jnp.concatenate([ml_b_ig[0], bpad, ml_b_fg[0], bpad])[None],
        "ffn_f32": (w_out[0], w_gate[0], w_val[0], w_down[0]), "g2": norm2_g, "cw": conv_w[0], "cb": conv_b,
        "gf": final_norm_g[None],
    }
    b = x_prompt.shape[0]
    zs = jnp.zeros((b, HEADS, DH, DH), F32)
    prompt, p = _layer(x_prompt, zs, zs, jnp.zeros((b, HEADS, DH), F32), jnp.zeros((b, HEADS), F32),
                       jnp.zeros((b, CONV_W - 1, D_FF), F32), p)
    sample, p = _layer(x_sample, state_hgrn_S[0], state_mlstm_C[0], state_mlstm_n[0], state_mlstm_m[0],
                    state_conv[0], p)
    out = []
    for a, c in zip(prompt, sample):
        out += [a, c]
    return tuple(out)
```

```python
import functools

import jax
import jax.numpy as jnp
from jax import lax
from jax.experimental import pallas as pl
from jax.experimental.pallas import tpu as pltpu

F32 = jnp.float32
BF16 = jnp.bfloat16
HI = lax.Precision.HIGHEST
NT = (((1,), (1,)), ((), ()))
TN = (((0,), (0,)), ((), ()))

D_MODEL = 1024
HEADS = 4
DH = 128
GW = HEADS * DH
D_FF = 2816
CONV_W = 3
EPS = 1e-6
TILE = 128
SUB = 8
FF_CHUNK = 256
VMEM_LIMIT = 56 * 1024 * 1024


def _rms(x, g):
    return x * lax.rsqrt(jnp.mean(x * x, axis=-1, keepdims=True) + EPS) * g


def _rms_mxu(x, g):
    ms = jnp.dot((x * x).astype(BF16), jnp.full((DH, DH), 1.0 / DH, BF16), preferred_element_type=F32)
    return x * lax.rsqrt(ms + EPS) * g


def _bdot(a, b):
    return jnp.dot(a.astype(BF16), b.astype(BF16), preferred_element_type=F32)


def _resident(shape):
    zeros = (0,) * len(shape)
    return pl.BlockSpec(shape, lambda *_: zeros, pipeline_mode=pl.Buffered(1))


def _run(gen):
    try:
        while True:
            next(gen)
    except StopIteration as stop:
        return stop.value


def _interleave(primary, *others):
    gens = [primary, *others]
    done = [False] * len(gens)
    vals = [None] * len(gens)

    def step(j):
        try:
            next(gens[j])
        except StopIteration as stop:
            done[j], vals[j] = True, stop.value

    turn = 0
    while not all(done):
        if not done[0]:
            step(0)
        pending = [j for j in range(1, len(gens)) if not done[j]]
        if pending:
            step(pending[turn % len(pending)])
            turn += 1
    return vals


def _inproj_core(x, g_ref, w_ref, wg_ref, proj_ref, gate_ref):
    hb = _rms(x, g_ref[...]).astype(BF16)
    gate_ref[...] = jnp.dot(hb, wg_ref[...], preferred_element_type=F32)
    for c in range(0, 8 * GW, GW):
        yield
        proj_ref[:, c:c + GW] = jnp.dot(hb, w_ref[:, c:c + GW], preferred_element_type=F32)


def _inproj_kernel(x_ref, g_ref, w_ref, wg_ref, proj_ref, gate_ref):
    _run(_inproj_core(x_ref[...], g_ref, w_ref, wg_ref, proj_ref, gate_ref))


IN_COLS = 8 * GW + 2 * HEADS


def _inproj(x2, g1, w_all, w_gate2, tm):
    n = x2.shape[0]
    return pl.pallas_call(
        _inproj_kernel,
        grid=(n // tm,),
        in_specs=[
            pl.BlockSpec((tm, D_MODEL), lambda i: (i, 0)),
            _resident((1, D_MODEL)),
            _resident((D_MODEL, IN_COLS)),
            _resident((D_MODEL, 2 * DH)),
        ],
        out_specs=[
            pl.BlockSpec((tm, 8 * GW), lambda i: (i, 0)),
            pl.BlockSpec((tm, 2 * DH), lambda i: (i, 0)),
        ],
        out_shape=[
            jax.ShapeDtypeStruct((n, 8 * GW), F32),
            jax.ShapeDtypeStruct((n, 2 * DH), F32),
        ],
        compiler_params=pltpu.CompilerParams(
            dimension_semantics=("arbitrary",), vmem_limit_bytes=VMEM_LIMIT),
        name="inproj",
    )(x2, g1, w_all, w_gate2)


def _seg_last(x, seg):
    nseg = TILE // seg
    w = x.shape[-1]
    if nseg == 1:
        last = jnp.broadcast_to(x[TILE - SUB:TILE][SUB - 1:SUB], (SUB, w))
        return jnp.concatenate([last] * (TILE // SUB), axis=0)
    y = x.reshape(nseg, seg, w)[:, seg - 1:seg, :]
    return jnp.broadcast_to(y, (nseg, seg, w)).reshape(TILE, w)


def _hgrn_intra(q, k, v, b, seg, b_ref, k_ref, v_ref):
    sub = lax.broadcasted_iota(jnp.int32, (SUB, 1), 0)
    blocks = []
    for r0 in range(0, TILE, SUB):
        bb = b[r0:r0 + SUB]
        qb = q[r0:r0 + SUB]
        acc = jnp.zeros((SUB, DH), F32)
        for s in range(SUB):
            bs = jnp.broadcast_to(b_ref[r0 + s:r0 + s + 1], (SUB, DH))
            ks = jnp.broadcast_to(k_ref[r0 + s:r0 + s + 1], (SUB, DH))
            vs = jnp.broadcast_to(v_ref[r0 + s:r0 + s + 1], (SUB, DH))
            p = jnp.exp2(bb - bs) * (qb * ks)
            col = jnp.sum(p, axis=-1, keepdims=True)
            col = jnp.where(sub >= s, col, 0.0)
            acc = acc + col * vs
        blocks.append(acc)
        if r0 % (4 * SUB) == 3 * SUB:
            yield
    intra = jnp.concatenate(blocks, axis=0)

    if seg > SUB:
        ri = lax.broadcasted_iota(jnp.int32, (TILE, TILE), 0)
        ci = lax.broadcasted_iota(jnp.int32, (TILE, TILE), 1)
        a_off = jnp.zeros((TILE, TILE), F32)
        w = SUB
        while w < seg:
            zeros = jnp.zeros((w, DH), F32)
            qs, ks = [], []
            for r0 in range(0, TILE, 2 * w):
                ref = b[r0 + w - 1:r0 + w]
                ks += [k[r0:r0 + w] * jnp.exp2(ref - b[r0:r0 + w]), zeros]
                qs += [zeros, q[r0 + w:r0 + 2 * w] * jnp.exp2(b[r0 + w:r0 + 2 * w] - ref)]
            qt = jnp.concatenate(qs, axis=0).astype(BF16)
            kt = jnp.concatenate(ks, axis=0).astype(BF16)
            a_lvl = lax.dot_general(qt, kt, NT, preferred_element_type=F32)
            same_block = (ri // (2 * w)) == (ci // (2 * w))
            a_off = a_off + jnp.where(same_block, a_lvl, 0.0)
            w *= 2
        yield
        intra = intra + _bdot(a_off, v)
    return intra


def _mixer_core(proj_ref, gate_ref, lbl_ref, hgn_ref, mln_ref, gb_ref,
                s_in, c_in, s_out, c_out, n_all, m_rows, put_n, o_ref, b_sc, k_sc, seg):
    nseg = TILE // seg
    ri = lax.broadcasted_iota(jnp.int32, (TILE, TILE), 0)
    ci = lax.broadcasted_iota(jnp.int32, (TILE, TILE), 1)
    tri = ((ri // seg) == (ci // seg)) & (ci <= ri)
    tri_f = tri.astype(F32)
    lane = lax.broadcasted_iota(jnp.int32, (TILE, DH), 1)

    lg = lbl_ref[...]
    ex = jnp.exp(lg - jnp.max(lg, axis=0, keepdims=True))
    lb_all = ex[0:1] / jnp.sum(ex, axis=0, keepdims=True)

    ig_all = gate_ref[:, 0:DH] + gb_ref[:, 0:DH]
    lf_all = jax.nn.log_sigmoid(gate_ref[:, DH:2 * DH] + gb_ref[:, DH:2 * DH])
    f_all = lb_all + (1.0 - lb_all) * jax.nn.sigmoid(proj_ref[:, GW:2 * GW])
    cums = jnp.dot(tri_f, jnp.concatenate([jnp.log2(f_all), lf_all], axis=1), precision=HI,
                   preferred_element_type=F32)
    a_all = cums[:, GW:GW + DH]
    a_all_t = a_all.T
    ig_all_t = ig_all.T
    m_out = jnp.zeros((TILE, DH), F32)
    yield

    for h in range(HEADS):
        hs = slice(h * DH, (h + 1) * DH)

        q = proj_ref[:, 0 * GW + h * DH:0 * GW + (h + 1) * DH]
        v = proj_ref[:, 2 * GW + h * DH:2 * GW + (h + 1) * DH]
        og = proj_ref[:, 3 * GW + h * DH:3 * GW + (h + 1) * DH]
        k = 1.0 - f_all[:, hs]
        b = cums[:, hs]
        b_last = _seg_last(b, seg)
        qe = (q * jnp.exp2(b)).astype(BF16)
        kd = (k * jnp.exp2(b_last - b)).astype(BF16)
        vb16 = v.astype(BF16)
        e_last = jnp.exp2(b_last)
        b_sc[h] = b
        k_sc[h] = k
        yield
        out = yield from _hgrn_intra(q, k, v, b, seg, b_sc.at[h], k_sc.at[h],
                                     proj_ref.at[:, 2 * GW + h * DH:2 * GW + (h + 1) * DH])
        inter = []
        for sg in range(nseg):
            rows = slice(sg * seg, (sg + 1) * seg)
            st = s_in[sg, h]
            inter.append(jnp.dot(qe[rows], st.astype(BF16), preferred_element_type=F32))
            decay = jnp.broadcast_to(e_last[sg * seg:sg * seg + 1], (DH, DH)).T
            s_out[sg, h] = decay * st + lax.dot_general(kd[rows], vb16[rows], TN, preferred_element_type=F32)
        out = out + jnp.concatenate(inter, axis=0)
        out = _rms_mxu(out, hgn_ref[:, hs]) * (og * jax.nn.sigmoid(og))
        o_ref[:, hs] = out
        yield

        q = proj_ref[:, 4 * GW + h * DH:4 * GW + (h + 1) * DH]
        k = proj_ref[:, 5 * GW + h * DH:5 * GW + (h + 1) * DH] * (DH ** -0.5)
        v = proj_ref[:, 6 * GW + h * DH:6 * GW + (h + 1) * DH]
        og = proj_ref[:, 7 * GW + h * DH:7 * GW + (h + 1) * DH]
        a_col = jnp.broadcast_to(a_all[:, h:h + 1], (TILE, TILE))
        i_col = jnp.broadcast_to(ig_all[:, h:h + 1], (TILE, TILE))
        m_col = jnp.broadcast_to(m_rows[:, h:h + 1], (TILE, TILE))
        a_row = a_all_t[h:h + 1]
        i_row = ig_all_t[h:h + 1]
        log_d = jnp.where(tri, a_col - a_row + i_row, -jnp.inf)
        log_inter = a_col + m_col
        m_t = jnp.maximum(log_inter, jnp.max(log_d, axis=-1, keepdims=True))
        d_w = jnp.exp(log_d - m_t)
        w_i = jnp.exp(log_inter - m_t)
        qb16 = q.astype(BF16)
        kb16 = k.astype(BF16)
        vb16 = v.astype(BF16)
        s_w = lax.dot_general(qb16, kb16, NT, preferred_element_type=F32) * d_w
        nd = jnp.dot(s_w.astype(BF16), jnp.concatenate([vb16, jnp.ones((TILE, DH), BF16)], axis=1),
                     preferred_element_type=F32)
        num, den = nd[:, 0:DH], nd[:, DH:2 * DH]
        yield

        m_new = _seg_last(m_t, seg)
        a_end = _seg_last(a_col, seg)
        w_end = jnp.exp(a_end - a_col + i_col - m_new)
        f_end = jnp.exp(a_end + m_col - m_new)
        kw = k * w_end
        kw16 = kw.astype(BF16)
        qc, qn = [], []
        for sg in range(nseg):
            rows = slice(sg * seg, (sg + 1) * seg)
            last = sg * seg + seg - 1
            ct = c_in[sg, h]
            qc.append(jnp.dot(qb16[rows], ct.astype(BF16), preferred_element_type=F32))
            qn.append(q[rows] * n_all[sg:sg + 1, hs])
            fe = f_end[last:last + 1, 0:1]
            c_out[sg, h] = fe * ct + lax.dot_general(kw16[rows], vb16[rows], TN, preferred_element_type=F32)
            put_n(sg, hs, fe * n_all[sg:sg + 1, hs] + jnp.sum(kw[rows], axis=0, keepdims=True))
        num = w_i * jnp.concatenate(qc, axis=0) + num
        qn = jnp.dot(jnp.concatenate(qn, axis=0).astype(BF16), jnp.ones((DH, DH), BF16), preferred_element_type=F32)
        den = w_i * qn + den
        hout = num / jnp.maximum(jnp.abs(den), jnp.exp(-m_t))
        hout = _rms_mxu(hout, mln_ref[:, hs]) * jax.nn.sigmoid(og)
        o_ref[:, GW + h * DH:GW + (h + 1) * DH] = hout
        m_out = jnp.where(lane == h, m_t, m_out)
        yield
    return m_out


def _mixer_seg_kernel(proj_ref, gate_ref, lbl_ref, hgn_ref, mln_ref, gb_ref, s_ref, c_ref, n_ref, m_ref,
                      o_ref, so_ref, co_ref, no_ref, mo_ref, b_sc, k_sc, *, seg):
    def put_n(sg, hs, val):
        no_ref[sg:sg + 1, hs] = val

    m_out = _run(_mixer_core(proj_ref, gate_ref, lbl_ref, hgn_ref, mln_ref, gb_ref, s_ref, c_ref, so_ref, co_ref,
                             n_ref[...], m_ref[...], put_n, o_ref, b_sc, k_sc, seg))
    for sg in range(TILE // seg):
        last = sg * seg + seg - 1
        mo_ref[sg:sg + 1, :] = m_out[last:last + 1]


def _mixer_seg(proj, gates, lb_logits, hg_norm, ml_norm, gate_bias, s0, c0, n0, m0, batch, seq, to_cast=()):
    n = batch * seq
    assert TILE % seq == 0 and seq % SUB == 0 and n % TILE == 0
    nseg = TILE // seq
    row = lambda shape: pl.BlockSpec(shape, lambda i: (i, 0))
    st_spec = pl.BlockSpec((nseg, HEADS, DH, DH), lambda i: (i, 0, 0, 0))
    st_shape = jax.ShapeDtypeStruct((batch, HEADS, DH, DH), F32)
    m_rows = jnp.repeat(jnp.pad(m0, ((0, 0), (0, DH - HEADS))), seq, axis=0)
    o, s_new, c_new, n_new, m_new = pl.pallas_call(
        functools.partial(_mixer_seg_kernel, seg=seq),
        grid=(n // TILE,),
        in_specs=[row((TILE, 8 * GW)), row((TILE, 2 * DH)),
                  _resident((2, GW)), _resident((1, GW)), _resident((1, GW)), _resident((1, 2 * DH)),
                  st_spec, st_spec, row((nseg, GW)), row((TILE, DH))],
        out_specs=[row((TILE, 2 * GW)), st_spec, st_spec, row((nseg, GW)), row((nseg, DH))],
        out_shape=[jax.ShapeDtypeStruct((n, 2 * GW), F32), st_shape, st_shape,
                   jax.ShapeDtypeStruct((batch, GW), F32), jax.ShapeDtypeStruct((batch, DH), F32)],
        scratch_shapes=[pltpu.VMEM((HEADS, TILE, DH), F32), pltpu.VMEM((HEADS, TILE, DH), F32)],
        compiler_params=pltpu.CompilerParams(
            dimension_semantics=("arbitrary",), vmem_limit_bytes=VMEM_LIMIT),
        name="mixer_seg",
    )(proj, gates, lb_logits, hg_norm, ml_norm, gate_bias, s0, c0, n0.reshape(batch, GW), m_rows)
    return o, s_new, c_new, n_new.reshape(batch, HEADS, DH), m_new[:, :HEADS], []


SEQS_PER_STEP = 4


def _mixer_carry_kernel(proj_ref, gate_ref, lbl_ref, hgn_ref, mln_ref, gb_ref, s_ref, c_ref, n_ref, m_ref,
                        *rest, n_cast):
    cast_in, rest = rest[:n_cast], rest[n_cast:]
    (o_ref, so_ref, co_ref, no_ref, mo_ref), rest = rest[:5], rest[5:]
    cast_out, (b_sc, k_sc) = rest[:n_cast], rest[n_cast:]
    @pl.when(pl.program_id(0) == 0)
    def _():
        for src, dst in zip(cast_in, cast_out):
            dst[...] = src[...].astype(BF16)

    @pl.when(pl.program_id(1) == 0)
    def _():
        so_ref[...] = s_ref[...]
        co_ref[...] = c_ref[...]
        no_ref[...] = n_ref[...]
        mo_ref[...] = m_ref[...]

    tiles = []
    for j in range(SEQS_PER_STEP):
        def put_n(sg, hs, val, j=j):
            no_ref[j, :, hs] = val

        state, cell = so_ref.at[j:j + 1], co_ref.at[j:j + 1]
        tiles.append(_mixer_core(proj_ref.at[j], gate_ref.at[j], lbl_ref, hgn_ref, mln_ref, gb_ref,
                                 state, cell, state, cell, no_ref[j], jnp.broadcast_to(mo_ref[j], (TILE, DH)),
                                 put_n, o_ref.at[j], b_sc.at[j], k_sc.at[j], TILE))
    for j, m_out in enumerate(_interleave(*tiles)):
        mo_ref[j] = m_out[TILE - 1:TILE]


def _mixer_carry(proj, gates, lb_logits, hg_norm, ml_norm, gate_bias, s0, c0, n0, m0, batch, seq, to_cast=()):
    assert seq % TILE == 0 and batch % SEQS_PER_STEP == 0
    nb = SEQS_PER_STEP
    nt = seq // TILE
    assert all(w.shape[0] % (16 * nt) == 0 for w in to_cast)
    cast_specs = [pl.BlockSpec((w.shape[0] // nt, w.shape[1]), lambda b, t: (jnp.where(b == 0, t, nt - 1), 0))
                  for w in to_cast]
    tok = lambda w: pl.BlockSpec((nb, TILE, w), lambda b, t: (b, t, 0))
    st_spec = pl.BlockSpec((nb, HEADS, DH, DH), lambda b, t: (b, 0, 0, 0))
    n_spec = pl.BlockSpec((nb, 1, GW), lambda b, t: (b, 0, 0))
    m_spec = pl.BlockSpec((nb, 1, DH), lambda b, t: (b, 0, 0))
    st_shape = jax.ShapeDtypeStruct((batch, HEADS, DH, DH), F32)
    o, s_new, c_new, n_new, m_new, *cast = pl.pallas_call(
        functools.partial(_mixer_carry_kernel, n_cast=len(to_cast)),
        grid=(batch // nb, nt),
        in_specs=[tok(8 * GW), tok(2 * DH),
                  _resident((2, GW)), _resident((1, GW)), _resident((1, GW)), _resident((1, 2 * DH)),
                  st_spec, st_spec, n_spec, m_spec] + cast_specs,
        out_specs=[tok(2 * GW), st_spec, st_spec, n_spec, m_spec] + cast_specs,
        out_shape=[jax.ShapeDtypeStruct((batch, seq, 2 * GW), F32), st_shape, st_shape,
                   jax.ShapeDtypeStruct((batch, 1, GW), F32), jax.ShapeDtypeStruct((batch, 1, DH), F32)]
                  + [jax.ShapeDtypeStruct(w.shape, BF16) for w in to_cast],
        scratch_shapes=[pltpu.VMEM((nb, HEADS, TILE, DH), F32), pltpu.VMEM((nb, HEADS, TILE, DH), F32)],
        compiler_params=pltpu.CompilerParams(
            dimension_semantics=("arbitrary", "arbitrary"), vmem_limit_bytes=VMEM_LIMIT),
        name="mixer_carry",
    )(proj.reshape(batch, seq, 8 * GW), gates.reshape(batch, seq, 2 * DH), lb_logits, hg_norm, ml_norm, gate_bias,
      s0, c0, n0.reshape(batch, 1, GW), jnp.pad(m0, ((0, 0), (0, DH - HEADS))).reshape(batch, 1, DH), *to_cast)
    return o, s_new, c_new, n_new.reshape(batch, HEADS, DH), m_new.reshape(batch, DH)[:, :HEADS], cast


def _ffn_core(x, o, wo_ref, g2_ref, wg_ref, wv_ref, cw_ref, cb_ref, wd_ref, gf_ref, hist_ref, nb_ref, u_sc, g_sc, bb, tt):
    m = bb * tt
    x1 = x + _bdot(o, wo_ref[...])
    h2 = _rms(x1, g2_ref[...]).astype(BF16)
    tpos = lax.broadcasted_iota(jnp.int32, (m, 1), 0) & (tt - 1)
    for c in range(0, D_FF, FF_CHUNK):
        yield
        cs = slice(c, c + FF_CHUNK)
        u = jnp.dot(h2, wg_ref[:, cs], preferred_element_type=F32)
        val = jnp.dot(h2, wv_ref[:, cs], preferred_element_type=F32)
        u_sc[SUB:SUB + m, cs] = u
        u1 = u_sc[SUB - 1:SUB - 1 + m, cs]
        u2 = u_sc[SUB - 2:SUB - 2 + m, cs]
        if bb > 1:
            rows = lambda j0: jnp.concatenate(
                [jnp.broadcast_to(hist_ref[j:j + 1, j0 + c:j0 + c + FF_CHUNK], (tt, FF_CHUNK)) for j in range(bb)],
                axis=0)
            older, newer = rows(0), rows(D_FF)
            u1 = jnp.where(tpos == 0, newer, u1)
            u2 = jnp.where(tpos == 0, older, jnp.where(tpos == 1, newer, u2))
            last = u.reshape(bb, tt, FF_CHUNK)
            for j in range(CONV_W - 1):
                nb_ref[:, j, cs] = last[:, tt - (CONV_W - 1) + j, :]
        conv = cb_ref[:, cs] + cw_ref[0:1, cs] * u2 + cw_ref[1:2, cs] * u1 + cw_ref[2:3, cs] * u
        g_sc[:, cs] = (jax.nn.gelu(conv) * val).astype(BF16)
    if bb == 1:
        u_sc[SUB - 2:SUB, :] = u_sc[SUB + m - 2:SUB + m, :]
    yield
    y = x1 + jnp.dot(g_sc[...], wd_ref[...], preferred_element_type=F32)
    return _rms(y, gf_ref[...])


def _ffn_kernel(x_ref, o_ref, buf_ref, wo_ref, g2_ref, wg_ref, wv_ref, cw_ref, cb_ref, wd_ref, gf_ref,
                y_ref, nb_ref, u_sc, g_sc, *, bb, tt):
    m = bb * tt
    if bb > 1:
        u_sc[0:SUB, :] = jnp.zeros((SUB, D_FF), F32)
    else:
        @pl.when(pl.program_id(1) == 0)
        def _():
            for j in range(CONV_W - 1):
                u_sc[SUB - (CONV_W - 1) + j:SUB - (CONV_W - 1) + j + 1, :] = buf_ref[0, :, j * D_FF:(j + 1) * D_FF]

    y = _run(_ffn_core(x_ref[...].reshape(m, D_MODEL), o_ref[...].reshape(m, D_MODEL), wo_ref, g2_ref, wg_ref,
                       wv_ref, cw_ref, cb_ref, wd_ref, gf_ref, buf_ref, nb_ref, u_sc, g_sc, bb, tt))
    if bb == 1:
        nb_ref[0] = u_sc[SUB - (CONV_W - 1):SUB, :]
    y_ref[...] = y.reshape(bb, tt, D_MODEL)


def _ffn(x, o, buf, wo, g2, wg, wv, cw, cb, wd, gf, bb, tt):
    batch, seq, _ = x.shape
    assert batch % bb == 0 and seq % tt == 0 and tt % SUB == 0 and tt >= CONV_W - 1
    kern = functools.partial(_ffn_kernel, bb=bb, tt=tt)
    tok = pl.BlockSpec((bb, tt, D_MODEL), lambda b, t: (b, t, 0))
    assert tt & (tt - 1) == 0 and (bb == 1 or seq == tt)
    hw = (CONV_W - 1) * D_FF
    if bb == 1:
        hist, hist_shape = pl.BlockSpec((1, 1, hw), lambda b, t: (b, 0, 0)), (batch, 1, hw)
    else:
        hist, hist_shape = pl.BlockSpec((bb, hw), lambda b, t: (b, 0)), (batch, hw)
    hist_out = pl.BlockSpec((bb, CONV_W - 1, D_FF), lambda b, t: (b, 0, 0))
    return pl.pallas_call(
        kern,
        grid=(batch // bb, seq // tt),
        in_specs=[tok, tok, hist,
                  _resident((D_MODEL, D_MODEL)), _resident((1, D_MODEL)),
                  _resident((D_MODEL, D_FF)), _resident((D_MODEL, D_FF)),
                  _resident((CONV_W, D_FF)), _resident((1, D_FF)),
                  _resident((D_FF, D_MODEL)), _resident((1, D_MODEL))],
        out_specs=[tok, hist_out],
        out_shape=[jax.ShapeDtypeStruct((batch, seq, D_MODEL), F32),
                   jax.ShapeDtypeStruct((batch, CONV_W - 1, D_FF), F32)],
        scratch_shapes=[pltpu.VMEM((SUB + bb * tt, D_FF), F32), pltpu.VMEM((bb * tt, D_FF), BF16)],
        compiler_params=pltpu.CompilerParams(
            dimension_semantics=("arbitrary", "arbitrary"), vmem_limit_bytes=VMEM_LIMIT),
        name="ffn",
    )(x, o, buf.reshape(hist_shape), wo, g2, wg, wv, cw, cb, wd, gf)


ROWS_PER_STEP = 512


FFN_WEIGHTS = ("wo", "wg", "wv", "wd")


def _layer(x, s0, c0, n0, m0, buf0, p):
    batch, seq, _ = x.shape
    n = batch * seq
    tm = ROWS_PER_STEP if n >= 8 * ROWS_PER_STEP else ROWS_PER_STEP // 2
    proj, gates = _inproj(x.reshape(n, D_MODEL), p["g1"], p["w_all"], p["w_gate2"], tm)
    mixer = _mixer_carry if seq >= TILE else _mixer_seg
    pending = () if FFN_WEIGHTS[0] in p else tuple(p["ffn_f32"])
    o, s_new, c_new, n_new, m_new, cast = mixer(proj, gates, p["lb_logits"], p["hg_norm"], p["ml_norm"],
                                                p["gate_bias"], s0, c0, n0, m0, batch, seq, pending)
    if pending:
        p = {**p, **dict(zip(FFN_WEIGHTS, cast or [w.astype(BF16) for w in pending]))}
    tt = min(seq, ROWS_PER_STEP)
    bb = ROWS_PER_STEP // tt
    y, buf_new = _ffn(x, o.reshape(batch, seq, D_MODEL), buf0, p["wo"], p["g2"], p["wg"], p["wv"],
                      p["cw"], p["cb"], p["wd"], p["gf"], bb, tt)
    return (y, s_new[None], c_new[None], n_new[None], m_new[None], buf_new[None]), p


def kernel(x_prompt, x_sample, state_hgrn_S, state_mlstm_C, state_mlstm_n, state_mlstm_m, state_conv, norm1_g, w_in, hg_lb_logits, hg_norm_g, ml_b_ig, ml_b_fg, ml_norm_g, w_out, norm2_g, w_gate, w_val, conv_w, conv_b, w_down, final_norm_g):
    assert norm1_g.shape[0] == 1, "single-layer trunk"
    w = w_in[0]
    gate_cols = w[:, 8 * GW:]
    zpad = jnp.zeros((D_MODEL, DH - HEADS), w.dtype)
    w_gate2 = jnp.concatenate([gate_cols[:, :HEADS], zpad, gate_cols[:, HEADS:], zpad], axis=1)
    bpad = jnp.zeros((DH - HEADS,), F32)
    p = {
        "g1": norm1_g, "w_all": w.astype(BF16), "w_gate2": w_gate2.astype(BF16),
        "lb_logits": hg_lb_logits, "hg_norm": hg_norm_g, "ml_norm": ml_norm_g,
        "gate_bias": jnp.concatenate([ml_b_ig[0], bpad, ml_b_fg[0], bpad])[None],
        "ffn_f32": (w_out[0], w_gate[0], w_val[0], w_down[0]), "g2": norm2_g, "cw": conv_w[0], "cb": conv_b,
        "gf": final_norm_g[None],
    }
    b = x_prompt.shape[0]
    zs = jnp.zeros((b, HEADS, DH, DH), F32)
    prompt, p = _layer(x_prompt, zs, zs, jnp.zeros((b, HEADS, DH), F32), jnp.zeros((b, HEADS), F32),
                       jnp.zeros((b, CONV_W - 1, D_FF), F32), p)
    sample, p = _layer(x_sample, state_hgrn_S[0], state_mlstm_C[0], state_mlstm_n[0], state_mlstm_m[0],
                    state_conv[0], p)
    out = []
    for a, c in zip(prompt, sample):
        out += [a, c]
    return tuple(out)
```
